```python
import math
import jax, jax.numpy as jnp
from jax import lax
import numpy as np

D_MODEL = 1024
BATCH = 16
SEQ = 2048
DEPTH = 2

HEAD_DIM = 64
SSM_WIDTH = D_MODEL // 4
SSM_GROUP_CH = 16
SSM_GROUPS = SSM_WIDTH // SSM_GROUP_CH
SSM_STATE = 64
DT_MIN = 1e-3
DT_MAX = 1e-1
SWA_HEADS = (3 * D_MODEL // 8) // HEAD_DIM
SWA_KV_HEADS = 2
SWA_WINDOW = 128
MOBA_HEADS = (3 * D_MODEL // 8) // HEAD_DIM
MOBA_KV_HEADS = 2
MOBA_BLOCK = 256
MOBA_TOPK = 3
MOBA_Q_CHUNK = 32
D_MIX = SSM_WIDTH + (SWA_HEADS + MOBA_HEADS) * HEAD_DIM
PROJ_SIZES = (
    SSM_WIDTH, SSM_WIDTH,
    SWA_HEADS * HEAD_DIM, SWA_KV_HEADS * HEAD_DIM, SWA_KV_HEADS * HEAD_DIM, SWA_HEADS * HEAD_DIM,
    MOBA_HEADS * HEAD_DIM, MOBA_KV_HEADS * HEAD_DIM, MOBA_KV_HEADS * HEAD_DIM, MOBA_HEADS * HEAD_DIM,
)
D_PROJ = sum(PROJ_SIZES)
RMS_EPS = 1e-6

kernel_name = 'hybrid_s5_swa_moba_block'


def rms_norm(x, g):
    xf = x.astype(jnp.float32)
    y = xf * lax.rsqrt(jnp.mean(xf * xf, axis=-1, keepdims=True) + RMS_EPS)
    return (y * g.astype(jnp.float32)).astype(x.dtype)


def s5_mixer(u, lam_re, lam_im, log_dt, b_re, b_im, c_re, c_im, d, glu_w, glu_b):
    bsz, seqlen, width = u.shape
    f32 = jnp.float32
    uf = u.astype(f32).reshape(bsz, seqlen, SSM_GROUPS, SSM_GROUP_CH)
    dt = jnp.exp(log_dt.astype(f32))[:, None]
    lr = lam_re.astype(f32)
    li = lam_im.astype(f32)
    mag = jnp.exp(lr * dt)
    abar_re = mag * jnp.cos(li * dt)
    abar_im = mag * jnp.sin(li * dt)
    den = lr * lr + li * li
    nr = abar_re - 1.0
    ni = abar_im
    cr = (nr * lr + ni * li) / den
    ci = (ni * lr - nr * li) / den
    br = b_re.astype(f32)
    bi = b_im.astype(f32)
    bbar_re = cr[..., None] * br - ci[..., None] * bi
    bbar_im = cr[..., None] * bi + ci[..., None] * br
    bu_re = jnp.einsum('blgh,gph->blgp', uf, bbar_re)
    bu_im = jnp.einsum('blgh,gph->blgp', uf, bbar_im)
    a_re = jnp.broadcast_to(abar_re, bu_re.shape)
    a_im = jnp.broadcast_to(abar_im, bu_im.shape)

    def combine(e1, e2):
        a1r, a1i, b1r, b1i = e1
        a2r, a2i, b2r, b2i = e2
        return (a2r * a1r - a2i * a1i,
                a2r * a1i + a2i * a1r,
                a2r * b1r - a2i * b1i + b2r,
                a2r * b1i + a2i * b1r + b2i)

    _, _, xs_re, xs_im = lax.associative_scan(combine, (a_re, a_im, bu_re, bu_im), axis=1)
    y = (jnp.einsum('blgp,ghp->blgh', xs_re, c_re.astype(f32))
         - jnp.einsum('blgp,ghp->blgh', xs_im, c_im.astype(f32)))
    y = y.reshape(bsz, seqlen, width) + d.astype(f32) * u.astype(f32)
    y = jax.nn.gelu(y)
    y = y * jax.nn.sigmoid(jnp.einsum('ble,ef->blf', y, glu_w.astype(f32)) + glu_b.astype(f32))
    return y.astype(u.dtype)


def swa_attention(q, k, v, sink):
    bsz, seqlen, n_q, hd = q.shape
    n_kv = k.shape[2]
    grp = n_q // n_kv
    w = SWA_WINDOW
    nblk = seqlen // w
    qb = q.reshape(bsz, nblk, w, n_kv, grp, hd)

    def band(a):
        prev = jnp.pad(a, ((0, 0), (w, 0), (0, 0), (0, 0)))[:, :seqlen]
        return jnp.concatenate([prev.reshape(bsz, nblk, w, n_kv, hd),
                                a.reshape(bsz, nblk, w, n_kv, hd)], axis=2)

    kb = band(k)
    vb = band(v)
    s = jnp.einsum('bnqhgd,bnshd->bnhgqs', qb, kb).astype(jnp.float32) * (hd ** -0.5)
    qi = jnp.arange(w)[:, None]
    si = jnp.arange(2 * w)[None, :]
    rel = qi + w - si
    in_win = (rel >= 0) & (rel < w)
    blk = jnp.arange(nblk)[:, None, None]
    mask = in_win[None] & ((blk > 0) | (si[None] >= w))
    s = jnp.where(mask[None, :, None, None], s, -jnp.inf)
    sink_l = jnp.broadcast_to(sink.astype(jnp.float32).reshape(1, 1, n_kv, grp, 1, 1),
                              s.shape[:-1] + (1,))
    p = jax.nn.softmax(jnp.concatenate([s, sink_l], axis=-1), axis=-1)[..., :-1]
    o = jnp.einsum('bnhgqs,bnshd->bnqhgd', p.astype(vb.dtype), vb)
    return o.reshape(bsz, seqlen, n_q * hd)


def moba_attention(q, k, v):
    bsz, seqlen, n_q, hd = q.shape
    n_kv = k.shape[2]
    grp = n_q // n_kv
    f32 = jnp.float32
    nb = -(-seqlen // MOBA_BLOCK)
    padded = nb * MOBA_BLOCK
    pad = ((0, 0), (0, padded - seqlen), (0, 0), (0, 0))
    kp = jnp.pad(k, pad).reshape(bsz, nb, MOBA_BLOCK, n_kv, hd)
    vp = jnp.pad(v, pad).reshape(bsz, nb, MOBA_BLOCK, n_kv, hd)
    k_mean = jnp.mean(kp.astype(f32), axis=2)
    kb = kp.transpose(0, 3, 1, 2, 4)
    vb = vp.transpose(0, 3, 1, 2, 4)
    qg = q.reshape(bsz, seqlen, n_kv, grp, hd)
    pos = jnp.arange(seqlen, dtype=jnp.int32)
    q_blk = pos // MOBA_BLOCK
    gate = jnp.einsum('blhgd,bnhd->blhn', qg.astype(f32), k_mean)
    fully_past = jnp.arange(nb, dtype=jnp.int32)[None, :] < q_blk[:, None]
    gate = jnp.where(fully_past[None, :, None, :], gate, -jnp.inf)
    n_sel = min(MOBA_TOPK, nb)
    _, sel = lax.top_k(gate, n_sel)
    sel = sel.astype(jnp.int32)
    sel_ok = sel < q_blk[None, :, None, None]
    own = jnp.broadcast_to(q_blk[None, :, None, None], (bsz, seqlen, n_kv, 1))
    idx = jnp.concatenate([sel, own], axis=-1)
    slot_ok = jnp.concatenate([sel_ok, jnp.ones(own.shape, dtype=bool)], axis=-1)
    n_chunks = seqlen // MOBA_Q_CHUNK

    def to_chunks(a):
        return jnp.moveaxis(a.reshape((bsz, n_chunks, MOBA_Q_CHUNK) + a.shape[2:]), 1, 0)

    b_ix = jnp.arange(bsz)[:, None, None, None]
    h_ix = jnp.arange(n_kv)[None, None, :, None]
    blk_off = jnp.arange(MOBA_BLOCK, dtype=jnp.int32)
    scale = hd ** -0.5

    def attend_chunk(args):
        qc, ic, oc, pc = args
        kg = kb[b_ix, h_ix, ic]
        vg = vb[b_ix, h_ix, ic]
        s = jnp.einsum('bchgd,bchnkd->bchgnk', qc, kg).astype(f32) * scale
        key_pos = ic[..., None] * MOBA_BLOCK + blk_off
        ok = oc[..., None] & (key_pos <= pc[None, :, None, None, None])
        s = jnp.where(ok[:, :, :, None], s, -jnp.inf)
        shp = s.shape
        p = jax.nn.softmax(s.reshape(shp[:4] + (-1,)), axis=-1).reshape(shp)
        return jnp.einsum('bchgnk,bchnkd->bchgd', p.astype(vg.dtype), vg)

    out = lax.map(attend_chunk, (to_chunks(qg), to_chunks(idx), to_chunks(slot_ok),
                                 pos.reshape(n_chunks, MOBA_Q_CHUNK)))
    return jnp.moveaxis(out, 0, 1).reshape(bsz, seqlen, n_q * hd)


def hybrid_layer(x, norm_g, w_in, lam_re, lam_im, log_dt, b_re, b_im, c_re, c_im, d,
                 glu_w, glu_b, swa_q_norm, swa_k_norm, swa_sink, moba_q_norm, moba_k_norm, w_out):
    bsz, seqlen, _ = x.shape
    h = rms_norm(x, norm_g)
    proj = jnp.einsum('bld,de->ble', h, w_in)
    parts = []
    off = 0
    for n in PROJ_SIZES:
        parts.append(proj[..., off:off + n])
        off += n
    s_u, s_g, a_q, a_k, a_v, a_g, m_q, m_k, m_v, m_g = parts

    def heads(t, n):
        return t.reshape(bsz, seqlen, n, HEAD_DIM)

    y_ssm = s5_mixer(s_u, lam_re, lam_im, log_dt, b_re, b_im, c_re, c_im, d, glu_w, glu_b) * jax.nn.silu(s_g)
    aq = rms_norm(heads(a_q, SWA_HEADS), swa_q_norm)
    ak = rms_norm(heads(a_k, SWA_KV_HEADS), swa_k_norm)
    y_swa = swa_attention(aq, ak, heads(a_v, SWA_KV_HEADS), swa_sink) * jax.nn.silu(a_g)
    mq = rms_norm(heads(m_q, MOBA_HEADS), moba_q_norm)
    mk = rms_norm(heads(m_k, MOBA_KV_HEADS), moba_k_norm)
    y_moba = moba_attention(mq, mk, heads(m_v, MOBA_KV_HEADS)) * jax.nn.silu(m_g)
    y = jnp.concatenate([y_ssm, y_swa.astype(y_ssm.dtype), y_moba.astype(y_ssm.dtype)], axis=-1)
    return x + jnp.einsum('ble,ed->bld', y, w_out).astype(x.dtype)


def setup_inputs(seed: int = 0) -> dict:
    key = jax.random.key(seed)
    ks = jax.random.split(key, 19)
    f32 = jnp.float32
    nrm = jax.random.normal
    G, P, H = SSM_GROUPS, SSM_STATE, SSM_GROUP_CH
    n_idx = jnp.arange(P, dtype=f32)
    return {
        'x': nrm(ks[0], (BATCH, SEQ, D_MODEL), f32),
        'norm_g': 1.0 + 0.02 * nrm(ks[1], (DEPTH, D_MODEL), f32),
        'w_in': nrm(ks[2], (DEPTH, D_MODEL, D_PROJ), f32) * D_MODEL ** -0.5,
        'ssm_lam_re': -0.5 + 0.01 * nrm(ks[3], (DEPTH, G, P), f32),
        'ssm_lam_im': math.pi * n_idx + 0.01 * nrm(ks[4], (DEPTH, G, P), f32),
        'ssm_log_dt': jax.random.uniform(ks[5], (DEPTH, G), f32, math.log(DT_MIN), math.log(DT_MAX)),
        'ssm_b_re': nrm(ks[6], (DEPTH, G, P, H), f32) * (2 * H) ** -0.5,
        'ssm_b_im': nrm(ks[7], (DEPTH, G, P, H), f32) * (2 * H) ** -0.5,
        'ssm_c_re': nrm(ks[8], (DEPTH, G, H, P), f32) * P ** -0.5,
        'ssm_c_im': nrm(ks[9], (DEPTH, G, H, P), f32) * P ** -0.5,
        'ssm_d': nrm(ks[10], (DEPTH, SSM_WIDTH), f32),
        'ssm_glu_w': nrm(ks[11], (DEPTH, SSM_WIDTH, SSM_WIDTH), f32) * SSM_WIDTH ** -0.5,
        'ssm_glu_b': 0.02 * nrm(ks[12], (DEPTH, SSM_WIDTH), f32),
        'swa_q_norm': 1.0 + 0.02 * nrm(ks[13], (DEPTH, HEAD_DIM), f32),
        'swa_k_norm': 1.0 + 0.02 * nrm(ks[14], (DEPTH, HEAD_DIM), f32),
        'swa_sink': 0.5 * nrm(ks[15], (DEPTH, SWA_HEADS), f32),
        'moba_q_norm': 1.0 + 0.02 * nrm(ks[16], (DEPTH, HEAD_DIM), f32),
        'moba_k_norm': 1.0 + 0.02 * nrm(ks[17], (DEPTH, HEAD_DIM), f32),
        'w_out': nrm(ks[18], (DEPTH, D_MIX, D_MODEL), f32) * D_MIX ** -0.5,
    }


def reference(x, norm_g, w_in, ssm_lam_re, ssm_lam_im, ssm_log_dt, ssm_b_re, ssm_b_im,
              ssm_c_re, ssm_c_im, ssm_d, ssm_glu_w, ssm_glu_b, swa_q_norm, swa_k_norm,
              swa_sink, moba_q_norm, moba_k_norm, w_out):
    for l in range(DEPTH):
        x = hybrid_layer(x, norm_g[l], w_in[l], ssm_lam_re[l], ssm_lam_im[l], ssm_log_dt[l],
                         ssm_b_re[l], ssm_b_im[l], ssm_c_re[l], ssm_c_im[l], ssm_d[l],
                         ssm_glu_w[l], ssm_glu_b[l], swa_q_norm[l], swa_k_norm[l], swa_sink[l],
                         moba_q_norm[l], moba_k_norm[l], w_out[l])
    return x
```

```python
import functools
import math

import jax
import jax.numpy as jnp
from jax import lax
from jax.experimental import pallas as pl
from jax.experimental.pallas import tpu as pltpu

F32 = jnp.float32
BF16 = jnp.bfloat16

HEAD_DIM = 64
SSM_WIDTH = 256
SSM_GROUP_CH = 16
SSM_GROUPS = 16
SSM_STATE = 64
N_STATE = SSM_GROUPS * SSM_STATE
N_Q_HEADS = 6
N_KV_HEADS = 2
Q_WIDTH = N_Q_HEADS * HEAD_DIM
KV_WIDTH = N_KV_HEADS * HEAD_DIM
SWA_WINDOW = 128
MOBA_BLOCK = 256
MOBA_TOPK = 3
RMS_EPS = 1e-6
ATTN_SCALE = HEAD_DIM ** -0.5
LANES = 128
HEAD_ORDER = (0, 3, 1, 4, 2, 5)
N_Q_TILES = Q_WIDTH // LANES
VMEM_LIMIT = 56 * 1024 * 1024


def _cparams(*sem):
    return pltpu.CompilerParams(dimension_semantics=sem, vmem_limit_bytes=VMEM_LIMIT)


def _ssm_prep_kernel(lr_ref, li_ref, ldt_ref, br_ref, bi_ref,
                     are_ref, aim_ref, bbr_ref, bbi_ref):
    lr = lr_ref[...]
    li = li_ref[...]
    dt = jnp.exp(ldt_ref[...])
    mag = jnp.exp(lr * dt)
    a_re = mag * jnp.cos(li * dt)
    a_im = mag * jnp.sin(li * dt)
    den = lr * lr + li * li
    nr = a_re - 1.0
    ni = a_im
    cr = (nr * lr + ni * li) / den
    ci = (ni * lr - nr * li) / den
    br = br_ref[...]
    bi = bi_ref[...]
    are_ref[...] = a_re
    aim_ref[...] = a_im
    bbr_ref[...] = cr * br - ci * bi
    bbi_ref[...] = cr * bi + ci * br


def _ssm_prep(lam_re, lam_im, log_dt, b_re, b_im):
    d, g, p = lam_re.shape
    h = b_re.shape[-1]
    shp = (d * g, h, p)
    bc = lambda a: jnp.broadcast_to(a.reshape(d * g, 1, -1), shp).astype(F32)
    tr = lambda a: jnp.swapaxes(a, -1, -2).reshape(shp).astype(F32)
    outs = pl.pallas_call(
        _ssm_prep_kernel,
        out_shape=[jax.ShapeDtypeStruct(shp, F32)] * 4,
        name="ssm_prep",
    )(bc(lam_re), bc(lam_im), bc(log_dt[..., None]), tr(b_re), tr(b_im))
    a_re, a_im, bb_re, bb_im = outs
    return (a_re[:, 0, :].reshape(d, g, p), a_im[:, 0, :].reshape(d, g, p),
            bb_re.reshape(d, g, h, p), bb_im.reshape(d, g, h, p))


def _inproj_kernel(x_ref, g_ref, w_ref, u_ref, sg_ref, aq_ref, ag_ref, mq_ref, mg_ref,
                   akv_ref, mkv_ref):
    x = x_ref[0]
    ms = jnp.mean(x * x, axis=-1, keepdims=True)
    h = (x * lax.rsqrt(ms + RMS_EPS) * g_ref[...]).astype(BF16)

    def mm(lo, width):
        return jnp.dot(h, w_ref[:, lo:lo + width], preferred_element_type=F32)

    u_ref[...] = mm(0, SSM_WIDTH)
    sg_ref[...] = mm(SSM_WIDTH, SSM_WIDTH)
    off = 2 * SSM_WIDTH
    aq_ref[0] = mm(off, Q_WIDTH)
    ag_ref[0] = mm(off + Q_WIDTH, Q_WIDTH)
    mq_ref[0] = mm(off + 2 * Q_WIDTH, Q_WIDTH)
    mg_ref[0] = mm(off + 3 * Q_WIDTH, Q_WIDTH)
    off += 4 * Q_WIDTH
    akv_ref[0] = mm(off, 2 * KV_WIDTH)
    mkv_ref[0] = mm(off + 2 * KV_WIDTH, 2 * KV_WIDTH)


def _inproj(x, norm_g, w_perm, tl):
    b, l, d = x.shape
    nt = l // tl
    dp = w_perm.shape[1]
    tm = jax.ShapeDtypeStruct((l, b * SSM_WIDTH), F32)
    bq = jax.ShapeDtypeStruct((b, l, Q_WIDTH), F32)
    bkv = jax.ShapeDtypeStruct((b, l, 2 * KV_WIDTH), F32)
    tm_spec = pl.BlockSpec((tl, SSM_WIDTH), lambda i, t: (t, i))
    q_spec = pl.BlockSpec((1, tl, Q_WIDTH), lambda i, t: (i, t, 0))
    kv_spec = pl.BlockSpec((1, tl, 2 * KV_WIDTH), lambda i, t: (i, t, 0))
    return pl.pallas_call(
        _inproj_kernel,
        grid=(b, nt),
        in_specs=[pl.BlockSpec((1, tl, d), lambda i, t: (i, t, 0)),
                  pl.BlockSpec((1, d), lambda i, t: (0, 0)),
                  pl.BlockSpec((d, dp), lambda i, t: (0, 0))],
        out_specs=[tm_spec, tm_spec, q_spec, q_spec, q_spec, q_spec, kv_spec, kv_spec],
        out_shape=[tm, tm, bq, bq, bq, bq, bkv, bkv],
        compiler_params=_cparams("parallel", "parallel"),
        name="inproj",
    )(x, norm_g.reshape(1, d), w_perm)


def _ssm_kernel(nb, tc, u_ref, sg_ref, bmat_ref, are_ref, aim_ref, cre_ref, cim_ref,
                d_ref, gw_ref, gb_ref, y_ref, bu_ref, sre_ref, sim_ref):
    @pl.when(pl.program_id(0) == 0)
    def _():
        sre_ref[...] = jnp.zeros_like(sre_ref)
        sim_ref[...] = jnp.zeros_like(sim_ref)

    u = u_ref[...]
    bu_ref[...] = jnp.dot(u.astype(BF16), bmat_ref[...], preferred_element_type=F32)
    a_re = jnp.broadcast_to(are_ref[...], (nb, N_STATE))
    a_im = jnp.broadcast_to(aim_ref[...], (nb, N_STATE))

    def step(t, carry):
        xr, xi = carry
        r0 = pl.multiple_of(t * nb, nb)
        br = bu_ref[pl.ds(r0, nb), 0:N_STATE]
        bi = bu_ref[pl.ds(r0, nb), N_STATE:2 * N_STATE]
        nr = a_re * xr - a_im * xi + br
        ni = a_re * xi + a_im * xr + bi
        bu_ref[pl.ds(r0, nb), 0:N_STATE] = nr
        bu_ref[pl.ds(r0, nb), N_STATE:2 * N_STATE] = ni
        return nr, ni

    xr, xi = lax.fori_loop(0, tc, step, (sre_ref[...], sim_ref[...]))
    sre_ref[...] = xr
    sim_ref[...] = xi

    xs_re = bu_ref[:, 0:N_STATE].astype(BF16)
    xs_im = bu_ref[:, N_STATE:2 * N_STATE].astype(BF16)
    y = (jnp.dot(xs_re, cre_ref[...], preferred_element_type=F32)
         - jnp.dot(xs_im, cim_ref[...], preferred_element_type=F32))
    y = y + d_ref[...] * u
    y = jax.nn.gelu(y)
    z = jnp.dot(y.astype(BF16), gw_ref[...], preferred_element_type=F32) + gb_ref[...]
    y = y * jax.nn.sigmoid(z)
    sg = sg_ref[...]
    y_ref[...] = y * (sg * jax.nn.sigmoid(sg))


def _ssm(u_rows, sg_rows, bmat, a_re, a_im, c_re, c_im, d, glu_w, glu_b, nb, tc):
    rows = u_rows.shape[0]
    blk = tc * nb
    row_spec = pl.BlockSpec((blk, SSM_WIDTH), lambda i: (i, 0))
    full = lambda a: pl.BlockSpec(a.shape, lambda i: (0,) * a.ndim)
    args = (bmat, a_re, a_im, c_re, c_im, d, glu_w, glu_b)
    return pl.pallas_call(
        functools.partial(_ssm_kernel, nb, tc),
        grid=(rows // blk,),
        in_specs=[row_spec, row_spec] + [full(a) for a in args],
        out_specs=row_spec,
        out_shape=jax.ShapeDtypeStruct((rows, SSM_WIDTH), F32),
        scratch_shapes=[pltpu.VMEM((blk, 2 * N_STATE), F32),
                        pltpu.VMEM((nb, N_STATE), F32),
                        pltpu.VMEM((nb, N_STATE), F32)],
        compiler_params=_cparams("arbitrary"),
        name="ssm",
    )(u_rows, sg_rows, *args)


def _pair_rms_norm(t, gain2):
    lane = lax.broadcasted_iota(jnp.int32, t.shape, 1)
    lo = lane < HEAD_DIM
    sq = t * t
    s_lo = jnp.sum(jnp.where(lo, sq, 0.0), axis=-1, keepdims=True)
    s_hi = jnp.sum(jnp.where(lo, 0.0, sq), axis=-1, keepdims=True)
    r_lo = lax.rsqrt(s_lo * (1.0 / HEAD_DIM) + RMS_EPS)
    r_hi = lax.rsqrt(s_hi * (1.0 / HEAD_DIM) + RMS_EPS)
    return t * jnp.where(lo, r_lo, r_hi) * gain2


def _half_select(t, kv_head):
    lane = lax.broadcasted_iota(jnp.int32, t.shape, 1)
    keep = (lane < HEAD_DIM) if kv_head == 0 else (lane >= HEAD_DIM)
    return jnp.where(keep, t, 0.0)


def _nt_dot(a, b):
    return lax.dot_general(a, b, (((1,), (1,)), ((), ())), preferred_element_type=F32)


def _swa_kernel(sink_ref, q_ref, g_ref, kvp_ref, kvc_ref, qn_ref, kn_ref, o_ref):
    n = pl.program_id(1)
    w = SWA_WINDOW
    kv = jnp.concatenate([kvp_ref[0], kvc_ref[0]], axis=0)
    k = _pair_rms_norm(kv[:, 0:KV_WIDTH], kn_ref[...]).astype(BF16)
    v_t = jnp.transpose(kv[:, KV_WIDTH:2 * KV_WIDTH]).astype(BF16)

    si = lax.broadcasted_iota(jnp.int32, (2 * w, w), 0)
    qi = lax.broadcasted_iota(jnp.int32, (2 * w, w), 1)
    rel = qi + w - si
    first_key = jnp.where(n > 0, 0, w)
    mask = (rel >= 0) & (rel < w) & (si >= first_key)
    row = lax.broadcasted_iota(jnp.int32, (KV_WIDTH, w), 0)

    for j in range(N_Q_TILES):
        q = _pair_rms_norm(q_ref[0, :, j * LANES:(j + 1) * LANES], qn_ref[...]) * ATTN_SCALE
        halves = []
        for h in range(N_KV_HEADS):
            sink = sink_ref[0, j + N_Q_TILES * h]
            qh = _half_select(q, h).astype(BF16)
            s = jnp.where(mask, _nt_dot(k, qh), -jnp.inf)
            m = jnp.maximum(jnp.max(s, axis=0, keepdims=True), sink)
            p = jnp.exp(s - m)
            den = jnp.sum(p, axis=0, keepdims=True) + jnp.exp(sink - m)
            o_t = jnp.dot(v_t, p.astype(BF16), preferred_element_type=F32)
            halves.append(o_t / den)
        o_t = jnp.where(row < HEAD_DIM, halves[0], halves[1])
        g = g_ref[0, :, j * LANES:(j + 1) * LANES]
        o_ref[0, :, j * LANES:(j + 1) * LANES] = jnp.transpose(o_t) * (g * jax.nn.sigmoid(g))


def _swa(q, g, kv, sink, q_norm2, k_norm2):
    b, l, _ = q.shape
    w = SWA_WINDOW
    qspec = pl.BlockSpec((1, w, Q_WIDTH), lambda i, n: (i, n, 0))
    vec = pl.BlockSpec((1, LANES), lambda i, n: (0, 0))
    return pl.pallas_call(
        _swa_kernel,
        grid=(b, l // w),
        in_specs=[pl.BlockSpec(memory_space=pltpu.SMEM),
                  qspec, qspec,
                  pl.BlockSpec((1, w, 2 * KV_WIDTH), lambda i, n: (i, jnp.maximum(n - 1, 0), 0)),
                  pl.BlockSpec((1, w, 2 * KV_WIDTH), lambda i, n: (i, n, 0)),
                  vec, vec],
        out_specs=qspec,
        out_shape=jax.ShapeDtypeStruct((b, l, Q_WIDTH), F32),
        compiler_params=_cparams("parallel", "parallel"),
        name="swa",
    )(sink, q, g, kv, kv, q_norm2, k_norm2)


def _moba_kernel(nblk, q_ref, g_ref, kv_ref, qn_ref, kn_ref, o_ref,
                 kn_s, vt_s, kmean_s, qs_s, sel_s, m_s, l_s, acc_s):
    i = pl.program_id(1)
    blk = MOBA_BLOCK

    @pl.when(i == 0)
    def _():
        for n in range(nblk):
            kvb = kv_ref[0, n * blk:(n + 1) * blk, :]
            kf = _pair_rms_norm(kvb[:, 0:KV_WIDTH], kn_ref[...])
            kn_s[n] = kf.astype(BF16)
            kmean_s[n:n + 1, :] = jnp.mean(kf, axis=0, keepdims=True)
            vt_s[n] = jnp.transpose(kvb[:, KV_WIDTH:2 * KV_WIDTH]).astype(BF16)

    qsum = jnp.zeros((blk, LANES), F32)
    for j in range(N_Q_TILES):
        qf = _pair_rms_norm(q_ref[0, :, j * LANES:(j + 1) * LANES], qn_ref[...])
        qsum = qsum + qf
        for h in range(N_KV_HEADS):
            qs_s[j * N_KV_HEADS + h] = (_half_select(qf, h) * ATTN_SCALE).astype(BF16)

    blk_id = lax.broadcasted_iota(jnp.int32, (nblk, blk), 0)
    past = blk_id < i
    kmean = kmean_s[...]
    for h in range(N_KV_HEADS):
        gate = lax.dot_general(_half_select(kmean, h), _half_select(qsum, h),
                               (((1,), (1,)), ((), ())),
                               precision=lax.Precision.HIGHEST,
                               preferred_element_type=F32)
        gate = jnp.where(past, gate, -jnp.inf)
        rank = jnp.zeros((nblk, blk), jnp.int32)
        for n2 in range(nblk):
            other = gate[n2:n2 + 1, :]
            ahead = (other > gate) | ((other == gate) & (n2 < blk_id))
            rank = rank + ahead.astype(jnp.int32)
        sel_s[h] = jnp.where((rank < MOBA_TOPK) & past, 1.0, 0.0)

    ki = lax.broadcasted_iota(jnp.int32, (blk, blk), 0)
    qi = lax.broadcasted_iota(jnp.int32, (blk, blk), 1)
    causal = ki <= qi
    row = lax.broadcasted_iota(jnp.int32, (KV_WIDTH, blk), 0)

    k_own = kn_s[i]
    v_own = vt_s[i]
    for j in range(N_Q_TILES):
        parts = []
        for h in range(N_KV_HEADS):
            c = j * N_KV_HEADS + h
            s = jnp.where(causal, _nt_dot(k_own, qs_s[c]), -jnp.inf)
            m = jnp.max(s, axis=0, keepdims=True)
            p = jnp.exp(s - m)
            m_s[c] = m
            l_s[c] = jnp.sum(p, axis=0, keepdims=True)
            parts.append(jnp.dot(v_own, p.astype(BF16), preferred_element_type=F32))
        acc_s[j] = jnp.where(row < HEAD_DIM, parts[0], parts[1])

    def past_block(n, carry):
        k_n = kn_s[n]
        v_n = vt_s[n]
        for j in range(N_Q_TILES):
            parts = []
            alphas = []
            for h in range(N_KV_HEADS):
                c = j * N_KV_HEADS + h
                ok = sel_s[h, pl.ds(n, 1), :] > 0.5
                s = _nt_dot(k_n, qs_s[c])
                m_old = m_s[c]
                m_new = jnp.where(ok, jnp.maximum(m_old, jnp.max(s, axis=0, keepdims=True)), m_old)
                p = jnp.where(ok, jnp.exp(s - m_new), 0.0)
                alpha = jnp.exp(m_old - m_new)
                m_s[c] = m_new
                l_s[c] = alpha * l_s[c] + jnp.sum(p, axis=0, keepdims=True)
                parts.append(jnp.dot(v_n, p.astype(BF16), preferred_element_type=F32))
                alphas.append(alpha)
            lo = row < HEAD_DIM
            acc_s[j] = (acc_s[j] * jnp.where(lo, alphas[0], alphas[1])
                        + jnp.where(lo, parts[0], parts[1]))
        return carry

    lax.fori_loop(0, i, past_block, 0)

    for j in range(N_Q_TILES):
        inv = jnp.where(row < HEAD_DIM, 1.0 / l_s[j * N_KV_HEADS], 1.0 / l_s[j * N_KV_HEADS + 1])
        g = g_ref[0, :, j * LANES:(j + 1) * LANES]
        o_ref[0, :, j * LANES:(j + 1) * LANES] = (
            jnp.transpose(acc_s[j] * inv) * (g * jax.nn.sigmoid(g)))


def _moba(q, g, kv, q_norm2, k_norm2):
    b, l, _ = q.shape
    blk = MOBA_BLOCK
    nblk = l // blk
    qspec = pl.BlockSpec((1, blk, Q_WIDTH), lambda i, n: (i, n, 0))
    vec = pl.BlockSpec((1, LANES), lambda i, n: (0, 0))
    nc = N_Q_TILES * N_KV_HEADS
    return pl.pallas_call(
        functools.partial(_moba_kernel, nblk),
        grid=(b, nblk),
        in_specs=[qspec, qspec,
                  pl.BlockSpec((1, l, 2 * KV_WIDTH), lambda i, n: (i, 0, 0)),
                  vec, vec],
        out_specs=qspec,
        out_shape=jax.ShapeDtypeStruct((b, l, Q_WIDTH), F32),
        scratch_shapes=[pltpu.VMEM((nblk, blk, KV_WIDTH), BF16),
                        pltpu.VMEM((nblk, KV_WIDTH, blk), BF16),
                        pltpu.VMEM((nblk, KV_WIDTH), F32),
                        pltpu.VMEM((nc, blk, LANES), BF16),
                        pltpu.VMEM((N_KV_HEADS, nblk, blk), F32),
                        pltpu.VMEM((nc, 1, blk), F32),
                        pltpu.VMEM((nc, 1, blk), F32),
                        pltpu.VMEM((N_Q_TILES, KV_WIDTH, blk), F32)],
        compiler_params=_cparams("parallel", "arbitrary"),
        name="moba",
    )(q, g, kv, q_norm2, k_norm2)


def _outproj_kernel(x_ref, ys_ref, ya_ref, ym_ref, ws_ref, wa_ref, wm_ref, o_ref):
    acc = jnp.dot(ys_ref[...].astype(BF16), ws_ref[...], preferred_element_type=F32)
    acc = acc + jnp.dot(ya_ref[0].astype(BF16), wa_ref[...], preferred_element_type=F32)
    acc = acc + jnp.dot(ym_ref[0].astype(BF16), wm_ref[...], preferred_element_type=F32)
    o_ref[0] = x_ref[0] + acc


def _outproj(x, y_ssm_tm, y_swa, y_moba, w_s, w_a, w_m, tl):
    b, l, d = x.shape
    full = lambda a: pl.BlockSpec(a.shape, lambda i, t: (0, 0))
    qspec = pl.BlockSpec((1, tl, Q_WIDTH), lambda i, t: (i, t, 0))
    xspec = pl.BlockSpec((1, tl, d), lambda i, t: (i, t, 0))
    return pl.pallas_call(
        _outproj_kernel,
        grid=(b, l // tl),
        in_specs=[xspec,
                  pl.BlockSpec((tl, SSM_WIDTH), lambda i, t: (t, i)),
                  qspec, qspec, full(w_s), full(w_a), full(w_m)],
        out_specs=xspec,
        out_shape=jax.ShapeDtypeStruct((b, l, d), F32),
        compiler_params=_cparams("parallel", "parallel"),
        name="outproj",
    )(x, y_ssm_tm, y_swa, y_moba, w_s, w_a, w_m)


def _head_perm_cols(w):
    r = w.shape[0]
    return w.reshape(r, N_Q_HEADS, HEAD_DIM)[:, HEAD_ORDER, :].reshape(r, Q_WIDTH)


def _permute_w_in(w):
    s = SSM_WIDTH
    pieces = []
    off = 0
    sizes = (s, s, Q_WIDTH, KV_WIDTH, KV_WIDTH, Q_WIDTH, Q_WIDTH, KV_WIDTH, KV_WIDTH, Q_WIDTH)
    for n in sizes:
        pieces.append(w[:, off:off + n])
        off += n
    s_u, s_g, a_q, a_k, a_v, a_g, m_q, m_k, m_v, m_g = pieces
    return jnp.concatenate(
        [s_u, s_g, _head_perm_cols(a_q), _head_perm_cols(a_g), _head_perm_cols(m_q),
         _head_perm_cols(m_g), a_k, a_v, m_k, m_v], axis=1)


def _block_diag_in(bb):
    g, h, p = bb.shape
    eye = jnp.eye(g, dtype=bb.dtype)
    return jnp.einsum('ghp,gk->ghkp', bb, eye).reshape(g * h, g * p)


def _block_diag_out(c):
    g, h, p = c.shape
    eye = jnp.eye(g, dtype=c.dtype)
    return jnp.einsum('ghp,gk->kpgh', c, eye).reshape(g * p, g * h)


def kernel(x, norm_g, w_in, ssm_lam_re, ssm_lam_im, ssm_log_dt, ssm_b_re, ssm_b_im,
           ssm_c_re, ssm_c_im, ssm_d, ssm_glu_w, ssm_glu_b, swa_q_norm, swa_k_norm,
           swa_sink, moba_q_norm, moba_k_norm, w_out):
    b, l, d = x.shape
    depth = norm_g.shape[0]
    tl = min(512, l)
    tc = min(64, l)
    a_re, a_im, bb_re, bb_im = _ssm_prep(ssm_lam_re, ssm_lam_im, ssm_log_dt, ssm_b_re, ssm_b_im)
    two = lambda v: jnp.concatenate([v, v]).reshape(1, LANES).astype(F32)
    for layer in range(depth):
        w_perm = _permute_w_in(w_in[layer]).astype(BF16)
        bmat = jnp.concatenate([_block_diag_in(bb_re[layer]), _block_diag_in(bb_im[layer])],
                               axis=1).astype(BF16)
        c_re = _block_diag_out(ssm_c_re[layer]).astype(BF16)
        c_im = _block_diag_out(ssm_c_im[layer]).astype(BF16)
        w_o = w_out[layer]
        w_s = w_o[0:SSM_WIDTH].astype(BF16)
        w_a = _head_perm_cols(w_o[SSM_WIDTH:SSM_WIDTH + Q_WIDTH].T).T.astype(BF16)
        w_m = _head_perm_cols(w_o[SSM_WIDTH + Q_WIDTH:].T).T.astype(BF16)
        sink = swa_sink[layer].reshape(1, N_Q_HEADS).astype(F32)

        u_tm, sg_tm, a_q, a_g, m_q, m_g, a_kv, m_kv = _inproj(x, norm_g[layer], w_perm, tl)
        y_ssm = _ssm(u_tm.reshape(l * b, SSM_WIDTH), sg_tm.reshape(l * b, SSM_WIDTH), bmat,
                     a_re[layer].reshape(1, N_STATE), a_im[layer].reshape(1, N_STATE),
                     c_re, c_im, ssm_d[layer].reshape(1, SSM_WIDTH).astype(F32),
                     ssm_glu_w[layer].astype(BF16),
                     ssm_glu_b[layer].reshape(1, SSM_WIDTH).astype(F32), b, tc)
        y_swa = _swa(a_q, a_g, a_kv, sink, two(swa_q_norm[layer]), two(swa_k_norm[layer]))
        y_moba = _moba(m_q, m_g, m_kv, two(moba_q_norm[layer]), two(moba_k_norm[layer]))
        x = _outproj(x, y_ssm.reshape(l, b * SSM_WIDTH), y_swa, y_moba, w_s, w_a, w_m, tl)
    return x
```

```python
import functools
import math

import jax
import jax.numpy as jnp
from jax import lax
from jax.experimental import pallas as pl
from jax.experimental.pallas import tpu as pltpu

F32 = jnp.float32
BF16 = jnp.bfloat16

HEAD_DIM = 64
SSM_WIDTH = 256
SSM_GROUP_CH = 16
SSM_GROUPS = 16
SSM_STATE = 64
N_STATE = SSM_GROUPS * SSM_STATE
N_Q_HEADS = 6
N_KV_HEADS = 2
Q_WIDTH = N_Q_HEADS * HEAD_DIM
KV_WIDTH = N_KV_HEADS * HEAD_DIM
SWA_WINDOW = 128
MOBA_BLOCK = 256
MOBA_TOPK = 3
RMS_EPS = 1e-6
ATTN_SCALE = HEAD_DIM ** -0.5
LOG2E = math.log2(math.e)
LANES = 128
HEAD_ORDER = (0, 3, 1, 4, 2, 5)
N_Q_TILES = Q_WIDTH // LANES
VMEM_LIMIT = 56 * 1024 * 1024


def _cparams(*sem):
    return pltpu.CompilerParams(dimension_semantics=sem, vmem_limit_bytes=VMEM_LIMIT)


def _ssm_prep_kernel(lr_ref, li_ref, ldt_ref, br_ref, bi_ref,
                     are_ref, aim_ref, bbr_ref, bbi_ref):
    lr = lr_ref[...]
    li = li_ref[...]
    dt = jnp.exp(ldt_ref[...])
    mag = jnp.exp(lr * dt)
    a_re = mag * jnp.cos(li * dt)
    a_im = mag * jnp.sin(li * dt)
    den = lr * lr + li * li
    nr = a_re - 1.0
    ni = a_im
    cr = (nr * lr + ni * li) / den
    ci = (ni * lr - nr * li) / den
    br = br_ref[...]
    bi = bi_ref[...]
    are_ref[...] = a_re
    aim_ref[...] = a_im
    bbr_ref[...] = cr * br - ci * bi
    bbi_ref[...] = cr * bi + ci * br


def _ssm_prep(lam_re, lam_im, log_dt, b_re, b_im):
    d, g, p = lam_re.shape
    h = b_re.shape[-1]
    shp = (d * g, h, p)
    bc = lambda a: jnp.broadcast_to(a.reshape(d * g, 1, -1), shp).astype(F32)
    tr = lambda a: jnp.swapaxes(a, -1, -2).reshape(shp).astype(F32)
    outs = pl.pallas_call(
        _ssm_prep_kernel,
        out_shape=[jax.ShapeDtypeStruct(shp, F32)] * 4,
        name="ssm_prep",
    )(bc(lam_re), bc(lam_im), bc(log_dt[..., None]), tr(b_re), tr(b_im))
    a_re, a_im, bb_re, bb_im = outs
    return (a_re[:, 0, :].reshape(d, g, p), a_im[:, 0, :].reshape(d, g, p),
            bb_re.reshape(d, g, h, p), bb_im.reshape(d, g, h, p))


def _inproj_kernel(x_ref, g_ref, w_ref, u_ref, sg_ref, aq_ref, ag_ref, mq_ref, mg_ref,
                   akv_ref, mkv_ref):
    x = x_ref[0]
    ms = jnp.mean(x * x, axis=-1, keepdims=True)
    h = (x * lax.rsqrt(ms + RMS_EPS) * g_ref[...]).astype(BF16)

    def mm(lo, width):
        return jnp.dot(h, w_ref[:, lo:lo + width], preferred_element_type=F32)

    u_ref[...] = mm(0, SSM_WIDTH)
    sg_ref[...] = mm(SSM_WIDTH, SSM_WIDTH)
    off = 2 * SSM_WIDTH
    aq_ref[0] = mm(off, Q_WIDTH)
    ag_ref[0] = mm(off + Q_WIDTH, Q_WIDTH)
    mq_ref[0] = mm(off + 2 * Q_WIDTH, Q_WIDTH)
    mg_ref[0] = mm(off + 3 * Q_WIDTH, Q_WIDTH)
    off += 4 * Q_WIDTH
    akv_ref[0] = mm(off, 2 * KV_WIDTH)
    mkv_ref[0] = mm(off + 2 * KV_WIDTH, 2 * KV_WIDTH)


def _inproj(x, norm_g, w_perm, tl):
    b, l, d = x.shape
    nt = l // tl
    dp = w_perm.shape[1]
    tm = jax.ShapeDtypeStruct((l, b * SSM_WIDTH), F32)
    bq = jax.ShapeDtypeStruct((b, l, Q_WIDTH), F32)
    bkv = jax.ShapeDtypeStruct((b, l, 2 * KV_WIDTH), F32)
    tm_spec = pl.BlockSpec((tl, SSM_WIDTH), lambda i, t: (t, i))
    q_spec = pl.BlockSpec((1, tl, Q_WIDTH), lambda i, t: (i, t, 0))
    kv_spec = pl.BlockSpec((1, tl, 2 * KV_WIDTH), lambda i, t: (i, t, 0))
    return pl.pallas_call(
        _inproj_kernel,
        grid=(b, nt),
        in_specs=[pl.BlockSpec((1, tl, d), lambda i, t: (i, t, 0)),
                  pl.BlockSpec((1, d), lambda i, t: (0, 0)),
                  pl.BlockSpec((d, dp), lambda i, t: (0, 0))],
        out_specs=[tm_spec, tm_spec, q_spec, q_spec, q_spec, q_spec, kv_spec, kv_spec],
        out_shape=[tm, tm, bq, bq, bq, bq, bkv, bkv],
        compiler_params=_cparams("parallel", "parallel"),
        name="inproj",
    )(x, norm_g.reshape(1, d), w_perm)


def _ssm_kernel(nb, tc, u_ref, sg_ref, bmat_ref, are_ref, aim_ref, cre_ref, cim_ref,
                d_ref, gw_ref, gb_ref, y_ref, bu_ref, sre_ref, sim_ref):
    @pl.when(pl.program_id(0) == 0)
    def _():
        sre_ref[...] = jnp.zeros_like(sre_ref)
        sim_ref[...] = jnp.zeros_like(sim_ref)

    u = u_ref[...]
    bu_ref[...] = jnp.dot(u.astype(BF16), bmat_ref[...], preferred_element_type=F32)
    a_re = jnp.broadcast_to(are_ref[...], (nb, N_STATE))
    a_im = jnp.broadcast_to(aim_ref[...], (nb, N_STATE))

    def step(t, carry):
        xr, xi = carry
        r0 = pl.multiple_of(t * nb, nb)
        br = bu_ref[pl.ds(r0, nb), 0:N_STATE]
        bi = bu_ref[pl.ds(r0, nb), N_STATE:2 * N_STATE]
        nr = a_re * xr - a_im * xi + br
        ni = a_re * xi + a_im * xr + bi
        bu_ref[pl.ds(r0, nb), 0:N_STATE] = nr
        bu_ref[pl.ds(r0, nb), N_STATE:2 * N_STATE] = ni
        return nr, ni

    xr, xi = lax.fori_loop(0, tc, step, (sre_ref[...], sim_ref[...]))
    sre_ref[...] = xr
    sim_ref[...] = xi

    xs_re = bu_ref[:, 0:N_STATE].astype(BF16)
    xs_im = bu_ref[:, N_STATE:2 * N_STATE].astype(BF16)
    y = (jnp.dot(xs_re, cre_ref[...], preferred_element_type=F32)
         - jnp.dot(xs_im, cim_ref[...], preferred_element_type=F32))
    y = y + d_ref[...] * u
    y = jax.nn.gelu(y)
    z = jnp.dot(y.astype(BF16), gw_ref[...], preferred_element_type=F32) + gb_ref[...]
    y = y * jax.nn.sigmoid(z)
    sg = sg_ref[...]
    y_ref[...] = y * (sg * jax.nn.sigmoid(sg))


def _ssm(u_rows, sg_rows, bmat, a_re, a_im, c_re, c_im, d, glu_w, glu_b, nb, tc):
    rows = u_rows.shape[0]
    blk = tc * nb
    row_spec = pl.BlockSpec((blk, SSM_WIDTH), lambda i: (i, 0))
    full = lambda a: pl.BlockSpec(a.shape, lambda i: (0,) * a.ndim)
    args = (bmat, a_re, a_im, c_re, c_im, d, glu_w, glu_b)
    return pl.pallas_call(
        functools.partial(_ssm_kernel, nb, tc),
        grid=(rows // blk,),
        in_specs=[row_spec, row_spec] + [full(a) for a in args],
        out_specs=row_spec,
        out_shape=jax.ShapeDtypeStruct((rows, SSM_WIDTH), F32),
        scratch_shapes=[pltpu.VMEM((blk, 2 * N_STATE), F32),
                        pltpu.VMEM((nb, N_STATE), F32),
                        pltpu.VMEM((nb, N_STATE), F32)],
        compiler_params=_cparams("arbitrary"),
        name="ssm",
    )(u_rows, sg_rows, *args)


def _pair_rms_norm(t, gain2):
    lane = lax.broadcasted_iota(jnp.int32, t.shape, 1)
    lo = lane < HEAD_DIM
    sq = t * t
    s_lo = jnp.sum(jnp.where(lo, sq, 0.0), axis=-1, keepdims=True)
    s_hi = jnp.sum(jnp.where(lo, 0.0, sq), axis=-1, keepdims=True)
    r_lo = lax.rsqrt(s_lo * (1.0 / HEAD_DIM) + RMS_EPS)
    r_hi = lax.rsqrt(s_hi * (1.0 / HEAD_DIM) + RMS_EPS)
    return t * jnp.where(lo, r_lo, r_hi) * gain2


def _half_select(t, kv_head):
    lane = lax.broadcasted_iota(jnp.int32, t.shape, 1)
    keep = (lane < HEAD_DIM) if kv_head == 0 else (lane >= HEAD_DIM)
    return jnp.where(keep, t, 0.0)


SUM_ROWS = 16
PREP_ROWS = 256


def _sum_rows(n_keys):
    r = lax.broadcasted_iota(jnp.int32, (SUM_ROWS, 2 * n_keys), 0)
    c = lax.broadcasted_iota(jnp.int32, (SUM_ROWS, 2 * n_keys), 1)
    return jnp.where(((r == 0) & (c < n_keys)) | ((r == 1) & (c >= n_keys)), 1.0, 0.0).astype(BF16)


def _prep_keys_values(kv_ref, kn_ref, ks_s, vs_s, seq):
    vrow = lax.broadcasted_iota(jnp.int32, (KV_WIDTH, PREP_ROWS), 0)
    means = []
    for c in range(seq // PREP_ROWS):
        rows = slice(c * PREP_ROWS, (c + 1) * PREP_ROWS)
        kvb = kv_ref[0, rows, :]
        kf = _pair_rms_norm(kvb[:, 0:KV_WIDTH], kn_ref[...])
        means.append(jnp.mean(kf, axis=0, keepdims=True))
        ks_s[0, rows, :] = _half_select(kf, 0).astype(BF16)
        ks_s[1, rows, :] = _half_select(kf, 1).astype(BF16)
        vt = jnp.transpose(kvb[:, KV_WIDTH:2 * KV_WIDTH])
        vs_s[0, :, rows] = jnp.where(vrow < HEAD_DIM, vt, 0.0).astype(BF16)
        vs_s[1, :, rows] = jnp.where(vrow < HEAD_DIM, 0.0, vt).astype(BF16)
    return means


def _stacked(ks_s, vs_s, lo, hi, sum_rows):
    kst = jnp.concatenate([ks_s[0, lo:hi, :], ks_s[1, lo:hi, :]], axis=0)
    vst = jnp.concatenate([vs_s[0, :, lo:hi], vs_s[1, :, lo:hi]], axis=1)
    return kst, jnp.concatenate([vst, sum_rows], axis=0)


def _swa_kernel(nblk, sink_ref, q_ref, g_ref, kv_ref, qn_ref, kn_ref, o_ref, ks_s, vs_s):
    w = SWA_WINDOW
    nq = N_Q_TILES * w
    _prep_keys_values(kv_ref, kn_ref, ks_s, vs_s, nblk * w)

    lane = lax.broadcasted_iota(jnp.int32, (1, nq), 1)
    sink2 = []
    for h in range(N_KV_HEADS):
        a = [sink_ref[0, j + N_Q_TILES * h] * LOG2E for j in range(N_Q_TILES)]
        sink2.append(jnp.where(lane < w, a[0], jnp.where(lane < 2 * w, a[1], a[2])))
    lo_rows = lax.broadcasted_iota(jnp.int32, (KV_WIDTH, nq), 0) < HEAD_DIM

    def window_mask(n_keys):
        kpos = (lax.broadcasted_iota(jnp.int32, (2 * n_keys, nq), 0) & (n_keys - 1)) - (n_keys - w)
        qpos = lax.broadcasted_iota(jnp.int32, (2 * n_keys, nq), 1) & (w - 1)
        rel = qpos - kpos
        return (rel >= 0) & (rel < w)

    masks = {w: window_mask(w), 2 * w: window_mask(2 * w)}
    sums = {w: _sum_rows(w), 2 * w: _sum_rows(2 * w)}

    for n in range(nblk):
        rows = slice(n * w, (n + 1) * w)
        lo = max(n - 1, 0) * w
        hi = (n + 1) * w
        nk = hi - lo
        kst, vst = _stacked(ks_s, vs_s, lo, hi, sums[nk])
        q_t = jnp.concatenate(
            [jnp.transpose(_pair_rms_norm(q_ref[0, rows, j * LANES:(j + 1) * LANES], qn_ref[...])
                           * (ATTN_SCALE * LOG2E)) for j in range(N_Q_TILES)], axis=1).astype(BF16)
        s = jnp.where(masks[nk], jnp.dot(kst, q_t, preferred_element_type=F32), -jnp.inf)
        m = [jnp.maximum(jnp.max(s[h * nk:(h + 1) * nk], axis=0, keepdims=True), sink2[h])
             for h in range(N_KV_HEADS)]
        p = jnp.concatenate([jnp.exp2(s[h * nk:(h + 1) * nk] - m[h]) for h in range(N_KV_HEADS)],
                            axis=0).astype(BF16)
        pv = jnp.dot(vst, p, preferred_element_type=F32)
        den = [pv[KV_WIDTH + h:KV_WIDTH + h + 1] + jnp.exp2(sink2[h] - m[h])
               for h in range(N_KV_HEADS)]
        o_t = pv[0:KV_WIDTH] * jnp.where(lo_rows, 1.0 / den[0], 1.0 / den[1])
        for j in range(N_Q_TILES):
            g = g_ref[0, rows, j * LANES:(j + 1) * LANES]
            o_ref[0, rows, j * LANES:(j + 1) * LANES] = (
                jnp.transpose(o_t[:, j * w:(j + 1) * w]) * (g * jax.nn.sigmoid(g)))


def _swa(q, g, kv, sink, q_norm2, k_norm2):
    b, l, _ = q.shape
    qspec = pl.BlockSpec((1, l, Q_WIDTH), lambda i: (i, 0, 0))
    vec = pl.BlockSpec((1, LANES), lambda i: (0, 0))
    return pl.pallas_call(
        functools.partial(_swa_kernel, l // SWA_WINDOW),
        grid=(b,),
        in_specs=[pl.BlockSpec(memory_space=pltpu.SMEM),
                  qspec, qspec,
                  pl.BlockSpec((1, l, 2 * KV_WIDTH), lambda i: (i, 0, 0)),
                  vec, vec],
        out_specs=qspec,
        out_shape=jax.ShapeDtypeStruct((b, l, Q_WIDTH), F32),
        scratch_shapes=[pltpu.VMEM((N_KV_HEADS, l, KV_WIDTH), BF16),
                        pltpu.VMEM((N_KV_HEADS, KV_WIDTH, l), BF16)],
        compiler_params=_cparams("parallel"),
        name="swa",
    )(sink, q, g, kv, q_norm2, k_norm2)


def _moba_kernel(nblk, q_ref, g_ref, kv_ref, qn_ref, kn_ref, o_ref, ks_s, vs_s):
    blk = MOBA_BLOCK
    assert blk == PREP_ROWS
    nq = N_Q_TILES * blk
    kmean = jnp.concatenate(_prep_keys_values(kv_ref, kn_ref, ks_s, vs_s, nblk * blk), axis=0)
    sum_rows = _sum_rows(blk)

    blk_id = lax.broadcasted_iota(jnp.int32, (nblk, blk), 0)
    ki = lax.broadcasted_iota(jnp.int32, (2 * blk, nq), 0) & (blk - 1)
    qi = lax.broadcasted_iota(jnp.int32, (2 * blk, nq), 1) & (blk - 1)
    causal = ki <= qi
    lo = lax.broadcasted_iota(jnp.int32, (KV_WIDTH, nq), 0) < HEAD_DIM

    for i in range(nblk):
        rows = slice(i * blk, (i + 1) * blk)
        qsum = jnp.zeros((blk, LANES), F32)
        q_t = []
        for j in range(N_Q_TILES):
            qf = _pair_rms_norm(q_ref[0, rows, j * LANES:(j + 1) * LANES], qn_ref[...])
            qsum = qsum + qf
            q_t.append(jnp.transpose(qf * (ATTN_SCALE * LOG2E)))
        q_t = jnp.concatenate(q_t, axis=1).astype(BF16)

        past = blk_id < i
        sel = []
        for h in range(N_KV_HEADS):
            gate = lax.dot_general(_half_select(kmean, h), _half_select(qsum, h),
                                   (((1,), (1,)), ((), ())),
                                   precision=lax.Precision.HIGHEST,
                                   preferred_element_type=F32)
            gate = jnp.where(past, gate, -jnp.inf)
            rank = jnp.zeros((nblk, blk), jnp.int32)
            for n2 in range(nblk):
                other = gate[n2:n2 + 1, :]
                ahead = (other > gate) | ((other == gate) & (n2 < blk_id))
                rank = rank + ahead.astype(jnp.int32)
            chosen = jnp.where((rank < MOBA_TOPK) & past, 1.0, 0.0)
            sel.append(jnp.concatenate([chosen] * N_Q_TILES, axis=1))

        kst, vst = _stacked(ks_s, vs_s, i * blk, (i + 1) * blk, sum_rows)
        s = jnp.where(causal, jnp.dot(kst, q_t, preferred_element_type=F32), -jnp.inf)
        m = [jnp.max(s[h * blk:(h + 1) * blk], axis=0, keepdims=True) for h in range(N_KV_HEADS)]
        p = jnp.concatenate([jnp.exp2(s[h * blk:(h + 1) * blk] - m[h]) for h in range(N_KV_HEADS)],
                            axis=0).astype(BF16)
        pv = jnp.dot(vst, p, preferred_element_type=F32)
        acc = pv[0:KV_WIDTH]
        den = [pv[KV_WIDTH + h:KV_WIDTH + h + 1] for h in range(N_KV_HEADS)]

        for n in range(i):
            kst, vst = _stacked(ks_s, vs_s, n * blk, (n + 1) * blk, sum_rows)
            s = jnp.dot(kst, q_t, preferred_element_type=F32)
            alpha = []
            parts = []
            for h in range(N_KV_HEADS):
                ok = sel[h][n:n + 1, :] > 0.5
                s_h = s[h * blk:(h + 1) * blk]
                m_new = jnp.where(ok, jnp.maximum(m[h], jnp.max(s_h, axis=0, keepdims=True)), m[h])
                alpha.append(jnp.exp2(m[h] - m_new))
                parts.append(jnp.exp2(s_h - jnp.where(ok, m_new, jnp.inf)))
                m[h] = m_new
            p = jnp.concatenate(parts, axis=0).astype(BF16)
            pv = jnp.dot(vst, p, preferred_element_type=F32)
            acc = acc * jnp.where(lo, alpha[0], alpha[1]) + pv[0:KV_WIDTH]
            den = [alpha[h] * den[h] + pv[KV_WIDTH + h:KV_WIDTH + h + 1] for h in range(N_KV_HEADS)]

        o_t = acc * jnp.where(lo, 1.0 / den[0], 1.0 / den[1])
        for j in range(N_Q_TILES):
            g = g_ref[0, rows, j * LANES:(j + 1) * LANES]
            o_ref[0, rows, j * LANES:(j + 1) * LANES] = (
                jnp.transpose(o_t[:, j * blk:(j + 1) * blk]) * (g * jax.nn.sigmoid(g)))


def _moba(q, g, kv, q_norm2, k_norm2):
    b, l, _ = q.shape
    blk = MOBA_BLOCK
    nblk = l // blk
    qspec = pl.BlockSpec((1, l, Q_WIDTH), lambda i: (i, 0, 0))
    vec = pl.BlockSpec((1, LANES), lambda i: (0, 0))
    return pl.pallas_call(
        functools.partial(_moba_kernel, nblk),
        grid=(b,),
        in_specs=[qspec, qspec,
                  pl.BlockSpec((1, l, 2 * KV_WIDTH), lambda i: (i, 0, 0)),
                  vec, vec],
        out_specs=qspec,
        out_shape=jax.ShapeDtypeStruct((b, l, Q_WIDTH), F32),
        scratch_shapes=[pltpu.VMEM((N_KV_HEADS, l, KV_WIDTH), BF16),
                        pltpu.VMEM((N_KV_HEADS, KV_WIDTH, l), BF16)],
        compiler_params=_cparams("parallel"),
        name="moba",
    )(q, g, kv, q_norm2, k_norm2)


def _outproj_kernel(x_ref, ys_ref, ya_ref, ym_ref, ws_ref, wa_ref, wm_ref, o_ref):
    acc = jnp.dot(ys_ref[...].astype(BF16), ws_ref[...], preferred_element_type=F32)
    acc = acc + jnp.dot(ya_ref[0].astype(BF16), wa_ref[...], preferred_element_type=F32)
    acc = acc + jnp.dot(ym_ref[0].astype(BF16), wm_ref[...], preferred_element_type=F32)
    o_ref[0] = x_ref[0] + acc


def _outproj(x, y_ssm_tm, y_swa, y_moba, w_s, w_a, w_m, tl):
    b, l, d = x.shape
    full = lambda a: pl.BlockSpec(a.shape, lambda i, t: (0, 0))
    qspec = pl.BlockSpec((1, tl, Q_WIDTH), lambda i, t: (i, t, 0))
    xspec = pl.BlockSpec((1, tl, d), lambda i, t: (i, t, 0))
    return pl.pallas_call(
        _outproj_kernel,
        grid=(b, l // tl),
        in_specs=[xspec,
                  pl.BlockSpec((tl, SSM_WIDTH), lambda i, t: (t, i)),
                  qspec, qspec, full(w_s), full(w_a), full(w_m)],
        out_specs=xspec,
        out_shape=jax.ShapeDtypeStruct((b, l, d), F32),
        compiler_params=_cparams("parallel", "parallel"),
        name="outproj",
    )(x, y_ssm_tm, y_swa, y_moba, w_s, w_a, w_m)


def _head_perm_cols(w):
    r = w.shape[0]
    return w.reshape(r, N_Q_HEADS, HEAD_DIM)[:, HEAD_ORDER, :].reshape(r, Q_WIDTH)


def _permute_w_in(w):
    s = SSM_WIDTH
    pieces = []
    off = 0
    sizes = (s, s, Q_WIDTH, KV_WIDTH, KV_WIDTH, Q_WIDTH, Q_WIDTH, KV_WIDTH, KV_WIDTH, Q_WIDTH)
    for n in sizes:
        pieces.append(w[:, off:off + n])
        off += n
    s_u, s_g, a_q, a_k, a_v, a_g, m_q, m_k, m_v, m_g = pieces
    return jnp.concatenate(
        [s_u, s_g, _head_perm_cols(a_q), _head_perm_cols(a_g), _head_perm_cols(m_q),
         _head_perm_cols(m_g), a_k, a_v, m_k, m_v], axis=1)


def _block_diag_in(bb):
    g, h, p = bb.shape
    eye = jnp.eye(g, dtype=bb.dtype)
    return jnp.einsum('ghp,gk->ghkp', bb, eye).reshape(g * h, g * p)


def _block_diag_out(c):
    g, h, p = c.shape
    eye = jnp.eye(g, dtype=c.dtype)
    return jnp.einsum('ghp,gk->kpgh', c, eye).reshape(g * p, g * h)


def kernel(x, norm_g, w_in, ssm_lam_re, ssm_lam_im, ssm_log_dt, ssm_b_re, ssm_b_im,
           ssm_c_re, ssm_c_im, ssm_d, ssm_glu_w, ssm_glu_b, swa_q_norm, swa_k_norm,
           swa_sink, moba_q_norm, moba_k_norm, w_out):
    b, l, d = x.shape
    depth = norm_g.shape[0]
    tl = min(512, l)
    tc = min(64, l)
    a_re, a_im, bb_re, bb_im = _ssm_prep(ssm_lam_re, ssm_lam_im, ssm_log_dt, ssm_b_re, ssm_b_im)
    two = lambda v: jnp.concatenate([v, v]).reshape(1, LANES).astype(F32)
    for layer in range(depth):
        w_perm = _permute_w_in(w_in[layer]).astype(BF16)
        bmat = jnp.concatenate([_block_diag_in(bb_re[layer]), _block_diag_in(bb_im[layer])],
                               axis=1).astype(BF16)
        c_re = _block_diag_out(ssm_c_re[layer]).astype(BF16)
        c_im = _block_diag_out(ssm_c_im[layer]).astype(BF16)
        w_o = w_out[layer]
        w_s = w_o[0:SSM_WIDTH].astype(BF16)
        w_a = _head_perm_cols(w_o[SSM_WIDTH:SSM_WIDTH + Q_WIDTH].T).T.astype(BF16)
        w_m = _head_perm_cols(w_o[SSM_WIDTH + Q_WIDTH:].T).T.astype(BF16)
        sink = swa_sink[layer].reshape(1, N_Q_HEADS).astype(F32)

        u_tm, sg_tm, a_q, a_g, m_q, m_g, a_kv, m_kv = _inproj(x, norm_g[layer], w_perm, tl)
        y_ssm = _ssm(u_tm.reshape(l * b, SSM_WIDTH), sg_tm.reshape(l * b, SSM_WIDTH), bmat,
                     a_re[layer].reshape(1, N_STATE), a_im[layer].reshape(1, N_STATE),
                     c_re, c_im, ssm_d[layer].reshape(1, SSM_WIDTH).astype(F32),
                     ssm_glu_w[layer].astype(BF16),
                     ssm_glu_b[layer].reshape(1, SSM_WIDTH).astype(F32), b, tc)
        y_swa = _swa(a_q, a_g, a_kv, sink, two(swa_q_norm[layer]), two(swa_k_norm[layer]))
        y_moba = _moba(m_q, m_g, m_kv, two(moba_q_norm[layer]), two(moba_k_norm[layer]))
        x = _outproj(x, y_ssm.reshape(l, b * SSM_WIDTH), y_swa, y_moba, w_s, w_a, w_m, tl)
    return x
```

```python
import functools
import math

import jax
import jax.numpy as jnp
from jax import lax
from jax.experimental import pallas as pl
from jax.experimental.pallas import tpu as pltpu

F32 = jnp.float32
BF16 = jnp.bfloat16

HEAD_DIM = 64
SSM_WIDTH = 256
SSM_GROUPS = 16
SSM_STATE = 64
N_STATE = SSM_GROUPS * SSM_STATE
N_Q_HEADS = 6
N_KV_HEADS = 2
Q_WIDTH = N_Q_HEADS * HEAD_DIM
KV_WIDTH = N_KV_HEADS * HEAD_DIM
SWA_WINDOW = 128
MOBA_BLOCK = 256
MOBA_TOPK = 3
RMS_EPS = 1e-6
ATTN_SCALE = HEAD_DIM ** -0.5
LOG2E = math.log2(math.e)
LANES = 128
N_Q_TILES = Q_WIDTH // LANES
VMEM_LIMIT = 56 * 1024 * 1024
SUM_ROWS = 16
PREP_ROWS = 256
OFF_U, OFF_SG = 0, SSM_WIDTH
OFF_AQ = 2 * SSM_WIDTH
OFF_AG = OFF_AQ + Q_WIDTH
OFF_MQ = OFF_AG + Q_WIDTH
OFF_MG = OFF_MQ + Q_WIDTH
OFF_AKV = OFF_MG + Q_WIDTH
OFF_MKV = OFF_AKV + 2 * KV_WIDTH


def _tiles(seq):
    return min(512, seq), min(64, seq)


def _cparams(*sem):
    return pltpu.CompilerParams(dimension_semantics=sem, vmem_limit_bytes=VMEM_LIMIT)


def _silu(t):
    return t * jax.nn.sigmoid(t)


def _ssm_prep_kernel(lr_ref, li_ref, ldt_ref, br_ref, bi_ref,
                     are_ref, aim_ref, bbr_ref, bbi_ref):
    lr = lr_ref[...]
    li = li_ref[...]
    dt = jnp.exp(ldt_ref[...])
    mag = jnp.exp(lr * dt)
    a_re = mag * jnp.cos(li * dt)
    a_im = mag * jnp.sin(li * dt)
    den = lr * lr + li * li
    nr = a_re - 1.0
    ni = a_im
    cr = (nr * lr + ni * li) / den
    ci = (ni * lr - nr * li) / den
    br = br_ref[...]
    bi = bi_ref[...]
    are_ref[...] = a_re
    aim_ref[...] = a_im
    bbr_ref[...] = cr * br - ci * bi
    bbi_ref[...] = cr * bi + ci * br


def _ssm_prep(lam_re, lam_im, log_dt, b_re, b_im):
    d, g, p = lam_re.shape
    h = b_re.shape[-1]
    shp = (d * g, h, p)
    bc = lambda a: jnp.broadcast_to(a.reshape(d * g, 1, -1), shp).astype(F32)
    tr = lambda a: jnp.swapaxes(a, -1, -2).reshape(shp).astype(F32)
    outs = pl.pallas_call(
        _ssm_prep_kernel,
        out_shape=[jax.ShapeDtypeStruct(shp, F32)] * 4,
        name="ssm_prep",
    )(bc(lam_re), bc(lam_im), bc(log_dt[..., None]), tr(b_re), tr(b_im))
    a_re, a_im, bb_re, bb_im = outs
    return (a_re[:, 0, :].reshape(d, g, p), a_im[:, 0, :].reshape(d, g, p),
            bb_re.reshape(d, g, h, p), bb_im.reshape(d, g, h, p))


def _pair_rms_norm(t, gain2):
    lane = lax.broadcasted_iota(jnp.int32, t.shape, 1)
    lo = lane < HEAD_DIM
    sq = t * t
    s_lo = jnp.sum(jnp.where(lo, sq, 0.0), axis=-1, keepdims=True)
    s_hi = jnp.sum(jnp.where(lo, 0.0, sq), axis=-1, keepdims=True)
    r_lo = lax.rsqrt(s_lo * (1.0 / HEAD_DIM) + RMS_EPS)
    r_hi = lax.rsqrt(s_hi * (1.0 / HEAD_DIM) + RMS_EPS)
    return t * jnp.where(lo, r_lo, r_hi) * gain2


def _half_select(t, kv_head):
    lane = lax.broadcasted_iota(jnp.int32, t.shape, 1)
    keep = (lane < HEAD_DIM) if kv_head == 0 else (lane >= HEAD_DIM)
    return jnp.where(keep, t, 0.0)


def _inproj_kernel(x_ref, g_ref, w_ref, aqn_ref, akn_ref, mqn_ref, mkn_ref,
                   u_ref, sg_ref, aq_ref, ag_ref, mq_ref, mqs_ref, mg_ref, akv_ref, mkv_ref):
    x = x_ref[0]
    ms = jnp.mean(x * x, axis=-1, keepdims=True)
    h = (x * lax.rsqrt(ms + RMS_EPS) * g_ref[...]).astype(BF16)

    def mm(lo, width):
        return jnp.dot(h, w_ref[:, lo:lo + width], preferred_element_type=F32)

    def tile(t, j):
        return t[:, j * LANES:(j + 1) * LANES]

    ssm = mm(OFF_U, 2 * SSM_WIDTH)
    u_ref[...] = ssm[:, 0:SSM_WIDTH].astype(BF16)
    sg_ref[...] = _silu(ssm[:, SSM_WIDTH:2 * SSM_WIDTH]).astype(BF16)

    aq = mm(OFF_AQ, Q_WIDTH)
    mq = mm(OFF_MQ, Q_WIDTH)
    qsum = jnp.zeros((x.shape[0], LANES), F32)
    for j in range(N_Q_TILES):
        cols = slice(j * LANES, (j + 1) * LANES)
        aq_ref[0, :, cols] = (_pair_rms_norm(tile(aq, j), aqn_ref[...])
                              * (ATTN_SCALE * LOG2E)).astype(BF16)
        mqn = _pair_rms_norm(tile(mq, j), mqn_ref[...])
        qsum = qsum + mqn
        mq_ref[0, :, cols] = (mqn * (ATTN_SCALE * LOG2E)).astype(BF16)
    mqs_ref[0] = qsum
    ag_ref[0] = _silu(mm(OFF_AG, Q_WIDTH)).astype(BF16)
    mg_ref[0] = _silu(mm(OFF_MG, Q_WIDTH)).astype(BF16)

    for off, kn_ref, kv_ref in ((OFF_AKV, akn_ref, akv_ref), (OFF_MKV, mkn_ref, mkv_ref)):
        kv = mm(off, 2 * KV_WIDTH)
        kv_ref[0, :, 0:KV_WIDTH] = _pair_rms_norm(kv[:, 0:KV_WIDTH], kn_ref[...]).astype(BF16)
        kv_ref[0, :, KV_WIDTH:2 * KV_WIDTH] = kv[:, KV_WIDTH:2 * KV_WIDTH].astype(BF16)


def _inproj(x, norm_g, w_perm, head_norms, tl):
    b, l, d = x.shape
    dp = w_perm.shape[1]
    tm = jax.ShapeDtypeStruct((l, b * SSM_WIDTH), BF16)
    bq = jax.ShapeDtypeStruct((b, l, Q_WIDTH), BF16)
    bqs = jax.ShapeDtypeStruct((b, l, LANES), F32)
    bkv = jax.ShapeDtypeStruct((b, l, 2 * KV_WIDTH), BF16)
    tm_spec = pl.BlockSpec((tl, SSM_WIDTH), lambda i, t: (t, i))
    q_spec = pl.BlockSpec((1, tl, Q_WIDTH), lambda i, t: (i, t, 0))
    qs_spec = pl.BlockSpec((1, tl, LANES), lambda i, t: (i, t, 0))
    kv_spec = pl.BlockSpec((1, tl, 2 * KV_WIDTH), lambda i, t: (i, t, 0))
    vec = pl.BlockSpec((1, LANES), lambda i, t: (0, 0))
    return pl.pallas_call(
        _inproj_kernel,
        grid=(b, l // tl),
        in_specs=[pl.BlockSpec((1, tl, d), lambda i, t: (i, t, 0)),
                  pl.BlockSpec((1, d), lambda i, t: (0, 0)),
                  pl.BlockSpec((d, dp), lambda i, t: (0, 0)),
                  vec, vec, vec, vec],
        out_specs=[tm_spec, tm_spec, q_spec, q_spec, q_spec, qs_spec, q_spec, kv_spec, kv_spec],
        out_shape=[tm, tm, bq, bq, bq, bqs, bq, bkv, bkv],
        compiler_params=_cparams("parallel", "parallel"),
        name="inproj",
    )(x, norm_g.reshape(1, d), w_perm, *head_norms)


def _ssm_kernel(nb, tc, u_ref, sg_ref, bmat_ref, are_ref, aim_ref, cre_ref, cim_ref,
                d_ref, gw_ref, gb_ref, y_ref, bu_ref, sre_ref, sim_ref):
    @pl.when(pl.program_id(0) == 0)
    def _():
        sre_ref[...] = jnp.zeros_like(sre_ref)
        sim_ref[...] = jnp.zeros_like(sim_ref)

    u = u_ref[...]
    bu_ref[...] = jnp.dot(u, bmat_ref[...], preferred_element_type=F32)
    a_re = jnp.broadcast_to(are_ref[...], (nb, N_STATE))
    a_im = jnp.broadcast_to(aim_ref[...], (nb, N_STATE))

    def step(t, carry):
        xr, xi = carry
        r0 = pl.multiple_of(t * nb, nb)
        br = bu_ref[pl.ds(r0, nb), 0:N_STATE]
        bi = bu_ref[pl.ds(r0, nb), N_STATE:2 * N_STATE]
        nr = a_re * xr - a_im * xi + br
        ni = a_re * xi + a_im * xr + bi
        bu_ref[pl.ds(r0, nb), 0:N_STATE] = nr
        bu_ref[pl.ds(r0, nb), N_STATE:2 * N_STATE] = ni
        return nr, ni

    xr, xi = lax.fori_loop(0, tc, step, (sre_ref[...], sim_ref[...]))
    sre_ref[...] = xr
    sim_ref[...] = xi

    xs_re = bu_ref[:, 0:N_STATE].astype(BF16)
    xs_im = bu_ref[:, N_STATE:2 * N_STATE].astype(BF16)
    y = (jnp.dot(xs_re, cre_ref[...], preferred_element_type=F32)
         - jnp.dot(xs_im, cim_ref[...], preferred_element_type=F32))
    y = y + d_ref[...] * u.astype(F32)
    y = jax.nn.gelu(y)
    z = jnp.dot(y.astype(BF16), gw_ref[...], preferred_element_type=F32) + gb_ref[...]
    y = y * jax.nn.sigmoid(z)
    y_ref[...] = (y * sg_ref[...].astype(F32)).astype(BF16)


def _ssm(u_rows, sg_rows, bmat, a_re, a_im, c_re, c_im, d, glu_w, glu_b, nb, tc):
    rows = u_rows.shape[0]
    blk = tc * nb
    row_spec = pl.BlockSpec((blk, SSM_WIDTH), lambda i: (i, 0))
    full = lambda a: pl.BlockSpec(a.shape, lambda i: (0,) * a.ndim)
    args = (bmat, a_re, a_im, c_re, c_im, d, glu_w, glu_b)
    return pl.pallas_call(
        functools.partial(_ssm_kernel, nb, tc),
        grid=(rows // blk,),
        in_specs=[row_spec, row_spec] + [full(a) for a in args],
        out_specs=row_spec,
        out_shape=jax.ShapeDtypeStruct((rows, SSM_WIDTH), BF16),
        scratch_shapes=[pltpu.VMEM((blk, 2 * N_STATE), F32),
                        pltpu.VMEM((nb, N_STATE), F32),
                        pltpu.VMEM((nb, N_STATE), F32)],
        compiler_params=_cparams("arbitrary"),
        name="ssm",
    )(u_rows, sg_rows, *args)


def _sum_rows(n_keys):
    r = lax.broadcasted_iota(jnp.int32, (SUM_ROWS, 2 * n_keys), 0)
    c = lax.broadcasted_iota(jnp.int32, (SUM_ROWS, 2 * n_keys), 1)
    return jnp.where(((r == 0) & (c < n_keys)) | ((r == 1) & (c >= n_keys)), 1.0, 0.0).astype(BF16)


def _prep_keys_values(kv_ref, ks_s, vs_s, seq):
    vrow = lax.broadcasted_iota(jnp.int32, (KV_WIDTH, PREP_ROWS), 0)
    means = []
    for c in range(seq // PREP_ROWS):
        rows = slice(c * PREP_ROWS, (c + 1) * PREP_ROWS)
        kvb = kv_ref[0, rows, :].astype(F32)
        kf = kvb[:, 0:KV_WIDTH]
        means.append(jnp.mean(kf, axis=0, keepdims=True))
        ks_s[0, rows, :] = _half_select(kf, 0).astype(BF16)
        ks_s[1, rows, :] = _half_select(kf, 1).astype(BF16)
        vt = jnp.transpose(kvb[:, KV_WIDTH:2 * KV_WIDTH])
        vs_s[0, :, rows] = jnp.where(vrow < HEAD_DIM, vt, 0.0).astype(BF16)
        vs_s[1, :, rows] = jnp.where(vrow < HEAD_DIM, 0.0, vt).astype(BF16)
    return means


def _stacked(ks_s, vs_s, lo, hi, sum_rows):
    kst = jnp.concatenate([ks_s[0, lo:hi, :], ks_s[1, lo:hi, :]], axis=0)
    vst = jnp.concatenate([vs_s[0, :, lo:hi], vs_s[1, :, lo:hi]], axis=1)
    return kst, jnp.concatenate([vst, sum_rows], axis=0)


def _queries_t(q_ref, rows):
    return jnp.concatenate(
        [jnp.transpose(q_ref[0, rows, j * LANES:(j + 1) * LANES].astype(F32))
         for j in range(N_Q_TILES)], axis=1).astype(BF16)


def _store_gated(o_ref, g_ref, rows, o_t, width):
    for j in range(N_Q_TILES):
        cols = slice(j * LANES, (j + 1) * LANES)
        o = jnp.transpose(o_t[:, j * width:(j + 1) * width]) * g_ref[0, rows, cols].astype(F32)
        o_ref[0, rows, cols] = o.astype(BF16)


_ATTN_SCRATCH = lambda seq: [pltpu.VMEM((N_KV_HEADS, seq, KV_WIDTH), BF16),
                             pltpu.VMEM((N_KV_HEADS, KV_WIDTH, seq), BF16)]


def _swa_kernel(nblk, sink_ref, q_ref, g_ref, kv_ref, o_ref, ks_s, vs_s):
    w = SWA_WINDOW
    nq = N_Q_TILES * w
    _prep_keys_values(kv_ref, ks_s, vs_s, nblk * w)

    lane = lax.broadcasted_iota(jnp.int32, (1, nq), 1)
    sink2 = []
    for h in range(N_KV_HEADS):
        a = [sink_ref[0, j + N_Q_TILES * h] * LOG2E for j in range(N_Q_TILES)]
        sink2.append(jnp.where(lane < w, a[0], jnp.where(lane < 2 * w, a[1], a[2])))
    lo_rows = lax.broadcasted_iota(jnp.int32, (KV_WIDTH, nq), 0) < HEAD_DIM

    def window_mask(n_keys):
        kpos = (lax.broadcasted_iota(jnp.int32, (2 * n_keys, nq), 0) & (n_keys - 1)) - (n_keys - w)
        qpos = lax.broadcasted_iota(jnp.int32, (2 * n_keys, nq), 1) & (w - 1)
        rel = qpos - kpos
        return (rel >= 0) & (rel < w)

    masks = {w: window_mask(w), 2 * w: window_mask(2 * w)}
    sums = {w: _sum_rows(w), 2 * w: _sum_rows(2 * w)}

    for n in range(nblk):
        rows = slice(n * w, (n + 1) * w)
        lo = max(n - 1, 0) * w
        hi = (n + 1) * w
        nk = hi - lo
        kst, vst = _stacked(ks_s, vs_s, lo, hi, sums[nk])
        q_t = _queries_t(q_ref, rows)
        s = jnp.where(masks[nk], jnp.dot(kst, q_t, preferred_element_type=F32), -jnp.inf)
        m = [jnp.maximum(jnp.max(s[h * nk:(h + 1) * nk], axis=0, keepdims=True), sink2[h])
             for h in range(N_KV_HEADS)]
        p = jnp.concatenate([jnp.exp2(s[h * nk:(h + 1) * nk] - m[h]) for h in range(N_KV_HEADS)],
                            axis=0).astype(BF16)
        pv = jnp.dot(vst, p, preferred_element_type=F32)
        den = [pv[KV_WIDTH + h:KV_WIDTH + h + 1] + jnp.exp2(sink2[h] - m[h])
               for h in range(N_KV_HEADS)]
        o_t = pv[0:KV_WIDTH] * jnp.where(lo_rows, 1.0 / den[0], 1.0 / den[1])
        _store_gated(o_ref, g_ref, rows, o_t, w)


def _swa(q, g, kv, sink):
    b, l, _ = q.shape
    qspec = pl.BlockSpec((1, l, Q_WIDTH), lambda i: (i, 0, 0))
    return pl.pallas_call(
        functools.partial(_swa_kernel, l // SWA_WINDOW),
        grid=(b,),
        in_specs=[pl.BlockSpec(memory_space=pltpu.SMEM),
                  qspec, qspec,
                  pl.BlockSpec((1, l, 2 * KV_WIDTH), lambda i: (i, 0, 0))],
        out_specs=qspec,
        out_shape=jax.ShapeDtypeStruct((b, l, Q_WIDTH), BF16),
        scratch_shapes=_ATTN_SCRATCH(l),
        compiler_params=_cparams("parallel"),
        name="swa",
    )(sink, q, g, kv)


def _moba_kernel(nblk, q_ref, qs_ref, g_ref, kv_ref, o_ref, ks_s, vs_s):
    blk = MOBA_BLOCK
    assert blk == PREP_ROWS
    nq = N_Q_TILES * blk
    kmean = jnp.concatenate(_prep_keys_values(kv_ref, ks_s, vs_s, nblk * blk), axis=0)
    sum_rows = _sum_rows(blk)

    blk_id = lax.broadcasted_iota(jnp.int32, (nblk, blk), 0)
    ki = lax.broadcasted_iota(jnp.int32, (2 * blk, nq), 0) & (blk - 1)
    qi = lax.broadcasted_iota(jnp.int32, (2 * blk, nq), 1) & (blk - 1)
    causal = ki <= qi
    lo = lax.broadcasted_iota(jnp.int32, (KV_WIDTH, nq), 0) < HEAD_DIM

    for i in range(nblk):
        rows = slice(i * blk, (i + 1) * blk)
        q_t = _queries_t(q_ref, rows)
        qsum = qs_ref[0, rows, :]

        past = blk_id < i
        sel = []
        for h in range(N_KV_HEADS):
            gate = lax.dot_general(_half_select(kmean, h), _half_select(qsum, h),
                                   (((1,), (1,)), ((), ())),
                                   precision=lax.Precision.HIGHEST,
                                   preferred_element_type=F32)
            gate = jnp.where(past, gate, -jnp.inf)
            rank = jnp.zeros((nblk, blk), jnp.int32)
            for n2 in range(nblk):
                other = gate[n2:n2 + 1, :]
                ahead = (other > gate) | ((other == gate) & (n2 < blk_id))
                rank = rank + ahead.astype(jnp.int32)
            chosen = jnp.where((rank < MOBA_TOPK) & past, 1.0, 0.0)
            sel.append(jnp.concatenate([chosen] * N_Q_TILES, axis=1))

        kst, vst = _stacked(ks_s, vs_s, i * blk, (i + 1) * blk, sum_rows)
        s = jnp.where(causal, jnp.dot(kst, q_t, preferred_element_type=F32), -jnp.inf)
        m = [jnp.max(s[h * blk:(h + 1) * blk], axis=0, keepdims=True) for h in range(N_KV_HEADS)]
        p = jnp.concatenate([jnp.exp2(s[h * blk:(h + 1) * blk] - m[h]) for h in range(N_KV_HEADS)],
                            axis=0).astype(BF16)
        pv = jnp.dot(vst, p, preferred_element_type=F32)
        acc = pv[0:KV_WIDTH]
        den = [pv[KV_WIDTH + h:KV_WIDTH + h + 1] for h in range(N_KV_HEADS)]

        for n in range(i):
            kst, vst = _stacked(ks_s, vs_s, n * blk, (n + 1) * blk, sum_rows)
            s = jnp.dot(kst, q_t, preferred_element_type=F32)
            alpha = []
            parts = []
            for h in range(N_KV_HEADS):
                ok = sel[h][n:n + 1, :] > 0.5
                s_h = s[h * blk:(h + 1) * blk]
                m_new = jnp.where(ok, jnp.maximum(m[h], jnp.max(s_h, axis=0, keepdims=True)), m[h])
                alpha.append(jnp.exp2(m[h] - m_new))
                parts.append(jnp.exp2(s_h - jnp.where(ok, m_new, jnp.inf)))
                m[h] = m_new
            p = jnp.concatenate(parts, axis=0).astype(BF16)
            pv = jnp.dot(vst, p, preferred_element_type=F32)
            acc = acc * jnp.where(lo, alpha[0], alpha[1]) + pv[0:KV_WIDTH]
            den = [alpha[h] * den[h] + pv[KV_WIDTH + h:KV_WIDTH + h + 1] for h in range(N_KV_HEADS)]

        o_t = acc * jnp.where(lo, 1.0 / den[0], 1.0 / den[1])
        _store_gated(o_ref, g_ref, rows, o_t, blk)


def _moba(q, qsum, g, kv):
    b, l, _ = q.shape
    qspec = pl.BlockSpec((1, l, Q_WIDTH), lambda i: (i, 0, 0))
    return pl.pallas_call(
        functools.partial(_moba_kernel, l // MOBA_BLOCK),
        grid=(b,),
        in_specs=[qspec,
                  pl.BlockSpec((1, l, LANES), lambda i: (i, 0, 0)),
                  qspec,
                  pl.BlockSpec((1, l, 2 * KV_WIDTH), lambda i: (i, 0, 0))],
        out_specs=qspec,
        out_shape=jax.ShapeDtypeStruct((b, l, Q_WIDTH), BF16),
        scratch_shapes=_ATTN_SCRATCH(l),
        compiler_params=_cparams("parallel"),
        name="moba",
    )(q, qsum, g, kv)


def _outproj_kernel(x_ref, ys_ref, ya_ref, ym_ref, w_ref, o_ref):
    y = jnp.concatenate([ys_ref[...], ya_ref[0], ym_ref[0]], axis=1)
    o_ref[0] = x_ref[0] + jnp.dot(y, w_ref[...], preferred_element_type=F32)


def _outproj(x, y_ssm_tm, y_swa, y_moba, w_perm, tl):
    b, l, d = x.shape
    qspec = pl.BlockSpec((1, tl, Q_WIDTH), lambda i, t: (i, t, 0))
    xspec = pl.BlockSpec((1, tl, d), lambda i, t: (i, t, 0))
    return pl.pallas_call(
        _outproj_kernel,
        grid=(b, l // tl),
        in_specs=[xspec,
                  pl.BlockSpec((tl, SSM_WIDTH), lambda i, t: (t, i)),
                  qspec, qspec,
                  pl.BlockSpec(w_perm.shape, lambda i, t: (0, 0))],
        out_specs=xspec,
        out_shape=jax.ShapeDtypeStruct((b, l, d), F32),
        compiler_params=_cparams("parallel", "parallel"),
        name="outproj",
    )(x, y_ssm_tm, y_swa, y_moba, w_perm)


def _head_tiles(w):
    lead = w.shape[:-1]
    w = w.reshape(*lead, N_KV_HEADS, N_Q_TILES, HEAD_DIM)
    return jnp.swapaxes(w, -3, -2).reshape(*lead, Q_WIDTH)


def _permute_w_in(w):
    s, q, k = SSM_WIDTH, Q_WIDTH, KV_WIDTH
    bounds = [0]
    for n in (s, s, q, k, k, q, q, k, k, q):
        bounds.append(bounds[-1] + n)
    s_u, s_g, a_q, a_k, a_v, a_g, m_q, m_k, m_v, m_g = [
        w[..., bounds[i]:bounds[i + 1]] for i in range(10)]
    return jnp.concatenate(
        [s_u, s_g, _head_tiles(a_q), _head_tiles(a_g), _head_tiles(m_q), _head_tiles(m_g),
         a_k, a_v, m_k, m_v], axis=-1).astype(BF16)


def _permute_w_out(w):
    s, q = SSM_WIDTH, Q_WIDTH
    rows_t = lambda part: jnp.swapaxes(_head_tiles(jnp.swapaxes(part, -1, -2)), -1, -2)
    return jnp.concatenate([w[:, 0:s], rows_t(w[:, s:s + q]), rows_t(w[:, s + q:])],
                           axis=1).astype(BF16)


def _block_diag_in(bb):
    d, g, h, p = bb.shape
    eye = jnp.eye(g, dtype=bb.dtype)[None, :, None, :, None]
    return (bb[:, :, :, None, :] * eye).reshape(d, g * h, g * p)


def _block_diag_out(c):
    d, g, h, p = c.shape
    eye = jnp.eye(g, dtype=c.dtype)[None, :, None, :, None]
    return (jnp.swapaxes(c, -1, -2)[:, :, :, None, :] * eye).reshape(d, g * p, g * h)


def kernel(x, norm_g, w_in, ssm_lam_re, ssm_lam_im, ssm_log_dt, ssm_b_re, ssm_b_im,
           ssm_c_re, ssm_c_im, ssm_d, ssm_glu_w, ssm_glu_b, swa_q_norm, swa_k_norm,
           swa_sink, moba_q_norm, moba_k_norm, w_out):
    b, l, d = x.shape
    depth = norm_g.shape[0]
    tl, tc = _tiles(l)
    a_re, a_im, bb_re, bb_im = _ssm_prep(ssm_lam_re, ssm_lam_im, ssm_log_dt, ssm_b_re, ssm_b_im)
    a_re = a_re.reshape(depth, 1, N_STATE)
    a_im = a_im.reshape(depth, 1, N_STATE)
    w_in_p = _permute_w_in(w_in)
    w_out_p = _permute_w_out(w_out)
    bmat = jnp.concatenate([_block_diag_in(bb_re), _block_diag_in(bb_im)], axis=-1).astype(BF16)
    c_re = _block_diag_out(ssm_c_re).astype(BF16)
    c_im = _block_diag_out(ssm_c_im).astype(BF16)
    glu_w = ssm_glu_w.astype(BF16)
    two = lambda v: jnp.concatenate([v, v], axis=-1).reshape(depth, 1, LANES).astype(F32)
    head_norms = [two(v) for v in (swa_q_norm, swa_k_norm, moba_q_norm, moba_k_norm)]
    ssm_d = ssm_d.reshape(depth, 1, SSM_WIDTH).astype(F32)
    glu_b = ssm_glu_b.reshape(depth, 1, SSM_WIDTH).astype(F32)
    sink = swa_sink.reshape(depth, 1, N_Q_HEADS).astype(F32)

    for layer in range(depth):
        u_tm, sg_tm, a_q, a_g, m_q, m_qsum, m_g, a_kv, m_kv = _inproj(
            x, norm_g[layer], w_in_p[layer], [v[layer] for v in head_norms], tl)
        y_ssm = _ssm(u_tm.reshape(l * b, SSM_WIDTH), sg_tm.reshape(l * b, SSM_WIDTH), bmat[layer],
                     a_re[layer], a_im[layer], c_re[layer], c_im[layer], ssm_d[layer],
                     glu_w[layer], glu_b[layer], b, tc)
        y_swa = _swa(a_q, a_g, a_kv, sink[layer])
        y_moba = _moba(m_q, m_qsum, m_g, m_kv)
        x = _outproj(x, y_ssm.reshape(l, b * SSM_WIDTH), y_swa, y_moba, w_out_p[layer], tl)
    return x
```

```python
import functools
import math

import jax
import jax.numpy as jnp
from jax import lax
from jax.experimental import pallas as pl
from jax.experimental.pallas import tpu as pltpu

F32 = jnp.float32
BF16 = jnp.bfloat16

HEAD_DIM = 64
SSM_WIDTH = 256
SSM_GROUPS = 16
SSM_STATE = 64
N_STATE = SSM_GROUPS * SSM_STATE
N_Q_HEADS = 6
N_KV_HEADS = 2
Q_WIDTH = N_Q_HEADS * HEAD_DIM
KV_WIDTH = N_KV_HEADS * HEAD_DIM
SWA_WINDOW = 128
MOBA_BLOCK = 256
MOBA_TOPK = 3
RMS_EPS = 1e-6
ATTN_SCALE = HEAD_DIM ** -0.5
LOG2E = math.log2(math.e)
LANES = 128
N_Q_TILES = Q_WIDTH // LANES
VMEM_LIMIT = 56 * 1024 * 1024
SUM_ROWS = 16
PREP_ROWS = 256
OFF_U, OFF_SG = 0, SSM_WIDTH
OFF_AQ = 2 * SSM_WIDTH
OFF_AG = OFF_AQ + Q_WIDTH
OFF_MQ = OFF_AG + Q_WIDTH
OFF_MG = OFF_MQ + Q_WIDTH
OFF_AKV = OFF_MG + Q_WIDTH
OFF_MKV = OFF_AKV + 2 * KV_WIDTH


def _tiles(seq):
    return min(512, seq), min(64, seq)


def _cparams(*sem):
    return pltpu.CompilerParams(dimension_semantics=sem, vmem_limit_bytes=VMEM_LIMIT)


def _silu(t):
    return t * jax.nn.sigmoid(t)


def _ssm_prep_kernel(lr_ref, li_ref, ldt_ref, br_ref, bi_ref,
                     are_ref, aim_ref, bbr_ref, bbi_ref):
    lr = lr_ref[...]
    li = li_ref[...]
    dt = jnp.exp(ldt_ref[...])
    mag = jnp.exp(lr * dt)
    a_re = mag * jnp.cos(li * dt)
    a_im = mag * jnp.sin(li * dt)
    den = lr * lr + li * li
    nr = a_re - 1.0
    ni = a_im
    cr = (nr * lr + ni * li) / den
    ci = (ni * lr - nr * li) / den
    br = br_ref[...]
    bi = bi_ref[...]
    are_ref[...] = a_re
    aim_ref[...] = a_im
    bbr_ref[...] = cr * br - ci * bi
    bbi_ref[...] = cr * bi + ci * br


def _ssm_prep(lam_re, lam_im, log_dt, b_re, b_im):
    d, g, p = lam_re.shape
    h = b_re.shape[-1]
    shp = (d * g, h, p)
    bc = lambda a: jnp.broadcast_to(a.reshape(d * g, 1, -1), shp).astype(F32)
    tr = lambda a: jnp.swapaxes(a, -1, -2).reshape(shp).astype(F32)
    outs = pl.pallas_call(
        _ssm_prep_kernel,
        out_shape=[jax.ShapeDtypeStruct(shp, F32)] * 4,
        name="ssm_prep",
    )(bc(lam_re), bc(lam_im), bc(log_dt[..., None]), tr(b_re), tr(b_im))
    a_re, a_im, bb_re, bb_im = outs
    return (a_re[:, 0, :].reshape(d, g, p), a_im[:, 0, :].reshape(d, g, p),
            bb_re.reshape(d, g, h, p), bb_im.reshape(d, g, h, p))


def _pair_rms_norm(t, gain2):
    lane = lax.broadcasted_iota(jnp.int32, t.shape, 1)
    lo = lane < HEAD_DIM
    sq = t * t
    s_lo = jnp.sum(jnp.where(lo, sq, 0.0), axis=-1, keepdims=True)
    s_hi = jnp.sum(jnp.where(lo, 0.0, sq), axis=-1, keepdims=True)
    r_lo = lax.rsqrt(s_lo * (1.0 / HEAD_DIM) + RMS_EPS)
    r_hi = lax.rsqrt(s_hi * (1.0 / HEAD_DIM) + RMS_EPS)
    return t * jnp.where(lo, r_lo, r_hi) * gain2


def _half_select(t, kv_head):
    lane = lax.broadcasted_iota(jnp.int32, t.shape, 1)
    keep = (lane < HEAD_DIM) if kv_head == 0 else (lane >= HEAD_DIM)
    return jnp.where(keep, t, 0.0)


def _inproj_kernel(x_ref, g_ref, w_ref, aqn_ref, akn_ref, mqn_ref, mkn_ref,
                   u_ref, sg_ref, aq_ref, ag_ref, mq_ref, mqs_ref, mg_ref, akv_ref, mkv_ref):
    x = x_ref[0]
    ms = jnp.mean(x * x, axis=-1, keepdims=True)
    h = (x * lax.rsqrt(ms + RMS_EPS) * g_ref[...]).astype(BF16)

    def mm(lo, width):
        return jnp.dot(h, w_ref[:, lo:lo + width], preferred_element_type=F32)

    def tile(t, j):
        return t[:, j * LANES:(j + 1) * LANES]

    ssm = mm(OFF_U, 2 * SSM_WIDTH)
    u_ref[0] = ssm[:, 0:SSM_WIDTH].astype(BF16)
    sg_ref[0] = _silu(ssm[:, SSM_WIDTH:2 * SSM_WIDTH]).astype(BF16)

    aq = mm(OFF_AQ, Q_WIDTH)
    mq = mm(OFF_MQ, Q_WIDTH)
    qsum = jnp.zeros((x.shape[0], LANES), F32)
    for j in range(N_Q_TILES):
        cols = slice(j * LANES, (j + 1) * LANES)
        aq_ref[0, :, cols] = (_pair_rms_norm(tile(aq, j), aqn_ref[...])
                              * (ATTN_SCALE * LOG2E)).astype(BF16)
        mqn = _pair_rms_norm(tile(mq, j), mqn_ref[...])
        qsum = qsum + mqn
        mq_ref[0, :, cols] = (mqn * (ATTN_SCALE * LOG2E)).astype(BF16)
    mqs_ref[0] = qsum
    ag_ref[0] = _silu(mm(OFF_AG, Q_WIDTH)).astype(BF16)
    mg_ref[0] = _silu(mm(OFF_MG, Q_WIDTH)).astype(BF16)

    for off, kn_ref, kv_ref in ((OFF_AKV, akn_ref, akv_ref), (OFF_MKV, mkn_ref, mkv_ref)):
        kv = mm(off, 2 * KV_WIDTH)
        kv_ref[0, :, 0:KV_WIDTH] = _pair_rms_norm(kv[:, 0:KV_WIDTH], kn_ref[...]).astype(BF16)
        kv_ref[0, :, KV_WIDTH:2 * KV_WIDTH] = kv[:, KV_WIDTH:2 * KV_WIDTH].astype(BF16)


def _inproj(x, norm_g, w_perm, head_norms, tl):
    b, l, d = x.shape
    dp = w_perm.shape[1]
    tm = jax.ShapeDtypeStruct((b, l, SSM_WIDTH), BF16)
    bq = jax.ShapeDtypeStruct((b, l, Q_WIDTH), BF16)
    bqs = jax.ShapeDtypeStruct((b, l, LANES), F32)
    bkv = jax.ShapeDtypeStruct((b, l, 2 * KV_WIDTH), BF16)
    tm_spec = pl.BlockSpec((1, tl, SSM_WIDTH), lambda i, t: (i, t, 0))
    q_spec = pl.BlockSpec((1, tl, Q_WIDTH), lambda i, t: (i, t, 0))
    qs_spec = pl.BlockSpec((1, tl, LANES), lambda i, t: (i, t, 0))
    kv_spec = pl.BlockSpec((1, tl, 2 * KV_WIDTH), lambda i, t: (i, t, 0))
    vec = pl.BlockSpec((1, LANES), lambda i, t: (0, 0))
    return pl.pallas_call(
        _inproj_kernel,
        grid=(b, l // tl),
        in_specs=[pl.BlockSpec((1, tl, d), lambda i, t: (i, t, 0)),
                  pl.BlockSpec((1, d), lambda i, t: (0, 0)),
                  pl.BlockSpec((d, dp), lambda i, t: (0, 0)),
                  vec, vec, vec, vec],
        out_specs=[tm_spec, tm_spec, q_spec, q_spec, q_spec, qs_spec, q_spec, kv_spec, kv_spec],
        out_shape=[tm, tm, bq, bq, bq, bqs, bq, bkv, bkv],
        compiler_params=_cparams("parallel", "parallel"),
        name="inproj",
    )(x, norm_g.reshape(1, d), w_perm, *head_norms)


ROW_PAD = 4
N_SLABS = 2 * N_STATE // LANES
RE_SLABS = N_SLABS // 2


def _ssm_kernel(nb, tc, u_ref, sg_ref, bmat_ref, are_ref, aim_ref, cre_ref, cim_ref,
                d_ref, gw_ref, gb_ref, y_ref, upad_s, bu_s, ypad_s, st_s):
    pitch = tc + ROW_PAD

    @pl.when(pl.program_id(0) == 0)
    def _():
        st_s[...] = jnp.zeros_like(st_s)
        upad_s[...] = jnp.zeros_like(upad_s)

    for b in range(nb):
        upad_s[pl.ds(b * pitch, tc), :] = u_ref[b].astype(F32)
    u = upad_s[...]
    bu = jnp.dot(u.astype(BF16), bmat_ref[...], preferred_element_type=F32)
    for j in range(N_SLABS):
        bu_s[j] = bu[:, j * LANES:(j + 1) * LANES]

    a_re = [jnp.broadcast_to(are_ref[:, j * LANES:(j + 1) * LANES], (nb, LANES))
            for j in range(RE_SLABS)]
    a_im = [jnp.broadcast_to(aim_ref[:, j * LANES:(j + 1) * LANES], (nb, LANES))
            for j in range(RE_SLABS)]

    def step(t, carry):
        out_re, out_im = [], []
        for j in range(RE_SLABS):
            xr, xi = carry[j], carry[RE_SLABS + j]
            rows = pl.ds(t, nb, stride=pitch)
            nr = a_re[j] * xr - a_im[j] * xi + bu_s[j, rows, :]
            ni = a_re[j] * xi + a_im[j] * xr + bu_s[RE_SLABS + j, rows, :]
            bu_s[j, rows, :] = nr
            bu_s[RE_SLABS + j, rows, :] = ni
            out_re.append(nr)
            out_im.append(ni)
        return tuple(out_re + out_im)

    state = lax.fori_loop(0, tc, step, tuple(st_s[j] for j in range(N_SLABS)))
    for j in range(N_SLABS):
        st_s[j] = state[j]

    xs_re = jnp.concatenate([bu_s[j] for j in range(RE_SLABS)], axis=1).astype(BF16)
    xs_im = jnp.concatenate([bu_s[RE_SLABS + j] for j in range(RE_SLABS)], axis=1).astype(BF16)
    y = (jnp.dot(xs_re, cre_ref[...], preferred_element_type=F32)
         - jnp.dot(xs_im, cim_ref[...], preferred_element_type=F32))
    y = y + d_ref[...] * u
    y = jax.nn.gelu(y)
    z = jnp.dot(y.astype(BF16), gw_ref[...], preferred_element_type=F32) + gb_ref[...]
    ypad_s[...] = y * jax.nn.sigmoid(z)
    for b in range(nb):
        y_ref[b] = (ypad_s[pl.ds(b * pitch, tc), :] * sg_ref[b].astype(F32)).astype(BF16)


def _ssm(u, sg, bmat, a_re, a_im, c_re, c_im, d, glu_w, glu_b, tc):
    nb, l, _ = u.shape
    rows = nb * (tc + ROW_PAD)
    blk_spec = pl.BlockSpec((nb, tc, SSM_WIDTH), lambda i: (0, i, 0))
    full = lambda a: pl.BlockSpec(a.shape, lambda i: (0,) * a.ndim)
    args = (bmat, a_re, a_im, c_re, c_im, d, glu_w, glu_b)
    return pl.pallas_call(
        functools.partial(_ssm_kernel, nb, tc),
        grid=(l // tc,),
        in_specs=[blk_spec, blk_spec] + [full(a) for a in args],
        out_specs=blk_spec,
        out_shape=jax.ShapeDtypeStruct((nb, l, SSM_WIDTH), BF16),
        scratch_shapes=[pltpu.VMEM((rows, SSM_WIDTH), F32),
                        pltpu.VMEM((N_SLABS, rows, LANES), F32),
                        pltpu.VMEM((rows, SSM_WIDTH), F32),
                        pltpu.VMEM((N_SLABS, nb, LANES), F32)],
        compiler_params=_cparams("arbitrary"),
        name="ssm",
    )(u, sg, *args)


def _sum_rows(n_keys):
    r = lax.broadcasted_iota(jnp.int32, (SUM_ROWS, 2 * n_keys), 0)
    c = lax.broadcasted_iota(jnp.int32, (SUM_ROWS, 2 * n_keys), 1)
    return jnp.where(((r == 0) & (c < n_keys)) | ((r == 1) & (c >= n_keys)), 1.0, 0.0).astype(BF16)


def _prep_keys_values(kv_ref, ks_s, vs_s, seq):
    vrow = lax.broadcasted_iota(jnp.int32, (KV_WIDTH, PREP_ROWS), 0)
    means = []
    for c in range(seq // PREP_ROWS):
        rows = slice(c * PREP_ROWS, (c + 1) * PREP_ROWS)
        kvb = kv_ref[0, rows, :].astype(F32)
        kf = kvb[:, 0:KV_WIDTH]
        means.append(jnp.mean(kf, axis=0, keepdims=True))
        ks_s[0, rows, :] = _half_select(kf, 0).astype(BF16)
        ks_s[1, rows, :] = _half_select(kf, 1).astype(BF16)
        vt = jnp.transpose(kvb[:, KV_WIDTH:2 * KV_WIDTH])
        vs_s[0, :, rows] = jnp.where(vrow < HEAD_DIM, vt, 0.0).astype(BF16)
        vs_s[1, :, rows] = jnp.where(vrow < HEAD_DIM, 0.0, vt).astype(BF16)
    return means


def _stacked(ks_s, vs_s, lo, hi, sum_rows):
    kst = jnp.concatenate([ks_s[0, lo:hi, :], ks_s[1, lo:hi, :]], axis=0)
    vst = jnp.concatenate([vs_s[0, :, lo:hi], vs_s[1, :, lo:hi]], axis=1)
    return kst, jnp.concatenate([vst, sum_rows], axis=0)


def _queries_t(q_ref, rows):
    return jnp.concatenate(
        [jnp.transpose(q_ref[0, rows, j * LANES:(j + 1) * LANES].astype(F32))
         for j in range(N_Q_TILES)], axis=1).astype(BF16)


def _store_gated(o_ref, g_ref, rows, o_t, width):
    for j in range(N_Q_TILES):
        cols = slice(j * LANES, (j + 1) * LANES)
        o = jnp.transpose(o_t[:, j * width:(j + 1) * width]) * g_ref[0, rows, cols].astype(F32)
        o_ref[0, rows, cols] = o.astype(BF16)


_ATTN_SCRATCH = lambda seq: [pltpu.VMEM((N_KV_HEADS, seq, KV_WIDTH), BF16),
                             pltpu.VMEM((N_KV_HEADS, KV_WIDTH, seq), BF16)]


def _swa_kernel(nblk, sink_ref, q_ref, g_ref, kv_ref, o_ref, ks_s, vs_s):
    w = SWA_WINDOW
    nq = N_Q_TILES * w
    _prep_keys_values(kv_ref, ks_s, vs_s, nblk * w)

    lane = lax.broadcasted_iota(jnp.int32, (1, nq), 1)
    sink2 = []
    for h in range(N_KV_HEADS):
        a = [sink_ref[0, j + N_Q_TILES * h] * LOG2E for j in range(N_Q_TILES)]
        sink2.append(jnp.where(lane < w, a[0], jnp.where(lane < 2 * w, a[1], a[2])))
    lo_rows = lax.broadcasted_iota(jnp.int32, (KV_WIDTH, nq), 0) < HEAD_DIM

    def window_mask(n_keys):
        kpos = (lax.broadcasted_iota(jnp.int32, (2 * n_keys, nq), 0) & (n_keys - 1)) - (n_keys - w)
        qpos = lax.broadcasted_iota(jnp.int32, (2 * n_keys, nq), 1) & (w - 1)
        rel = qpos - kpos
        return (rel >= 0) & (rel < w)

    masks = {w: window_mask(w), 2 * w: window_mask(2 * w)}
    sums = {w: _sum_rows(w), 2 * w: _sum_rows(2 * w)}

    for n in range(nblk):
        rows = slice(n * w, (n + 1) * w)
        lo = max(n - 1, 0) * w
        hi = (n + 1) * w
        nk = hi - lo
        kst, vst = _stacked(ks_s, vs_s, lo, hi, sums[nk])
        q_t = _queries_t(q_ref, rows)
        s = jnp.where(masks[nk], jnp.dot(kst, q_t, preferred_element_type=F32), -jnp.inf)
        m = [jnp.maximum(jnp.max(s[h * nk:(h + 1) * nk], axis=0, keepdims=True), sink2[h])
             for h in range(N_KV_HEADS)]
        p = jnp.concatenate([jnp.exp2(s[h * nk:(h + 1) * nk] - m[h]) for h in range(N_KV_HEADS)],
                            axis=0).astype(BF16)
        pv = jnp.dot(vst, p, preferred_element_type=F32)
        den = [pv[KV_WIDTH + h:KV_WIDTH + h + 1] + jnp.exp2(sink2[h] - m[h])
               for h in range(N_KV_HEADS)]
        o_t = pv[0:KV_WIDTH] * jnp.where(lo_rows, 1.0 / den[0], 1.0 / den[1])
        _store_gated(o_ref, g_ref, rows, o_t, w)


def _swa(q, g, kv, sink):
    b, l, _ = q.shape
    qspec = pl.BlockSpec((1, l, Q_WIDTH), lambda i: (i, 0, 0))
    return pl.pallas_call(
        functools.partial(_swa_kernel, l // SWA_WINDOW),
        grid=(b,),
        in_specs=[pl.BlockSpec(memory_space=pltpu.SMEM),
                  qspec, qspec,
                  pl.BlockSpec((1, l, 2 * KV_WIDTH), lambda i: (i, 0, 0))],
        out_specs=qspec,
        out_shape=jax.ShapeDtypeStruct((b, l, Q_WIDTH), BF16),
        scratch_shapes=_ATTN_SCRATCH(l),
        compiler_params=_cparams("parallel"),
        name="swa",
    )(sink, q, g, kv)


def _moba_kernel(nblk, q_ref, qs_ref, g_ref, kv_ref, o_ref, ks_s, vs_s):
    blk = MOBA_BLOCK
    assert blk == PREP_ROWS
    nq = N_Q_TILES * blk
    kmean = jnp.concatenate(_prep_keys_values(kv_ref, ks_s, vs_s, nblk * blk), axis=0)
    sum_rows = _sum_rows(blk)

    blk_id = lax.broadcasted_iota(jnp.int32, (nblk, blk), 0)
    ki = lax.broadcasted_iota(jnp.int32, (2 * blk, nq), 0) & (blk - 1)
    qi = lax.broadcasted_iota(jnp.int32, (2 * blk, nq), 1) & (blk - 1)
    causal = ki <= qi
    lo = lax.broadcasted_iota(jnp.int32, (KV_WIDTH, nq), 0) < HEAD_DIM

    for i in range(nblk):
        rows = slice(i * blk, (i + 1) * blk)
        q_t = _queries_t(q_ref, rows)
        qsum = qs_ref[0, rows, :]

        past = blk_id < i
        sel = []
        for h in range(N_KV_HEADS):
            gate = lax.dot_general(_half_select(kmean, h), _half_select(qsum, h),
                                   (((1,), (1,)), ((), ())),
                                   precision=lax.Precision.HIGHEST,
                                   preferred_element_type=F32)
            gate = jnp.where(past, gate, -jnp.inf)
            rank = jnp.zeros((nblk, blk), jnp.int32)
            for n2 in range(nblk):
                other = gate[n2:n2 + 1, :]
                ahead = (other > gate) | ((other == gate) & (n2 < blk_id))
                rank = rank + ahead.astype(jnp.int32)
            chosen = jnp.where((rank < MOBA_TOPK) & past, 1.0, 0.0)
            sel.append(jnp.concatenate([chosen] * N_Q_TILES, axis=1))

        kst, vst = _stacked(ks_s, vs_s, i * blk, (i + 1) * blk, sum_rows)
        s = jnp.where(causal, jnp.dot(kst, q_t, preferred_element_type=F32), -jnp.inf)
        m = [jnp.max(s[h * blk:(h + 1) * blk], axis=0, keepdims=True) for h in range(N_KV_HEADS)]
        p = jnp.concatenate([jnp.exp2(s[h * blk:(h + 1) * blk] - m[h]) for h in range(N_KV_HEADS)],
                            axis=0).astype(BF16)
        pv = jnp.dot(vst, p, preferred_element_type=F32)
        acc = pv[0:KV_WIDTH]
        den = [pv[KV_WIDTH + h:KV_WIDTH + h + 1] for h in range(N_KV_HEADS)]

        for n in range(i):
            kst, vst = _stacked(ks_s, vs_s, n * blk, (n + 1) * blk, sum_rows)
            s = jnp.dot(kst, q_t, preferred_element_type=F32)
            alpha = []
            parts = []
            for h in range(N_KV_HEADS):
                ok = sel[h][n:n + 1, :] > 0.5
                s_h = s[h * blk:(h + 1) * blk]
                m_new = jnp.where(ok, jnp.maximum(m[h], jnp.max(s_h, axis=0, keepdims=True)), m[h])
                alpha.append(jnp.exp2(m[h] - m_new))
                parts.append(jnp.exp2(s_h - jnp.where(ok, m_new, jnp.inf)))
                m[h] = m_new
            p = jnp.concatenate(parts, axis=0).astype(BF16)
            pv = jnp.dot(vst, p, preferred_element_type=F32)
            acc = acc * jnp.where(lo, alpha[0], alpha[1]) + pv[0:KV_WIDTH]
            den = [alpha[h] * den[h] + pv[KV_WIDTH + h:KV_WIDTH + h + 1] for h in range(N_KV_HEADS)]

        o_t = acc * jnp.where(lo, 1.0 / den[0], 1.0 / den[1])
        _store_gated(o_ref, g_ref, rows, o_t, blk)


def _moba(q, qsum, g, kv):
    b, l, _ = q.shape
    qspec = pl.BlockSpec((1, l, Q_WIDTH), lambda i: (i, 0, 0))
    return pl.pallas_call(
        functools.partial(_moba_kernel, l // MOBA_BLOCK),
        grid=(b,),
        in_specs=[qspec,
                  pl.BlockSpec((1, l, LANES), lambda i: (i, 0, 0)),
                  qspec,
                  pl.BlockSpec((1, l, 2 * KV_WIDTH), lambda i: (i, 0, 0))],
        out_specs=qspec,
        out_shape=jax.ShapeDtypeStruct((b, l, Q_WIDTH), BF16),
        scratch_shapes=_ATTN_SCRATCH(l),
        compiler_params=_cparams("parallel"),
        name="moba",
    )(q, qsum, g, kv)


def _outproj_kernel(x_ref, ys_ref, ya_ref, ym_ref, w_ref, o_ref):
    y = jnp.concatenate([ys_ref[0], ya_ref[0], ym_ref[0]], axis=1)
    o_ref[0] = x_ref[0] + jnp.dot(y, w_ref[...], preferred_element_type=F32)


def _outproj(x, y_ssm, y_swa, y_moba, w_perm, tl):
    b, l, d = x.shape
    qspec = pl.BlockSpec((1, tl, Q_WIDTH), lambda i, t: (i, t, 0))
    xspec = pl.BlockSpec((1, tl, d), lambda i, t: (i, t, 0))
    return pl.pallas_call(
        _outproj_kernel,
        grid=(b, l // tl),
        in_specs=[xspec,
                  pl.BlockSpec((1, tl, SSM_WIDTH), lambda i, t: (i, t, 0)),
                  qspec, qspec,
                  pl.BlockSpec(w_perm.shape, lambda i, t: (0, 0))],
        out_specs=xspec,
        out_shape=jax.ShapeDtypeStruct((b, l, d), F32),
        compiler_params=_cparams("parallel", "parallel"),
        name="outproj",
    )(x, y_ssm, y_swa, y_moba, w_perm)


def _head_tiles(w):
    lead = w.shape[:-1]
    w = w.reshape(*lead, N_KV_HEADS, N_Q_TILES, HEAD_DIM)
    return jnp.swapaxes(w, -3, -2).reshape(*lead, Q_WIDTH)


def _permute_w_in(w):
    s, q, k = SSM_WIDTH, Q_WIDTH, KV_WIDTH
    bounds = [0]
    for n in (s, s, q, k, k, q, q, k, k, q):
        bounds.append(bounds[-1] + n)
    s_u, s_g, a_q, a_k, a_v, a_g, m_q, m_k, m_v, m_g = [
        w[..., bounds[i]:bounds[i + 1]] for i in range(10)]
    return jnp.concatenate(
        [s_u, s_g, _head_tiles(a_q), _head_tiles(a_g), _head_tiles(m_q), _head_tiles(m_g),
         a_k, a_v, m_k, m_v], axis=-1).astype(BF16)


def _permute_w_out(w):
    s, q = SSM_WIDTH, Q_WIDTH
    rows_t = lambda part: jnp.swapaxes(_head_tiles(jnp.swapaxes(part, -1, -2)), -1, -2)
    return jnp.concatenate([w[:, 0:s], rows_t(w[:, s:s + q]), rows_t(w[:, s + q:])],
                           axis=1).astype(BF16)


def _block_diag_in(bb):
    d, g, h, p = bb.shape
    eye = jnp.eye(g, dtype=bb.dtype)[None, :, None, :, None]
    return (bb[:, :, :, None, :] * eye).reshape(d, g * h, g * p)


def _block_diag_out(c):
    d, g, h, p = c.shape
    eye = jnp.eye(g, dtype=c.dtype)[None, :, None, :, None]
    return (jnp.swapaxes(c, -1, -2)[:, :, :, None, :] * eye).reshape(d, g * p, g * h)


def kernel(x, norm_g, w_in, ssm_lam_re, ssm_lam_im, ssm_log_dt, ssm_b_re, ssm_b_im,
           ssm_c_re, ssm_c_im, ssm_d, ssm_glu_w, ssm_glu_b, swa_q_norm, swa_k_norm,
           swa_sink, moba_q_norm, moba_k_norm, w_out):
    b, l, d = x.shape
    depth = norm_g.shape[0]
    tl, tc = _tiles(l)
    a_re, a_im, bb_re, bb_im = _ssm_prep(ssm_lam_re, ssm_lam_im, ssm_log_dt, ssm_b_re, ssm_b_im)
    a_re = a_re.reshape(depth, 1, N_STATE)
    a_im = a_im.reshape(depth, 1, N_STATE)
    w_in_p = _permute_w_in(w_in)
    w_out_p = _permute_w_out(w_out)
    bmat = jnp.concatenate([_block_diag_in(bb_re), _block_diag_in(bb_im)], axis=-1).astype(BF16)
    c_re = _block_diag_out(ssm_c_re).astype(BF16)
    c_im = _block_diag_out(ssm_c_im).astype(BF16)
    glu_w = ssm_glu_w.astype(BF16)
    two = lambda v: jnp.concatenate([v, v], axis=-1).reshape(depth, 1, LANES).astype(F32)
    head_norms = [two(v) for v in (swa_q_norm, swa_k_norm, moba_q_norm, moba_k_norm)]
    ssm_d = ssm_d.reshape(depth, 1, SSM_WIDTH).astype(F32)
    glu_b = ssm_glu_b.reshape(depth, 1, SSM_WIDTH).astype(F32)
    sink = swa_sink.reshape(depth, 1, N_Q_HEADS).astype(F32)

    for layer in range(depth):
        s_u, s_g, a_q, a_g, m_q, m_qsum, m_g, a_kv, m_kv = _inproj(
            x, norm_g[layer], w_in_p[layer], [v[layer] for v in head_norms], tl)
        y_ssm = _ssm(s_u, s_g, bmat[layer], a_re[layer], a_im[layer], c_re[layer], c_im[layer],
                     ssm_d[layer], glu_w[layer], glu_b[layer], tc)
        y_swa = _swa(a_q, a_g, a_kv, sink[layer])
        y_moba = _moba(m_q, m_qsum, m_g, m_kv)
        x = _outproj(x, y_ssm, y_swa, y_moba, w_out_p[layer], tl)
    return x
```

```python
import functools
import math

import jax
import jax.numpy as jnp
from jax import lax
from jax.experimental import pallas as pl
from jax.experimental.pallas import tpu as pltpu

F32 = jnp.float32
BF16 = jnp.bfloat16

HEAD_DIM = 64
SSM_WIDTH = 256
SSM_GROUPS = 16
SSM_STATE = 64
N_STATE = SSM_GROUPS * SSM_STATE
N_Q_HEADS = 6
N_KV_HEADS = 2
Q_WIDTH = N_Q_HEADS * HEAD_DIM
KV_WIDTH = N_KV_HEADS * HEAD_DIM
SWA_WINDOW = 128
MOBA_BLOCK = 256
MOBA_TOPK = 3
RMS_EPS = 1e-6
ATTN_SCALE = HEAD_DIM ** -0.5
LOG2E = math.log2(math.e)
LANES = 128
N_Q_TILES = Q_WIDTH // LANES
VMEM_LIMIT = 56 * 1024 * 1024
SUM_ROWS = 16
PREP_ROWS = 256
OFF_U, OFF_SG = 0, SSM_WIDTH
OFF_AQ = 2 * SSM_WIDTH
OFF_AG = OFF_AQ + Q_WIDTH
OFF_MQ = OFF_AG + Q_WIDTH
OFF_MG = OFF_MQ + Q_WIDTH
OFF_AKV = OFF_MG + Q_WIDTH
OFF_MKV = OFF_AKV + 2 * KV_WIDTH
PROJ_CHUNK = 512
LOGIT_BOUND = 60.0
MASK_BIAS = -300.0


def _tiles(seq):
    return min(512, seq), min(64, seq)


def _cparams(*sem):
    return pltpu.CompilerParams(dimension_semantics=sem, vmem_limit_bytes=VMEM_LIMIT)


def _silu(t):
    return t * jax.nn.sigmoid(t)


def _ssm_prep_kernel(lr_ref, li_ref, ldt_ref, br_ref, bi_ref,
                     are_ref, aim_ref, bbr_ref, bbi_ref):
    lr = lr_ref[...]
    li = li_ref[...]
    dt = jnp.exp(ldt_ref[...])
    mag = jnp.exp(lr * dt)
    a_re = mag * jnp.cos(li * dt)
    a_im = mag * jnp.sin(li * dt)
    den = lr * lr + li * li
    nr = a_re - 1.0
    ni = a_im
    cr = (nr * lr + ni * li) / den
    ci = (ni * lr - nr * li) / den
    br = br_ref[...]
    bi = bi_ref[...]
    are_ref[...] = a_re
    aim_ref[...] = a_im
    bbr_ref[...] = cr * br - ci * bi
    bbi_ref[...] = cr * bi + ci * br


def _ssm_prep(lam_re, lam_im, log_dt, b_re, b_im):
    d, g, p = lam_re.shape
    h = b_re.shape[-1]
    shp = (d * g, h, p)
    bc = lambda a: jnp.broadcast_to(a.reshape(d * g, 1, -1), shp).astype(F32)
    tr = lambda a: jnp.swapaxes(a, -1, -2).reshape(shp).astype(F32)
    outs = pl.pallas_call(
        _ssm_prep_kernel,
        out_shape=[jax.ShapeDtypeStruct(shp, F32)] * 4,
        name="ssm_prep",
    )(bc(lam_re), bc(lam_im), bc(log_dt[..., None]), tr(b_re), tr(b_im))
    a_re, a_im, bb_re, bb_im = outs
    return (a_re[:, 0, :].reshape(d, g, p), a_im[:, 0, :].reshape(d, g, p),
            bb_re.reshape(d, g, h, p), bb_im.reshape(d, g, h, p))


def _pair_rms_norm(t, gain2):
    lane = lax.broadcasted_iota(jnp.int32, t.shape, 1)
    lo = lane < HEAD_DIM
    sq = t * t
    s_lo = jnp.sum(jnp.where(lo, sq, 0.0), axis=-1, keepdims=True)
    s_hi = jnp.sum(jnp.where(lo, 0.0, sq), axis=-1, keepdims=True)
    r_lo = lax.rsqrt(s_lo * (1.0 / HEAD_DIM) + RMS_EPS)
    r_hi = lax.rsqrt(s_hi * (1.0 / HEAD_DIM) + RMS_EPS)
    return t * jnp.where(lo, r_lo, r_hi) * gain2


def _half_select(t, kv_head):
    lane = lax.broadcasted_iota(jnp.int32, t.shape, 1)
    keep = (lane < HEAD_DIM) if kv_head == 0 else (lane >= HEAD_DIM)
    return jnp.where(keep, t, 0.0)


def _inproj_kernel(x_ref, g_ref, w_ref, aqn_ref, akn_ref, mqn_ref, mkn_ref,
                   u_ref, sg_ref, aq_ref, ag_ref, mq_ref, mqs_ref, mg_ref, akv_ref, mkv_ref):
    x = x_ref[0]
    ms = jnp.mean(x * x, axis=-1, keepdims=True)
    h = (x * lax.rsqrt(ms + RMS_EPS) * g_ref[...]).astype(BF16)

    chunks = {}

    def tile(col):
        c, k = divmod(col, PROJ_CHUNK)
        if c not in chunks:
            chunks[c] = jnp.dot(h, w_ref[:, c * PROJ_CHUNK:(c + 1) * PROJ_CHUNK],
                                preferred_element_type=F32)
        return chunks[c][:, k:k + LANES]

    def lanes(j):
        return slice(j * LANES, (j + 1) * LANES)

    for j in range(SSM_WIDTH // LANES):
        u_ref[0, :, lanes(j)] = tile(OFF_U + j * LANES).astype(BF16)
        sg_ref[0, :, lanes(j)] = _silu(tile(OFF_SG + j * LANES)).astype(BF16)

    qsum = jnp.zeros((x.shape[0], LANES), F32)
    for j in range(N_Q_TILES):
        aq_ref[0, :, lanes(j)] = (_pair_rms_norm(tile(OFF_AQ + j * LANES), aqn_ref[...])
                                  * (ATTN_SCALE * LOG2E)).astype(BF16)
        ag_ref[0, :, lanes(j)] = _silu(tile(OFF_AG + j * LANES)).astype(BF16)
        mqn = _pair_rms_norm(tile(OFF_MQ + j * LANES), mqn_ref[...])
        qsum = qsum + mqn
        mq_ref[0, :, lanes(j)] = (mqn * (ATTN_SCALE * LOG2E)).astype(BF16)
        mg_ref[0, :, lanes(j)] = _silu(tile(OFF_MG + j * LANES)).astype(BF16)
    mqs_ref[0] = qsum

    for off, kn_ref, kv_ref in ((OFF_AKV, akn_ref, akv_ref), (OFF_MKV, mkn_ref, mkv_ref)):
        kv_ref[0, :, lanes(0)] = _pair_rms_norm(tile(off), kn_ref[...]).astype(BF16)
        kv_ref[0, :, lanes(1)] = tile(off + KV_WIDTH).astype(BF16)


def _inproj(x, norm_g, w_perm, head_norms, tl):
    b, l, d = x.shape
    dp = w_perm.shape[1]
    tm = jax.ShapeDtypeStruct((b, l, SSM_WIDTH), BF16)
    bq = jax.ShapeDtypeStruct((b, l, Q_WIDTH), BF16)
    bqs = jax.ShapeDtypeStruct((b, l, LANES), F32)
    bkv = jax.ShapeDtypeStruct((b, l, 2 * KV_WIDTH), BF16)
    tm_spec = pl.BlockSpec((1, tl, SSM_WIDTH), lambda i, t: (i, t, 0))
    q_spec = pl.BlockSpec((1, tl, Q_WIDTH), lambda i, t: (i, t, 0))
    qs_spec = pl.BlockSpec((1, tl, LANES), lambda i, t: (i, t, 0))
    kv_spec = pl.BlockSpec((1, tl, 2 * KV_WIDTH), lambda i, t: (i, t, 0))
    vec = pl.BlockSpec((1, LANES), lambda i, t: (0, 0))
    return pl.pallas_call(
        _inproj_kernel,
        grid=(b, l // tl),
        in_specs=[pl.BlockSpec((1, tl, d), lambda i, t: (i, t, 0)),
                  pl.BlockSpec((1, d), lambda i, t: (0, 0)),
                  pl.BlockSpec((d, dp), lambda i, t: (0, 0)),
                  vec, vec, vec, vec],
        out_specs=[tm_spec, tm_spec, q_spec, q_spec, q_spec, qs_spec, q_spec, kv_spec, kv_spec],
        out_shape=[tm, tm, bq, bq, bq, bqs, bq, bkv, bkv],
        compiler_params=_cparams("parallel", "parallel"),
        name="inproj",
    )(x, norm_g.reshape(1, d), w_perm, *head_norms)


ROW_PAD = 4
N_SLABS = 2 * N_STATE // LANES
RE_SLABS = N_SLABS // 2


def _ssm_kernel(nb, tc, u_ref, sg_ref, bmat_ref, are_ref, aim_ref, cre_ref, cim_ref,
                d_ref, gw_ref, gb_ref, y_ref, upad_s, bu_s, ypad_s, st_s):
    pitch = tc + ROW_PAD

    @pl.when(pl.program_id(0) == 0)
    def _():
        st_s[...] = jnp.zeros_like(st_s)
        upad_s[...] = jnp.zeros_like(upad_s)

    for b in range(nb):
        upad_s[pl.ds(b * pitch, tc), :] = u_ref[b].astype(F32)
    u = upad_s[...]
    bu = jnp.dot(u.astype(BF16), bmat_ref[...], preferred_element_type=F32)
    for j in range(N_SLABS):
        bu_s[j] = bu[:, j * LANES:(j + 1) * LANES]

    a_re = [jnp.broadcast_to(are_ref[:, j * LANES:(j + 1) * LANES], (nb, LANES))
            for j in range(RE_SLABS)]
    a_im = [jnp.broadcast_to(aim_ref[:, j * LANES:(j + 1) * LANES], (nb, LANES))
            for j in range(RE_SLABS)]

    def step(t, carry):
        out_re, out_im = [], []
        for j in range(RE_SLABS):
            xr, xi = carry[j], carry[RE_SLABS + j]
            rows = pl.ds(t, nb, stride=pitch)
            nr = a_re[j] * xr - a_im[j] * xi + bu_s[j, rows, :]
            ni = a_re[j] * xi + a_im[j] * xr + bu_s[RE_SLABS + j, rows, :]
            bu_s[j, rows, :] = nr
            bu_s[RE_SLABS + j, rows, :] = ni
            out_re.append(nr)
            out_im.append(ni)
        return tuple(out_re + out_im)

    state = lax.fori_loop(0, tc, step, tuple(st_s[j] for j in range(N_SLABS)), unroll=2)
    for j in range(N_SLABS):
        st_s[j] = state[j]

    xs_re = jnp.concatenate([bu_s[j] for j in range(RE_SLABS)], axis=1).astype(BF16)
    xs_im = jnp.concatenate([bu_s[RE_SLABS + j] for j in range(RE_SLABS)], axis=1).astype(BF16)
    y = (jnp.dot(xs_re, cre_ref[...], preferred_element_type=F32)
         - jnp.dot(xs_im, cim_ref[...], preferred_element_type=F32))
    y = y + d_ref[...] * u
    y = jax.nn.gelu(y)
    z = jnp.dot(y.astype(BF16), gw_ref[...], preferred_element_type=F32) + gb_ref[...]
    ypad_s[...] = y * jax.nn.sigmoid(z)
    for b in range(nb):
        y_ref[b] = (ypad_s[pl.ds(b * pitch, tc), :] * sg_ref[b].astype(F32)).astype(BF16)


def _ssm(u, sg, bmat, a_re, a_im, c_re, c_im, d, glu_w, glu_b, tc):
    nb, l, _ = u.shape
    rows = nb * (tc + ROW_PAD)
    blk_spec = pl.BlockSpec((nb, tc, SSM_WIDTH), lambda i: (0, i, 0))
    full = lambda a: pl.BlockSpec(a.shape, lambda i: (0,) * a.ndim)
    args = (bmat, a_re, a_im, c_re, c_im, d, glu_w, glu_b)
    return pl.pallas_call(
        functools.partial(_ssm_kernel, nb, tc),
        grid=(l // tc,),
        in_specs=[blk_spec, blk_spec] + [full(a) for a in args],
        out_specs=blk_spec,
        out_shape=jax.ShapeDtypeStruct((nb, l, SSM_WIDTH), BF16),
        scratch_shapes=[pltpu.VMEM((rows, SSM_WIDTH), F32),
                        pltpu.VMEM((N_SLABS, rows, LANES), F32),
                        pltpu.VMEM((rows, SSM_WIDTH), F32),
                        pltpu.VMEM((N_SLABS, nb, LANES), F32)],
        compiler_params=_cparams("arbitrary"),
        name="ssm",
    )(u, sg, *args)


def _sum_row(n_keys):
    r = lax.broadcasted_iota(jnp.int32, (SUM_ROWS, n_keys), 0)
    return jnp.where(r == 0, 1.0, 0.0).astype(BF16)


def _prep_keys_values(kv_ref, ks_s, vs_s, seq):
    means = []
    for c in range(seq // PREP_ROWS):
        rows = slice(c * PREP_ROWS, (c + 1) * PREP_ROWS)
        kvb = kv_ref[0, rows, :].astype(F32)
        kf = kvb[:, 0:KV_WIDTH]
        means.append(jnp.mean(kf, axis=0, keepdims=True))
        ks_s[0, rows, :] = _half_select(kf, 0).astype(BF16)
        ks_s[1, rows, :] = _half_select(kf, 1).astype(BF16)
        vs_s[:, rows] = jnp.transpose(kvb[:, KV_WIDTH:2 * KV_WIDTH]).astype(BF16)
    return means


def _stacked_keys(ks_s, lo, hi):
    return jnp.concatenate([ks_s[0, lo:hi, :], ks_s[1, lo:hi, :]], axis=0)


def _weighted_values(vs_s, lo, hi, p, sum_row):
    nk = hi - lo
    outs, sums = [], []
    for h in range(N_KV_HEADS):
        v_t = jnp.concatenate([vs_s[h * HEAD_DIM:(h + 1) * HEAD_DIM, lo:hi], sum_row], axis=0)
        pv = jnp.dot(v_t, p[h * nk:(h + 1) * nk], preferred_element_type=F32)
        outs.append(pv[0:HEAD_DIM])
        sums.append(pv[HEAD_DIM:HEAD_DIM + 1])
    return outs, sums


def _queries_t(q_ref, rows):
    return jnp.concatenate(
        [jnp.transpose(q_ref[0, rows, j * LANES:(j + 1) * LANES].astype(F32))
         for j in range(N_Q_TILES)], axis=1).astype(BF16)


def _store_gated(o_ref, g_ref, rows, o_t, width):
    for j in range(N_Q_TILES):
        cols = slice(j * LANES, (j + 1) * LANES)
        o = jnp.transpose(o_t[:, j * width:(j + 1) * width]) * g_ref[0, rows, cols].astype(F32)
        o_ref[0, rows, cols] = o.astype(BF16)


_ATTN_SCRATCH = lambda seq: [pltpu.VMEM((N_KV_HEADS, seq, KV_WIDTH), BF16),
                             pltpu.VMEM((KV_WIDTH, seq), BF16)]


def _swa_kernel(nblk, shifted, sink_ref, q_ref, g_ref, kv_ref, o_ref, ks_s, vs_s):
    w = SWA_WINDOW
    nq = N_Q_TILES * w
    _prep_keys_values(kv_ref, ks_s, vs_s, nblk * w)

    lane = lax.broadcasted_iota(jnp.int32, (1, nq), 1)
    sink2 = []
    for h in range(N_KV_HEADS):
        a = [sink_ref[0, j + N_Q_TILES * h] * LOG2E for j in range(N_Q_TILES)]
        sink2.append(jnp.where(lane < w, a[0], jnp.where(lane < 2 * w, a[1], a[2])))

    def window_mask(n_keys):
        kpos = (lax.broadcasted_iota(jnp.int32, (2 * n_keys, nq), 0) & (n_keys - 1)) - (n_keys - w)
        qpos = lax.broadcasted_iota(jnp.int32, (2 * n_keys, nq), 1) & (w - 1)
        rel = qpos - kpos
        return (rel >= 0) & (rel < w)

    masks = {w: window_mask(w), 2 * w: window_mask(2 * w)}
    sums = {w: _sum_row(w), 2 * w: _sum_row(2 * w)}

    for n in range(nblk):
        rows = slice(n * w, (n + 1) * w)
        lo = max(n - 1, 0) * w
        hi = (n + 1) * w
        nk = hi - lo
        q_t = _queries_t(q_ref, rows)
        s = jnp.where(masks[nk],
                      jnp.dot(_stacked_keys(ks_s, lo, hi), q_t, preferred_element_type=F32), -jnp.inf)
        if shifted:
            m = [jnp.maximum(jnp.max(s[h * nk:(h + 1) * nk], axis=0, keepdims=True), sink2[h])
                 for h in range(N_KV_HEADS)]
            p = jnp.concatenate([jnp.exp2(s[h * nk:(h + 1) * nk] - m[h])
                                 for h in range(N_KV_HEADS)], axis=0).astype(BF16)
            sink_p = [jnp.exp2(sink2[h] - m[h]) for h in range(N_KV_HEADS)]
        else:
            p = jnp.exp2(s).astype(BF16)
            sink_p = [jnp.exp2(sink2[h]) for h in range(N_KV_HEADS)]
        outs, sums_p = _weighted_values(vs_s, lo, hi, p, sums[nk])
        o_t = jnp.concatenate([outs[h] * (1.0 / (sums_p[h] + sink_p[h]))
                               for h in range(N_KV_HEADS)], axis=0)
        _store_gated(o_ref, g_ref, rows, o_t, w)


def _swa(shifted, q, g, kv, sink):
    b, l, _ = q.shape
    qspec = pl.BlockSpec((1, l, Q_WIDTH), lambda i: (i, 0, 0))
    return pl.pallas_call(
        functools.partial(_swa_kernel, l // SWA_WINDOW, shifted),
        grid=(b,),
        in_specs=[pl.BlockSpec(memory_space=pltpu.SMEM),
                  qspec, qspec,
                  pl.BlockSpec((1, l, 2 * KV_WIDTH), lambda i: (i, 0, 0))],
        out_specs=qspec,
        out_shape=jax.ShapeDtypeStruct((b, l, Q_WIDTH), BF16),
        scratch_shapes=_ATTN_SCRATCH(l),
        compiler_params=_cparams("parallel"),
        name="swa",
    )(sink, q, g, kv)


def _moba_kernel(nblk, shifted, q_ref, qs_ref, g_ref, kv_ref, o_ref, ks_s, vs_s):
    blk = MOBA_BLOCK
    assert blk == PREP_ROWS
    nq = N_Q_TILES * blk
    kmean = jnp.concatenate(_prep_keys_values(kv_ref, ks_s, vs_s, nblk * blk), axis=0)
    sum_row = _sum_row(blk)
    if not shifted:
        r = lax.broadcasted_iota(jnp.int32, (2 * blk, LANES), 0)
        c = lax.broadcasted_iota(jnp.int32, (2 * blk, LANES), 1)
        head_cols = jnp.where(((c == 0) & (r < blk)) | ((c == 1) & (r >= blk)), 1.0, 0.0).astype(BF16)
        bias_row = lax.broadcasted_iota(jnp.int32, (SUM_ROWS, nq), 0)
        bias_pad = jnp.zeros((LANES - SUM_ROWS, nq), BF16)

    blk_id = lax.broadcasted_iota(jnp.int32, (nblk, blk), 0)
    ki = lax.broadcasted_iota(jnp.int32, (2 * blk, nq), 0) & (blk - 1)
    qi = lax.broadcasted_iota(jnp.int32, (2 * blk, nq), 1) & (blk - 1)
    causal = ki <= qi

    for i in range(nblk):
        rows = slice(i * blk, (i + 1) * blk)
        q_t = _queries_t(q_ref, rows)
        qsum = qs_ref[0, rows, :]

        past = blk_id < i
        sel = []
        for h in range(N_KV_HEADS):
            gate = lax.dot_general(_half_select(kmean, h), _half_select(qsum, h),
                                   (((1,), (1,)), ((), ())),
                                   precision=lax.Precision.HIGHEST,
                                   preferred_element_type=F32)
            gate = jnp.where(past, gate, -jnp.inf)
            rank = jnp.zeros((nblk, blk), jnp.int32)
            for n2 in range(nblk):
                other = gate[n2:n2 + 1, :]
                ahead = (other > gate) | ((other == gate) & (n2 < blk_id))
                rank = rank + ahead.astype(jnp.int32)
            chosen = jnp.where((rank < MOBA_TOPK) & past, 1.0, 0.0)
            sel.append(jnp.concatenate([chosen] * N_Q_TILES, axis=1))

        kst = _stacked_keys(ks_s, i * blk, (i + 1) * blk)
        s = jnp.where(causal, jnp.dot(kst, q_t, preferred_element_type=F32), -jnp.inf)
        if shifted:
            m = [jnp.max(s[h * blk:(h + 1) * blk], axis=0, keepdims=True)
                 for h in range(N_KV_HEADS)]
            p = jnp.concatenate([jnp.exp2(s[h * blk:(h + 1) * blk] - m[h])
                                 for h in range(N_KV_HEADS)], axis=0).astype(BF16)
        else:
            p = jnp.exp2(s).astype(BF16)
        acc, den = _weighted_values(vs_s, i * blk, (i + 1) * blk, p, sum_row)

        for n in range(i):
            kst = _stacked_keys(ks_s, n * blk, (n + 1) * blk)
            ok = [sel[h][n:n + 1, :] > 0.5 for h in range(N_KV_HEADS)]
            if shifted:
                s = jnp.dot(kst, q_t, preferred_element_type=F32)
                alpha = []
                parts = []
                for h in range(N_KV_HEADS):
                    s_h = s[h * blk:(h + 1) * blk]
                    m_new = jnp.where(ok[h], jnp.maximum(m[h], jnp.max(s_h, axis=0, keepdims=True)),
                                      m[h])
                    alpha.append(jnp.exp2(m[h] - m_new))
                    parts.append(jnp.exp2(s_h - jnp.where(ok[h], m_new, jnp.inf)))
                    m[h] = m_new
                p = jnp.concatenate(parts, axis=0).astype(BF16)
                outs, sums_p = _weighted_values(vs_s, n * blk, (n + 1) * blk, p, sum_row)
                acc = [acc[h] * alpha[h] + outs[h] for h in range(N_KV_HEADS)]
                den = [den[h] * alpha[h] + sums_p[h] for h in range(N_KV_HEADS)]
            else:
                bias = jnp.where(bias_row == 0, jnp.where(ok[0], 0.0, MASK_BIAS),
                                 jnp.where(bias_row == 1, jnp.where(ok[1], 0.0, MASK_BIAS), 0.0))
                q_ext = jnp.concatenate([q_t, bias.astype(BF16), bias_pad], axis=0)
                k_ext = jnp.concatenate([kst, head_cols], axis=1)
                p = jnp.exp2(jnp.dot(k_ext, q_ext, preferred_element_type=F32)).astype(BF16)
                outs, sums_p = _weighted_values(vs_s, n * blk, (n + 1) * blk, p, sum_row)
                acc = [acc[h] + outs[h] for h in range(N_KV_HEADS)]
                den = [den[h] + sums_p[h] for h in range(N_KV_HEADS)]

        o_t = jnp.concatenate([acc[h] * (1.0 / den[h]) for h in range(N_KV_HEADS)], axis=0)
        _store_gated(o_ref, g_ref, rows, o_t, blk)


def _moba(shifted, q, qsum, g, kv):
    b, l, _ = q.shape
    qspec = pl.BlockSpec((1, l, Q_WIDTH), lambda i: (i, 0, 0))
    return pl.pallas_call(
        functools.partial(_moba_kernel, l // MOBA_BLOCK, shifted),
        grid=(b,),
        in_specs=[qspec,
                  pl.BlockSpec((1, l, LANES), lambda i: (i, 0, 0)),
                  qspec,
                  pl.BlockSpec((1, l, 2 * KV_WIDTH), lambda i: (i, 0, 0))],
        out_specs=qspec,
        out_shape=jax.ShapeDtypeStruct((b, l, Q_WIDTH), BF16),
        scratch_shapes=_ATTN_SCRATCH(l),
        compiler_params=_cparams("parallel"),
        name="moba",
    )(q, qsum, g, kv)


def _outproj_kernel(x_ref, ys_ref, ya_ref, ym_ref, w_ref, o_ref):
    y = jnp.concatenate([ys_ref[0], ya_ref[0], ym_ref[0]], axis=1)
    o_ref[0] = x_ref[0] + jnp.dot(y, w_ref[...], preferred_element_type=F32)


def _outproj(x, y_ssm, y_swa, y_moba, w_perm, tl):
    b, l, d = x.shape
    qspec = pl.BlockSpec((1, tl, Q_WIDTH), lambda i, t: (i, t, 0))
    xspec = pl.BlockSpec((1, tl, d), lambda i, t: (i, t, 0))
    return pl.pallas_call(
        _outproj_kernel,
        grid=(b, l // tl),
        in_specs=[xspec,
                  pl.BlockSpec((1, tl, SSM_WIDTH), lambda i, t: (i, t, 0)),
                  qspec, qspec,
                  pl.BlockSpec(w_perm.shape, lambda i, t: (0, 0))],
        out_specs=xspec,
        out_shape=jax.ShapeDtypeStruct((b, l, d), F32),
        compiler_params=_cparams("parallel", "parallel"),
        name="outproj",
    )(x, y_ssm, y_swa, y_moba, w_perm)


def _head_tiles(w):
    lead = w.shape[:-1]
    w = w.reshape(*lead, N_KV_HEADS, N_Q_TILES, HEAD_DIM)
    return jnp.swapaxes(w, -3, -2).reshape(*lead, Q_WIDTH)


def _permute_w_in(w):
    s, q, k = SSM_WIDTH, Q_WIDTH, KV_WIDTH
    bounds = [0]
    for n in (s, s, q, k, k, q, q, k, k, q):
        bounds.append(bounds[-1] + n)
    s_u, s_g, a_q, a_k, a_v, a_g, m_q, m_k, m_v, m_g = [
        w[..., bounds[i]:bounds[i + 1]] for i in range(10)]
    return jnp.concatenate(
        [s_u, s_g, _head_tiles(a_q), _head_tiles(a_g), _head_tiles(m_q), _head_tiles(m_g),
         a_k, a_v, m_k, m_v], axis=-1).astype(BF16)


def _permute_w_out(w):
    s, q = SSM_WIDTH, Q_WIDTH
    rows_t = lambda part: jnp.swapaxes(_head_tiles(jnp.swapaxes(part, -1, -2)), -1, -2)
    return jnp.concatenate([w[:, 0:s], rows_t(w[:, s:s + q]), rows_t(w[:, s + q:])],
                           axis=1).astype(BF16)


def _block_diag_in(bb):
    d, g, h, p = bb.shape
    eye = jnp.eye(g, dtype=bb.dtype)[None, :, None, :, None]
    return (bb[:, :, :, None, :] * eye).reshape(d, g * h, g * p)


def _block_diag_out(c):
    d, g, h, p = c.shape
    eye = jnp.eye(g, dtype=c.dtype)[None, :, None, :, None]
    return (jnp.swapaxes(c, -1, -2)[:, :, :, None, :] * eye).reshape(d, g * p, g * h)


def _logit_bound(q_gain, k_gain):
    return (HEAD_DIM * ATTN_SCALE * LOG2E) * jnp.max(jnp.abs(q_gain)) * jnp.max(jnp.abs(k_gain))


def kernel(x, norm_g, w_in, ssm_lam_re, ssm_lam_im, ssm_log_dt, ssm_b_re, ssm_b_im,
           ssm_c_re, ssm_c_im, ssm_d, ssm_glu_w, ssm_glu_b, swa_q_norm, swa_k_norm,
           swa_sink, moba_q_norm, moba_k_norm, w_out):
    b, l, d = x.shape
    depth = norm_g.shape[0]
    tl, tc = _tiles(l)
    a_re, a_im, bb_re, bb_im = _ssm_prep(ssm_lam_re, ssm_lam_im, ssm_log_dt, ssm_b_re, ssm_b_im)
    a_re = a_re.reshape(depth, 1, N_STATE)
    a_im = a_im.reshape(depth, 1, N_STATE)
    w_in_p = _permute_w_in(w_in)
    w_out_p = _permute_w_out(w_out)
    bmat = jnp.concatenate([_block_diag_in(bb_re), _block_diag_in(bb_im)], axis=-1).astype(BF16)
    c_re = _block_diag_out(ssm_c_re).astype(BF16)
    c_im = _block_diag_out(ssm_c_im).astype(BF16)
    glu_w = ssm_glu_w.astype(BF16)
    two = lambda v: jnp.concatenate([v, v], axis=-1).reshape(depth, 1, LANES).astype(F32)
    head_norms = [two(v) for v in (swa_q_norm, swa_k_norm, moba_q_norm, moba_k_norm)]
    ssm_d = ssm_d.reshape(depth, 1, SSM_WIDTH).astype(F32)
    glu_b = ssm_glu_b.reshape(depth, 1, SSM_WIDTH).astype(F32)
    sink = swa_sink.reshape(depth, 1, N_Q_HEADS).astype(F32)

    for layer in range(depth):
        s_u, s_g, a_q, a_g, m_q, m_qsum, m_g, a_kv, m_kv = _inproj(
            x, norm_g[layer], w_in_p[layer], [v[layer] for v in head_norms], tl)
        y_ssm = _ssm(s_u, s_g, bmat[layer], a_re[layer], a_im[layer], c_re[layer], c_im[layer],
                     ssm_d[layer], glu_w[layer], glu_b[layer], tc)
        swa_small = ((_logit_bound(swa_q_norm[layer], swa_k_norm[layer]) <= LOGIT_BOUND)
                     & (jnp.max(jnp.abs(swa_sink[layer])) * LOG2E <= LOGIT_BOUND))
        y_swa = lax.cond(swa_small, functools.partial(_swa, False), functools.partial(_swa, True),
                         a_q, a_g, a_kv, sink[layer])
        moba_small = _logit_bound(moba_q_norm[layer], moba_k_norm[layer]) <= LOGIT_BOUND
        y_moba = lax.cond(moba_small, functools.partial(_moba, False), functools.partial(_moba, True),
                          m_q, m_qsum, m_g, m_kv)
        x = _outproj(x, y_ssm, y_swa, y_moba, w_out_p[layer], tl)
    return x
```

```python
import functools
import math

import jax
import jax.numpy as jnp
from jax import lax
from jax.experimental import pallas as pl
from jax.experimental.pallas import tpu as pltpu

F32 = jnp.float32
BF16 = jnp.bfloat16

HEAD_DIM = 64
SSM_WIDTH = 256
SSM_GROUPS = 16
SSM_STATE = 64
N_STATE = SSM_GROUPS * SSM_STATE
N_Q_HEADS = 6
N_KV_HEADS = 2
Q_WIDTH = N_Q_HEADS * HEAD_DIM
KV_WIDTH = N_KV_HEADS * HEAD_DIM
SWA_WINDOW = 128
MOBA_BLOCK = 256
MOBA_TOPK = 3
RMS_EPS = 1e-6
ATTN_SCALE = HEAD_DIM ** -0.5
LOG2E = math.log2(math.e)
LANES = 128
N_Q_TILES = Q_WIDTH // LANES
VMEM_LIMIT = 56 * 1024 * 1024
SUM_ROWS = 16
PREP_ROWS = 256
OFF_U, OFF_SG = 0, SSM_WIDTH
OFF_AQ = 2 * SSM_WIDTH
OFF_AG = OFF_AQ + Q_WIDTH
OFF_MQ = OFF_AG + Q_WIDTH
OFF_MG = OFF_MQ + Q_WIDTH
OFF_AKV = OFF_MG + Q_WIDTH
OFF_MKV = OFF_AKV + 2 * KV_WIDTH
PROJ_CHUNK = 512
LOGIT_BOUND = 60.0
MASK_BIAS = -300.0


def _tiles(seq):
    return min(512, seq), min(64, seq)


def _cparams(*sem):
    return pltpu.CompilerParams(dimension_semantics=sem, vmem_limit_bytes=VMEM_LIMIT)


def _silu(t):
    return t * jax.nn.sigmoid(t)


def _ssm_prep_kernel(lr_ref, li_ref, ldt_ref, br_ref, bi_ref,
                     are_ref, aim_ref, bbr_ref, bbi_ref):
    lr = lr_ref[...]
    li = li_ref[...]
    dt = jnp.exp(ldt_ref[...])
    mag = jnp.exp(lr * dt)
    a_re = mag * jnp.cos(li * dt)
    a_im = mag * jnp.sin(li * dt)
    den = lr * lr + li * li
    nr = a_re - 1.0
    ni = a_im
    cr = (nr * lr + ni * li) / den
    ci = (ni * lr - nr * li) / den
    br = br_ref[...]
    bi = bi_ref[...]
    are_ref[...] = a_re
    aim_ref[...] = a_im
    bbr_ref[...] = cr * br - ci * bi
    bbi_ref[...] = cr * bi + ci * br


def _ssm_prep(lam_re, lam_im, log_dt, b_re, b_im):
    d, g, p = lam_re.shape
    h = b_re.shape[-1]
    shp = (d * g, h, p)
    bc = lambda a: jnp.broadcast_to(a.reshape(d * g, 1, -1), shp).astype(F32)
    tr = lambda a: jnp.swapaxes(a, -1, -2).reshape(shp).astype(F32)
    outs = pl.pallas_call(
        _ssm_prep_kernel,
        out_shape=[jax.ShapeDtypeStruct(shp, F32)] * 4,
        name="ssm_prep",
    )(bc(lam_re), bc(lam_im), bc(log_dt[..., None]), tr(b_re), tr(b_im))
    a_re, a_im, bb_re, bb_im = outs
    return (a_re[:, 0, :].reshape(d, g, p), a_im[:, 0, :].reshape(d, g, p),
            bb_re.reshape(d, g, h, p), bb_im.reshape(d, g, h, p))


def _pair_rms_norm(t, gain2):
    lane = lax.broadcasted_iota(jnp.int32, t.shape, 1)
    lo = lane < HEAD_DIM
    sq = t * t
    s_lo = jnp.sum(jnp.where(lo, sq, 0.0), axis=-1, keepdims=True)
    s_hi = jnp.sum(jnp.where(lo, 0.0, sq), axis=-1, keepdims=True)
    r_lo = lax.rsqrt(s_lo * (1.0 / HEAD_DIM) + RMS_EPS)
    r_hi = lax.rsqrt(s_hi * (1.0 / HEAD_DIM) + RMS_EPS)
    return t * jnp.where(lo, r_lo, r_hi) * gain2


def _half_select(t, kv_head):
    lane = lax.broadcasted_iota(jnp.int32, t.shape, 1)
    keep = (lane < HEAD_DIM) if kv_head == 0 else (lane >= HEAD_DIM)
    return jnp.where(keep, t, 0.0)


def _project(x, g_ref, w_ref, aqn_ref, akn_ref, mqn_ref, mkn_ref,
             u_ref, sg_ref, aq_ref, ag_ref, mq_ref, mqs_ref, mg_ref, akv_ref, mkv_ref):
    ms = jnp.mean(x * x, axis=-1, keepdims=True)
    h = (x * lax.rsqrt(ms + RMS_EPS) * g_ref[...]).astype(BF16)

    chunks = {}

    def tile(col):
        c, k = divmod(col, PROJ_CHUNK)
        if c not in chunks:
            chunks[c] = jnp.dot(h, w_ref[:, c * PROJ_CHUNK:(c + 1) * PROJ_CHUNK],
                                preferred_element_type=F32)
        return chunks[c][:, k:k + LANES]

    def lanes(j):
        return slice(j * LANES, (j + 1) * LANES)

    for j in range(SSM_WIDTH // LANES):
        u_ref[0, :, lanes(j)] = tile(OFF_U + j * LANES).astype(BF16)
        sg_ref[0, :, lanes(j)] = _silu(tile(OFF_SG + j * LANES)).astype(BF16)

    qsum = jnp.zeros((x.shape[0], LANES), F32)
    for j in range(N_Q_TILES):
        aq_ref[0, :, lanes(j)] = (_pair_rms_norm(tile(OFF_AQ + j * LANES), aqn_ref[...])
                                  * (ATTN_SCALE * LOG2E)).astype(BF16)
        ag_ref[0, :, lanes(j)] = _silu(tile(OFF_AG + j * LANES)).astype(BF16)
        mqn = _pair_rms_norm(tile(OFF_MQ + j * LANES), mqn_ref[...])
        qsum = qsum + mqn
        mq_ref[0, :, lanes(j)] = (mqn * (ATTN_SCALE * LOG2E)).astype(BF16)
        mg_ref[0, :, lanes(j)] = _silu(tile(OFF_MG + j * LANES)).astype(BF16)
    mqs_ref[0] = qsum

    for off, kn_ref, kv_ref in ((OFF_AKV, akn_ref, akv_ref), (OFF_MKV, mkn_ref, mkv_ref)):
        kv_ref[0, :, lanes(0)] = _pair_rms_norm(tile(off), kn_ref[...]).astype(BF16)
        kv_ref[0, :, lanes(1)] = tile(off + KV_WIDTH).astype(BF16)


def _residual_out(x_ref, ys_ref, ya_ref, ym_ref, wo_ref):
    y = jnp.concatenate([ys_ref[0], ya_ref[0], ym_ref[0]], axis=1)
    return x_ref[0] + jnp.dot(y, wo_ref[...], preferred_element_type=F32)


def _inproj_kernel(x_ref, *refs):
    _project(x_ref[0], *refs)


def _outproj_kernel(x_ref, ys_ref, ya_ref, ym_ref, wo_ref, o_ref):
    o_ref[0] = _residual_out(x_ref, ys_ref, ya_ref, ym_ref, wo_ref)


def _outin_kernel(x_ref, ys_ref, ya_ref, ym_ref, wo_ref, *refs):
    proj_in, proj_out = refs[:6], refs[7:]
    x = _residual_out(x_ref, ys_ref, ya_ref, ym_ref, wo_ref)
    refs[6][0] = x
    _project(x, *proj_in, *proj_out)


def _layer_edge(x, branch_outs, params, out_layer, in_layer, tl):
    b, l, d = x.shape
    row = lambda width: pl.BlockSpec((1, tl, width), lambda i, t: (i, t, 0))
    shape = lambda width, dtype: jax.ShapeDtypeStruct((b, l, width), dtype)

    def layer_block(a, layer):
        return pl.BlockSpec((None,) + a.shape[1:], lambda i, t: (layer,) + (0,) * (a.ndim - 1))

    args, in_specs, out_specs, out_shape = [x], [row(d)], [], []
    if out_layer is not None:
        args += list(branch_outs) + [params["w_out"]]
        in_specs += [row(SSM_WIDTH), row(Q_WIDTH), row(Q_WIDTH),
                     layer_block(params["w_out"], out_layer)]
        out_specs.append(row(d))
        out_shape.append(shape(d, F32))
    if in_layer is not None:
        proj_params = [params["norm_g"], params["w_in"]] + params["head_norms"]
        args += proj_params
        in_specs += [layer_block(a, in_layer) for a in proj_params]
        out_specs += [row(SSM_WIDTH), row(SSM_WIDTH), row(Q_WIDTH), row(Q_WIDTH), row(Q_WIDTH),
                      row(LANES), row(Q_WIDTH), row(2 * KV_WIDTH), row(2 * KV_WIDTH)]
        out_shape += [shape(SSM_WIDTH, BF16), shape(SSM_WIDTH, BF16), shape(Q_WIDTH, BF16),
                      shape(Q_WIDTH, BF16), shape(Q_WIDTH, BF16), shape(LANES, F32),
                      shape(Q_WIDTH, BF16), shape(2 * KV_WIDTH, BF16), shape(2 * KV_WIDTH, BF16)]
    body, name = {(False, True): (_inproj_kernel, "inproj"),
                  (True, False): (_outproj_kernel, "outproj"),
                  (True, True): (_outin_kernel, "outin")}[(out_layer is not None, in_layer is not None)]
    outs = pl.pallas_call(
        body,
        grid=(b, l // tl),
        in_specs=in_specs,
        out_specs=out_specs,
        out_shape=out_shape,
        compiler_params=_cparams("parallel", "parallel"),
        name=name,
    )(*args)
    return outs


ROW_PAD = 4
N_SLABS = 2 * N_STATE // LANES
RE_SLABS = N_SLABS // 2


def _ssm_kernel(nb, tc, u_ref, sg_ref, bmat_ref, are_ref, aim_ref, cre_ref, cim_ref,
                d_ref, gw_ref, gb_ref, y_ref, upad_s, bu_s, ypad_s, st_s):
    pitch = tc + ROW_PAD

    @pl.when(pl.program_id(0) == 0)
    def _():
        st_s[...] = jnp.zeros_like(st_s)
        upad_s[...] = jnp.zeros_like(upad_s)

    for b in range(nb):
        upad_s[pl.ds(b * pitch, tc), :] = u_ref[b].astype(F32)
    u = upad_s[...]
    bu = jnp.dot(u.astype(BF16), bmat_ref[...], preferred_element_type=F32)
    for j in range(N_SLABS):
        bu_s[j] = bu[:, j * LANES:(j + 1) * LANES]

    a_re = [jnp.broadcast_to(are_ref[:, j * LANES:(j + 1) * LANES], (nb, LANES))
            for j in range(RE_SLABS)]
    a_im = [jnp.broadcast_to(aim_ref[:, j * LANES:(j + 1) * LANES], (nb, LANES))
            for j in range(RE_SLABS)]

    def step(t, carry):
        out_re, out_im = [], []
        for j in range(RE_SLABS):
            xr, xi = carry[j], carry[RE_SLABS + j]
            rows = pl.ds(t, nb, stride=pitch)
            nr = a_re[j] * xr - a_im[j] * xi + bu_s[j, rows, :]
            ni = a_re[j] * xi + a_im[j] * xr + bu_s[RE_SLABS + j, rows, :]
            bu_s[j, rows, :] = nr
            bu_s[RE_SLABS + j, rows, :] = ni
            out_re.append(nr)
            out_im.append(ni)
        return tuple(out_re + out_im)

    state = lax.fori_loop(0, tc, step, tuple(st_s[j] for j in range(N_SLABS)), unroll=2)
    for j in range(N_SLABS):
        st_s[j] = state[j]

    xs_re = jnp.concatenate([bu_s[j] for j in range(RE_SLABS)], axis=1).astype(BF16)
    xs_im = jnp.concatenate([bu_s[RE_SLABS + j] for j in range(RE_SLABS)], axis=1).astype(BF16)
    y = (jnp.dot(xs_re, cre_ref[...], preferred_element_type=F32)
         - jnp.dot(xs_im, cim_ref[...], preferred_element_type=F32))
    y = y + d_ref[...] * u
    y = jax.nn.gelu(y)
    z = jnp.dot(y.astype(BF16), gw_ref[...], preferred_element_type=F32) + gb_ref[...]
    ypad_s[...] = y * jax.nn.sigmoid(z)
    for b in range(nb):
        y_ref[b] = (ypad_s[pl.ds(b * pitch, tc), :] * sg_ref[b].astype(F32)).astype(BF16)


def _ssm(u, sg, ssm_params, layer, tc):
    nb, l, _ = u.shape
    rows = nb * (tc + ROW_PAD)
    blk_spec = pl.BlockSpec((nb, tc, SSM_WIDTH), lambda i: (0, i, 0))
    full = lambda a: pl.BlockSpec((None,) + a.shape[1:], lambda i: (layer,) + (0,) * (a.ndim - 1))
    args = ssm_params
    return pl.pallas_call(
        functools.partial(_ssm_kernel, nb, tc),
        grid=(l // tc,),
        in_specs=[blk_spec, blk_spec] + [full(a) for a in args],
        out_specs=blk_spec,
        out_shape=jax.ShapeDtypeStruct((nb, l, SSM_WIDTH), BF16),
        scratch_shapes=[pltpu.VMEM((rows, SSM_WIDTH), F32),
                        pltpu.VMEM((N_SLABS, rows, LANES), F32),
                        pltpu.VMEM((rows, SSM_WIDTH), F32),
                        pltpu.VMEM((N_SLABS, nb, LANES), F32)],
        compiler_params=_cparams("arbitrary"),
        name="ssm",
    )(u, sg, *args)


def _sum_row(n_keys):
    r = lax.broadcasted_iota(jnp.int32, (SUM_ROWS, n_keys), 0)
    return jnp.where(r == 0, 1.0, 0.0).astype(BF16)


def _prep_keys_values(kv_ref, ks_s, vs_s, seq):
    means = []
    for c in range(seq // PREP_ROWS):
        rows = slice(c * PREP_ROWS, (c + 1) * PREP_ROWS)
        kvb = kv_ref[0, rows, :].astype(F32)
        kf = kvb[:, 0:KV_WIDTH]
        means.append(jnp.mean(kf, axis=0, keepdims=True))
        ks_s[0, rows, :] = _half_select(kf, 0).astype(BF16)
        ks_s[1, rows, :] = _half_select(kf, 1).astype(BF16)
        vs_s[:, rows] = jnp.transpose(kvb[:, KV_WIDTH:2 * KV_WIDTH]).astype(BF16)
    return means


def _stacked_keys(ks_s, lo, hi):
    return jnp.concatenate([ks_s[0, lo:hi, :], ks_s[1, lo:hi, :]], axis=0)


def _weighted_values(vs_s, lo, hi, p, sum_row):
    nk = hi - lo
    outs, sums = [], []
    for h in range(N_KV_HEADS):
        v_t = jnp.concatenate([vs_s[h * HEAD_DIM:(h + 1) * HEAD_DIM, lo:hi], sum_row], axis=0)
        pv = jnp.dot(v_t, p[h * nk:(h + 1) * nk], preferred_element_type=F32)
        outs.append(pv[0:HEAD_DIM])
        sums.append(pv[HEAD_DIM:HEAD_DIM + 1])
    return outs, sums


def _queries_t(q_ref, rows):
    return jnp.concatenate(
        [jnp.transpose(q_ref[0, rows, j * LANES:(j + 1) * LANES].astype(F32))
         for j in range(N_Q_TILES)], axis=1).astype(BF16)


def _store_gated(o_ref, g_ref, rows, o_t, width):
    for j in range(N_Q_TILES):
        cols = slice(j * LANES, (j + 1) * LANES)
        o = jnp.transpose(o_t[:, j * width:(j + 1) * width]) * g_ref[0, rows, cols].astype(F32)
        o_ref[0, rows, cols] = o.astype(BF16)


_ATTN_SCRATCH = lambda seq: [pltpu.VMEM((N_KV_HEADS, seq, KV_WIDTH), BF16),
                             pltpu.VMEM((KV_WIDTH, seq), BF16)]


def _swa_kernel(nblk, layer, small_ref, sink_ref, q_ref, g_ref, kv_ref, o_ref, ks_s, vs_s):
    _prep_keys_values(kv_ref, ks_s, vs_s, nblk * SWA_WINDOW)
    refs = (sink_ref, q_ref, g_ref, o_ref, ks_s, vs_s)
    small = small_ref[layer, 0]
    pl.when(small != 0)(lambda: _swa_blocks(nblk, layer, False, *refs))
    pl.when(small == 0)(lambda: _swa_blocks(nblk, layer, True, *refs))


def _swa_blocks(nblk, layer, shifted, sink_ref, q_ref, g_ref, o_ref, ks_s, vs_s):
    w = SWA_WINDOW
    nq = N_Q_TILES * w
    lane = lax.broadcasted_iota(jnp.int32, (1, nq), 1)
    sink2 = []
    for h in range(N_KV_HEADS):
        a = [sink_ref[layer, j + N_Q_TILES * h] * LOG2E for j in range(N_Q_TILES)]
        sink2.append(jnp.where(lane < w, a[0], jnp.where(lane < 2 * w, a[1], a[2])))

    def window_mask(n_keys):
        kpos = (lax.broadcasted_iota(jnp.int32, (2 * n_keys, nq), 0) & (n_keys - 1)) - (n_keys - w)
        qpos = lax.broadcasted_iota(jnp.int32, (2 * n_keys, nq), 1) & (w - 1)
        rel = qpos - kpos
        return (rel >= 0) & (rel < w)

    masks = {w: window_mask(w), 2 * w: window_mask(2 * w)}
    sums = {w: _sum_row(w), 2 * w: _sum_row(2 * w)}

    for n in range(nblk):
        rows = slice(n * w, (n + 1) * w)
        lo = max(n - 1, 0) * w
        hi = (n + 1) * w
        nk = hi - lo
        q_t = _queries_t(q_ref, rows)
        s = jnp.where(masks[nk],
                      jnp.dot(_stacked_keys(ks_s, lo, hi), q_t, preferred_element_type=F32), -jnp.inf)
        if shifted:
            m = [jnp.maximum(jnp.max(s[h * nk:(h + 1) * nk], axis=0, keepdims=True), sink2[h])
                 for h in range(N_KV_HEADS)]
            p = jnp.concatenate([jnp.exp2(s[h * nk:(h + 1) * nk] - m[h])
                                 for h in range(N_KV_HEADS)], axis=0).astype(BF16)
            sink_p = [jnp.exp2(sink2[h] - m[h]) for h in range(N_KV_HEADS)]
        else:
            p = jnp.exp2(s).astype(BF16)
            sink_p = [jnp.exp2(sink2[h]) for h in range(N_KV_HEADS)]
        outs, sums_p = _weighted_values(vs_s, lo, hi, p, sums[nk])
        o_t = jnp.concatenate([outs[h] * (1.0 / (sums_p[h] + sink_p[h]))
                               for h in range(N_KV_HEADS)], axis=0)
        _store_gated(o_ref, g_ref, rows, o_t, w)


def _swa(q, g, kv, sink, small, layer):
    b, l, _ = q.shape
    qspec = pl.BlockSpec((1, l, Q_WIDTH), lambda i: (i, 0, 0))
    smem = pl.BlockSpec(memory_space=pltpu.SMEM)
    return pl.pallas_call(
        functools.partial(_swa_kernel, l // SWA_WINDOW, layer),
        grid=(b,),
        in_specs=[smem, smem,
                  qspec, qspec,
                  pl.BlockSpec((1, l, 2 * KV_WIDTH), lambda i: (i, 0, 0))],
        out_specs=qspec,
        out_shape=jax.ShapeDtypeStruct((b, l, Q_WIDTH), BF16),
        scratch_shapes=_ATTN_SCRATCH(l),
        compiler_params=_cparams("parallel"),
        name="swa",
    )(small, sink, q, g, kv)


def _moba_kernel(nblk, layer, small_ref, q_ref, qs_ref, g_ref, kv_ref, o_ref, ks_s, vs_s):
    assert MOBA_BLOCK == PREP_ROWS
    kmean = jnp.concatenate(_prep_keys_values(kv_ref, ks_s, vs_s, nblk * MOBA_BLOCK), axis=0)
    refs = (q_ref, qs_ref, g_ref, o_ref, ks_s, vs_s)
    small = small_ref[layer, 1]
    pl.when(small != 0)(lambda: _moba_blocks(nblk, False, kmean, *refs))
    pl.when(small == 0)(lambda: _moba_blocks(nblk, True, kmean, *refs))


def _moba_blocks(nblk, shifted, kmean, q_ref, qs_ref, g_ref, o_ref, ks_s, vs_s):
    blk = MOBA_BLOCK
    nq = N_Q_TILES * blk
    sum_row = _sum_row(blk)
    if not shifted:
        r = lax.broadcasted_iota(jnp.int32, (2 * blk, LANES), 0)
        c = lax.broadcasted_iota(jnp.int32, (2 * blk, LANES), 1)
        head_cols = jnp.where(((c == 0) & (r < blk)) | ((c == 1) & (r >= blk)), 1.0, 0.0).astype(BF16)
        bias_row = lax.broadcasted_iota(jnp.int32, (SUM_ROWS, nq), 0)
        bias_pad = jnp.zeros((LANES - SUM_ROWS, nq), BF16)

    blk_id = lax.broadcasted_iota(jnp.int32, (nblk, blk), 0)
    ki = lax.broadcasted_iota(jnp.int32, (2 * blk, nq), 0) & (blk - 1)
    qi = lax.broadcasted_iota(jnp.int32, (2 * blk, nq), 1) & (blk - 1)
    causal = ki <= qi

    for i in range(nblk):
        rows = slice(i * blk, (i + 1) * blk)
        q_t = _queries_t(q_ref, rows)
        qsum = qs_ref[0, rows, :]

        past = blk_id < i
        sel = []
        for h in range(N_KV_HEADS):
            gate = lax.dot_general(_half_select(kmean, h), _half_select(qsum, h),
                                   (((1,), (1,)), ((), ())),
                                   precision=lax.Precision.HIGHEST,
                                   preferred_element_type=F32)
            gate = jnp.where(past, gate, -jnp.inf)
            rank = jnp.zeros((nblk, blk), jnp.int32)
            for n2 in range(nblk):
                other = gate[n2:n2 + 1, :]
                ahead = (other > gate) | ((other == gate) & (n2 < blk_id))
                rank = rank + ahead.astype(jnp.int32)
            chosen = jnp.where((rank < MOBA_TOPK) & past, 1.0, 0.0)
            sel.append(jnp.concatenate([chosen] * N_Q_TILES, axis=1))

        kst = _stacked_keys(ks_s, i * blk, (i + 1) * blk)
        s = jnp.where(causal, jnp.dot(kst, q_t, preferred_element_type=F32), -jnp.inf)
        if shifted:
            m = [jnp.max(s[h * blk:(h + 1) * blk], axis=0, keepdims=True)
                 for h in range(N_KV_HEADS)]
            p = jnp.concatenate([jnp.exp2(s[h * blk:(h + 1) * blk] - m[h])
                                 for h in range(N_KV_HEADS)], axis=0).astype(BF16)
        else:
            p = jnp.exp2(s).astype(BF16)
        acc, den = _weighted_values(vs_s, i * blk, (i + 1) * blk, p, sum_row)

        for n in range(i):
            kst = _stacked_keys(ks_s, n * blk, (n + 1) * blk)
            ok = [sel[h][n:n + 1, :] > 0.5 for h in range(N_KV_HEADS)]
            if shifted:
                s = jnp.dot(kst, q_t, preferred_element_type=F32)
                alpha = []
                parts = []
                for h in range(N_KV_HEADS):
                    s_h = s[h * blk:(h + 1) * blk]
                    m_new = jnp.where(ok[h], jnp.maximum(m[h], jnp.max(s_h, axis=0, keepdims=True)),
                                      m[h])
                    alpha.append(jnp.exp2(m[h] - m_new))
                    parts.append(jnp.exp2(s_h - jnp.where(ok[h], m_new, jnp.inf)))
                    m[h] = m_new
                p = jnp.concatenate(parts, axis=0).astype(BF16)
                outs, sums_p = _weighted_values(vs_s, n * blk, (n + 1) * blk, p, sum_row)
                acc = [acc[h] * alpha[h] + outs[h] for h in range(N_KV_HEADS)]
                den = [den[h] * alpha[h] + sums_p[h] for h in range(N_KV_HEADS)]
            else:
                bias = jnp.where(bias_row == 0, jnp.where(ok[0], 0.0, MASK_BIAS),
                                 jnp.where(bias_row == 1, jnp.where(ok[1], 0.0, MASK_BIAS), 0.0))
                q_ext = jnp.concatenate([q_t, bias.astype(BF16), bias_pad], axis=0)
                k_ext = jnp.concatenate([kst, head_cols], axis=1)
                p = jnp.exp2(jnp.dot(k_ext, q_ext, preferred_element_type=F32)).astype(BF16)
                outs, sums_p = _weighted_values(vs_s, n * blk, (n + 1) * blk, p, sum_row)
                acc = [acc[h] + outs[h] for h in range(N_KV_HEADS)]
                den = [den[h] + sums_p[h] for h in range(N_KV_HEADS)]

        o_t = jnp.concatenate([acc[h] * (1.0 / den[h]) for h in range(N_KV_HEADS)], axis=0)
        _store_gated(o_ref, g_ref, rows, o_t, blk)


def _moba(q, qsum, g, kv, small, layer):
    b, l, _ = q.shape
    qspec = pl.BlockSpec((1, l, Q_WIDTH), lambda i: (i, 0, 0))
    return pl.pallas_call(
        functools.partial(_moba_kernel, l // MOBA_BLOCK, layer),
        grid=(b,),
        in_specs=[pl.BlockSpec(memory_space=pltpu.SMEM),
                  qspec,
                  pl.BlockSpec((1, l, LANES), lambda i: (i, 0, 0)),
                  qspec,
                  pl.BlockSpec((1, l, 2 * KV_WIDTH), lambda i: (i, 0, 0))],
        out_specs=qspec,
        out_shape=jax.ShapeDtypeStruct((b, l, Q_WIDTH), BF16),
        scratch_shapes=_ATTN_SCRATCH(l),
        compiler_params=_cparams("parallel"),
        name="moba",
    )(small, q, qsum, g, kv)


def _head_tiles(w):
    lead = w.shape[:-1]
    w = w.reshape(*lead, N_KV_HEADS, N_Q_TILES, HEAD_DIM)
    return jnp.swapaxes(w, -3, -2).reshape(*lead, Q_WIDTH)


def _permute_w_in(w):
    s, q, k = SSM_WIDTH, Q_WIDTH, KV_WIDTH
    bounds = [0]
    for n in (s, s, q, k, k, q, q, k, k, q):
        bounds.append(bounds[-1] + n)
    s_u, s_g, a_q, a_k, a_v, a_g, m_q, m_k, m_v, m_g = [
        w[..., bounds[i]:bounds[i + 1]] for i in range(10)]
    return jnp.concatenate(
        [s_u, s_g, _head_tiles(a_q), _head_tiles(a_g), _head_tiles(m_q), _head_tiles(m_g),
         a_k, a_v, m_k, m_v], axis=-1).astype(BF16)


def _permute_w_out(w):
    s, q = SSM_WIDTH, Q_WIDTH
    rows_t = lambda part: jnp.swapaxes(_head_tiles(jnp.swapaxes(part, -1, -2)), -1, -2)
    return jnp.concatenate([w[:, 0:s], rows_t(w[:, s:s + q]), rows_t(w[:, s + q:])],
                           axis=1).astype(BF16)


def _block_diag_in(bb):
    d, g, h, p = bb.shape
    eye = jnp.eye(g, dtype=bb.dtype)[None, :, None, :, None]
    return (bb[:, :, :, None, :] * eye).reshape(d, g * h, g * p)


def _block_diag_out(c):
    d, g, h, p = c.shape
    eye = jnp.eye(g, dtype=c.dtype)[None, :, None, :, None]
    return (jnp.swapaxes(c, -1, -2)[:, :, :, None, :] * eye).reshape(d, g * p, g * h)


def _logit_bound(q_gain, k_gain):
    return ((HEAD_DIM * ATTN_SCALE * LOG2E) * jnp.max(jnp.abs(q_gain), axis=-1)
            * jnp.max(jnp.abs(k_gain), axis=-1))


def kernel(x, norm_g, w_in, ssm_lam_re, ssm_lam_im, ssm_log_dt, ssm_b_re, ssm_b_im,
           ssm_c_re, ssm_c_im, ssm_d, ssm_glu_w, ssm_glu_b, swa_q_norm, swa_k_norm,
           swa_sink, moba_q_norm, moba_k_norm, w_out):
    b, l, d = x.shape
    depth = norm_g.shape[0]
    tl, tc = _tiles(l)
    a_re, a_im, bb_re, bb_im = _ssm_prep(ssm_lam_re, ssm_lam_im, ssm_log_dt, ssm_b_re, ssm_b_im)
    two = lambda v: jnp.concatenate([v, v], axis=-1).reshape(depth, 1, LANES).astype(F32)
    edge_params = {
        "norm_g": norm_g.reshape(depth, 1, d).astype(F32),
        "w_in": _permute_w_in(w_in),
        "w_out": _permute_w_out(w_out),
        "head_norms": [two(v) for v in (swa_q_norm, swa_k_norm, moba_q_norm, moba_k_norm)],
    }
    ssm_params = (
        jnp.concatenate([_block_diag_in(bb_re), _block_diag_in(bb_im)], axis=-1).astype(BF16),
        a_re.reshape(depth, 1, N_STATE), a_im.reshape(depth, 1, N_STATE),
        _block_diag_out(ssm_c_re).astype(BF16), _block_diag_out(ssm_c_im).astype(BF16),
        ssm_d.reshape(depth, 1, SSM_WIDTH).astype(F32), ssm_glu_w.astype(BF16),
        ssm_glu_b.reshape(depth, 1, SSM_WIDTH).astype(F32))
    sink = swa_sink.astype(F32)
    small = jnp.stack(
        [(_logit_bound(swa_q_norm, swa_k_norm) <= LOGIT_BOUND)
         & (jnp.max(jnp.abs(sink), axis=-1) * LOG2E <= LOGIT_BOUND),
         _logit_bound(moba_q_norm, moba_k_norm) <= LOGIT_BOUND], axis=-1).astype(jnp.int32)

    proj = _layer_edge(x, None, edge_params, None, 0, tl)
    for layer in range(depth):
        s_u, s_g, a_q, a_g, m_q, m_qsum, m_g, a_kv, m_kv = proj
        branch_outs = (_ssm(s_u, s_g, ssm_params, layer, tc),
                       _swa(a_q, a_g, a_kv, sink, small, layer),
                       _moba(m_q, m_qsum, m_g, m_kv, small, layer))
        nxt = layer + 1 if layer + 1 < depth else None
        x, *proj = _layer_edge(x, branch_outs, edge_params, layer, nxt, tl)
    return x
```

```python
import functools
import math

import jax
import jax.numpy as jnp
from jax import lax
from jax.experimental import pallas as pl
from jax.experimental.pallas import tpu as pltpu

F32 = jnp.float32
BF16 = jnp.bfloat16

HEAD_DIM = 64
SSM_WIDTH = 256
SSM_GROUPS = 16
SSM_STATE = 64
N_STATE = SSM_GROUPS * SSM_STATE
N_Q_HEADS = 6
N_KV_HEADS = 2
Q_WIDTH = N_Q_HEADS * HEAD_DIM
KV_WIDTH = N_KV_HEADS * HEAD_DIM
SWA_WINDOW = 128
MOBA_BLOCK = 256
MOBA_TOPK = 3
RMS_EPS = 1e-6
ATTN_SCALE = HEAD_DIM ** -0.5
LOG2E = math.log2(math.e)
LANES = 128
N_Q_TILES = Q_WIDTH // LANES
VMEM_LIMIT = 56 * 1024 * 1024
SUM_ROWS = 16
PREP_ROWS = 256
OFF_U, OFF_SG = 0, SSM_WIDTH
OFF_AQ = 2 * SSM_WIDTH
OFF_AG = OFF_AQ + Q_WIDTH
OFF_MQ = OFF_AG + Q_WIDTH
OFF_MG = OFF_MQ + Q_WIDTH
OFF_AKV = OFF_MG + Q_WIDTH
OFF_MKV = OFF_AKV + 2 * KV_WIDTH
PROJ_CHUNK = 512
LOGIT_BOUND = 60.0
MASK_BIAS = -300.0


def _tiles(seq):
    return min(512, seq), min(64, seq)


def _cparams(*sem):
    return pltpu.CompilerParams(dimension_semantics=sem, vmem_limit_bytes=VMEM_LIMIT)


def _silu(t):
    return t * jax.nn.sigmoid(t)


def _ssm_prep_kernel(lr_ref, li_ref, ldt_ref, br_ref, bi_ref,
                     are_ref, aim_ref, bbr_ref, bbi_ref):
    lr = lr_ref[...]
    li = li_ref[...]
    dt = jnp.exp(ldt_ref[...])
    mag = jnp.exp(lr * dt)
    a_re = mag * jnp.cos(li * dt)
    a_im = mag * jnp.sin(li * dt)
    den = lr * lr + li * li
    nr = a_re - 1.0
    ni = a_im
    cr = (nr * lr + ni * li) / den
    ci = (ni * lr - nr * li) / den
    br = br_ref[...]
    bi = bi_ref[...]
    are_ref[...] = a_re
    aim_ref[...] = a_im
    bbr_ref[...] = cr * br - ci * bi
    bbi_ref[...] = cr * bi + ci * br


def _ssm_prep(lam_re, lam_im, log_dt, b_re, b_im):
    d, g, p = lam_re.shape
    h = b_re.shape[-1]
    shp = (d * g, h, p)
    bc = lambda a: jnp.broadcast_to(a.reshape(d * g, 1, -1), shp).astype(F32)
    tr = lambda a: jnp.swapaxes(a, -1, -2).reshape(shp).astype(F32)
    outs = pl.pallas_call(
        _ssm_prep_kernel,
        out_shape=[jax.ShapeDtypeStruct(shp, F32)] * 4,
        name="ssm_prep",
    )(bc(lam_re), bc(lam_im), bc(log_dt[..., None]), tr(b_re), tr(b_im))
    a_re, a_im, bb_re, bb_im = outs
    return (a_re[:, 0, :].reshape(d, g, p), a_im[:, 0, :].reshape(d, g, p),
            bb_re.reshape(d, g, h, p), bb_im.reshape(d, g, h, p))


def _pair_rms_norm(t, gain2):
    lane = lax.broadcasted_iota(jnp.int32, t.shape, 1)
    lo = lane < HEAD_DIM
    sq = t * t
    s_lo = jnp.sum(jnp.where(lo, sq, 0.0), axis=-1, keepdims=True)
    s_hi = jnp.sum(jnp.where(lo, 0.0, sq), axis=-1, keepdims=True)
    r_lo = lax.rsqrt(s_lo * (1.0 / HEAD_DIM) + RMS_EPS)
    r_hi = lax.rsqrt(s_hi * (1.0 / HEAD_DIM) + RMS_EPS)
    return t * jnp.where(lo, r_lo, r_hi) * gain2


def _half_select(t, kv_head):
    lane = lax.broadcasted_iota(jnp.int32, t.shape, 1)
    keep = (lane < HEAD_DIM) if kv_head == 0 else (lane >= HEAD_DIM)
    return jnp.where(keep, t, 0.0)


def _project(x, g_ref, w_ref, aqn_ref, akn_ref, mqn_ref, mkn_ref,
             u_ref, sg_ref, aq_ref, ag_ref, mq_ref, mqs_ref, mg_ref, akv_ref, mkv_ref):
    ms = jnp.mean(x * x, axis=-1, keepdims=True)
    h = (x * lax.rsqrt(ms + RMS_EPS) * g_ref[...]).astype(BF16)

    chunks = {}

    def tile(col):
        c, k = divmod(col, PROJ_CHUNK)
        if c not in chunks:
            chunks[c] = jnp.dot(h, w_ref[:, c * PROJ_CHUNK:(c + 1) * PROJ_CHUNK],
                                preferred_element_type=F32)
        return chunks[c][:, k:k + LANES]

    def lanes(j):
        return slice(j * LANES, (j + 1) * LANES)

    for j in range(SSM_WIDTH // LANES):
        u_ref[0, :, lanes(j)] = tile(OFF_U + j * LANES).astype(BF16)
        sg_ref[0, :, lanes(j)] = _silu(tile(OFF_SG + j * LANES)).astype(BF16)

    qsum = jnp.zeros((x.shape[0], LANES), F32)
    for j in range(N_Q_TILES):
        aq_ref[0, :, lanes(j)] = (_pair_rms_norm(tile(OFF_AQ + j * LANES), aqn_ref[...])
                                  * (ATTN_SCALE * LOG2E)).astype(BF16)
        ag_ref[0, :, lanes(j)] = _silu(tile(OFF_AG + j * LANES)).astype(BF16)
        mqn = _pair_rms_norm(tile(OFF_MQ + j * LANES), mqn_ref[...])
        qsum = qsum + mqn
        mq_ref[0, :, lanes(j)] = (mqn * (ATTN_SCALE * LOG2E)).astype(BF16)
        mg_ref[0, :, lanes(j)] = _silu(tile(OFF_MG + j * LANES)).astype(BF16)
    mqs_ref[0] = qsum

    for off, kn_ref, kv_ref in ((OFF_AKV, akn_ref, akv_ref), (OFF_MKV, mkn_ref, mkv_ref)):
        kv_ref[0, :, lanes(0)] = _pair_rms_norm(tile(off), kn_ref[...]).astype(BF16)
        kv_ref[0, :, lanes(1)] = tile(off + KV_WIDTH).astype(BF16)


def _residual_out(x_ref, ys_ref, ya_ref, ym_ref, wo_ref):
    y = jnp.concatenate([ys_ref[0], ya_ref[0], ym_ref[0]], axis=1)
    return x_ref[0] + jnp.dot(y, wo_ref[...], preferred_element_type=F32)


def _inproj_kernel(x_ref, *refs):
    _project(x_ref[0], *refs)


def _outproj_kernel(x_ref, ys_ref, ya_ref, ym_ref, wo_ref, o_ref):
    o_ref[0] = _residual_out(x_ref, ys_ref, ya_ref, ym_ref, wo_ref)


def _outin_kernel(x_ref, ys_ref, ya_ref, ym_ref, wo_ref, *refs):
    proj_in, proj_out = refs[:6], refs[7:]
    x = _residual_out(x_ref, ys_ref, ya_ref, ym_ref, wo_ref)
    refs[6][0] = x
    _project(x, *proj_in, *proj_out)


def _layer_edge(x, branch_outs, params, out_layer, in_layer, tl):
    b, l, d = x.shape
    row = lambda width: pl.BlockSpec((1, tl, width), lambda i, t: (i, t, 0))
    shape = lambda width, dtype: jax.ShapeDtypeStruct((b, l, width), dtype)

    def layer_block(a, layer):
        return pl.BlockSpec((None,) + a.shape[1:], lambda i, t: (layer,) + (0,) * (a.ndim - 1))

    args, in_specs, out_specs, out_shape = [x], [row(d)], [], []
    if out_layer is not None:
        args += list(branch_outs) + [params["w_out"]]
        in_specs += [row(SSM_WIDTH), row(Q_WIDTH), row(Q_WIDTH),
                     layer_block(params["w_out"], out_layer)]
        out_specs.append(row(d))
        out_shape.append(shape(d, F32))
    if in_layer is not None:
        proj_params = [params["norm_g"], params["w_in"]] + params["head_norms"]
        args += proj_params
        in_specs += [layer_block(a, in_layer) for a in proj_params]
        out_specs += [row(SSM_WIDTH), row(SSM_WIDTH), row(Q_WIDTH), row(Q_WIDTH), row(Q_WIDTH),
                      row(LANES), row(Q_WIDTH), row(2 * KV_WIDTH), row(2 * KV_WIDTH)]
        out_shape += [shape(SSM_WIDTH, BF16), shape(SSM_WIDTH, BF16), shape(Q_WIDTH, BF16),
                      shape(Q_WIDTH, BF16), shape(Q_WIDTH, BF16), shape(LANES, F32),
                      shape(Q_WIDTH, BF16), shape(2 * KV_WIDTH, BF16), shape(2 * KV_WIDTH, BF16)]
    body, name = {(False, True): (_inproj_kernel, "inproj"),
                  (True, False): (_outproj_kernel, "outproj"),
                  (True, True): (_outin_kernel, "outin")}[(out_layer is not None, in_layer is not None)]
    outs = pl.pallas_call(
        body,
        grid=(b, l // tl),
        in_specs=in_specs,
        out_specs=out_specs,
        out_shape=out_shape,
        compiler_params=_cparams("parallel", "parallel"),
        name=name,
    )(*args)
    return outs


ROW_PAD = 4
N_SLABS = 2 * N_STATE // LANES
RE_SLABS = N_SLABS // 2
N_STAGES = 3


def _ssm_kernel(nb, tc, u_ref, u2_ref, sg2_ref, bmat_ref, are_ref, aim_ref, cmat_ref,
                d_ref, gw_ref, gb_ref, y_ref, upad_a, upad_c, ypad_s, st_s, *slots):
    pitch = tc + ROW_PAD
    i = pl.program_id(0)

    @pl.when(i == 0)
    def _():
        st_s[...] = jnp.zeros_like(st_s)
        upad_a[...] = jnp.zeros_like(upad_a)
        upad_c[...] = jnp.zeros_like(upad_c)
        for s in slots:
            s[...] = jnp.zeros_like(s)

    pieces = RE_SLABS
    steps = tc // pieces

    def stages(in_s, scan_s, out_s):
        for b in range(nb):
            upad_a[pl.ds(b * pitch, tc), :] = u_ref[b].astype(F32)
            upad_c[pl.ds(b * pitch, tc), :] = u2_ref[b].astype(F32)
        u_in = upad_a[...].astype(BF16)
        a_re = [jnp.broadcast_to(are_ref[:, j * LANES:(j + 1) * LANES], (nb, LANES))
                for j in range(RE_SLABS)]
        a_im = [jnp.broadcast_to(aim_ref[:, j * LANES:(j + 1) * LANES], (nb, LANES))
                for j in range(RE_SLABS)]
        state = [st_s[j] for j in range(N_SLABS)]
        y = d_ref[...] * upad_c[...]
        for k in range(pieces):
            bu = jnp.dot(u_in, bmat_ref[k], preferred_element_type=F32)
            in_s[k] = bu[:, 0:LANES]
            in_s[RE_SLABS + k] = bu[:, LANES:2 * LANES]
            for t in range(k * steps, (k + 1) * steps):
                rows = pl.ds(t, nb, stride=pitch)
                for j in range(RE_SLABS):
                    xr, xi = state[j], state[RE_SLABS + j]
                    nr = a_re[j] * xr - a_im[j] * xi + scan_s[j, rows, :]
                    ni = a_re[j] * xi + a_im[j] * xr + scan_s[RE_SLABS + j, rows, :]
                    scan_s[j, rows, :] = nr
                    scan_s[RE_SLABS + j, rows, :] = ni
                    state[j], state[RE_SLABS + j] = nr, ni
            xs = jnp.concatenate([out_s[k], out_s[RE_SLABS + k]], axis=1).astype(BF16)
            y = y + jnp.dot(xs, cmat_ref[k], preferred_element_type=F32)
        for j in range(N_SLABS):
            st_s[j] = state[j]
        y = jax.nn.gelu(y)
        z = jnp.dot(y.astype(BF16), gw_ref[...], preferred_element_type=F32) + gb_ref[...]
        ypad_s[...] = y * jax.nn.sigmoid(z)
        for b in range(nb):
            y_ref[b] = (ypad_s[pl.ds(b * pitch, tc), :] * sg2_ref[b].astype(F32)).astype(BF16)

    for r in range(N_STAGES):
        @pl.when(lax.rem(i, N_STAGES) == r)
        def _(r=r):
            stages(slots[r], slots[(r + 2) % N_STAGES], slots[(r + 1) % N_STAGES])


def _ssm(u, sg, ssm_params, layer, tc):
    nb, l, _ = u.shape
    rows = nb * (tc + ROW_PAD)
    n_chunks = l // tc
    chunk = lambda lag: pl.BlockSpec(
        (nb, tc, SSM_WIDTH), lambda i: (0, jnp.clip(i - lag, 0, n_chunks - 1), 0))
    full = lambda a: pl.BlockSpec((None,) + a.shape[1:], lambda i: (layer,) + (0,) * (a.ndim - 1))
    slab = pltpu.VMEM((N_SLABS, rows, LANES), F32)
    return pl.pallas_call(
        functools.partial(_ssm_kernel, nb, tc),
        grid=(n_chunks + N_STAGES - 1,),
        in_specs=[chunk(0), chunk(2), chunk(2)] + [full(a) for a in ssm_params],
        out_specs=chunk(2),
        out_shape=jax.ShapeDtypeStruct((nb, l, SSM_WIDTH), BF16),
        scratch_shapes=[pltpu.VMEM((rows, SSM_WIDTH), F32),
                        pltpu.VMEM((rows, SSM_WIDTH), F32),
                        pltpu.VMEM((rows, SSM_WIDTH), F32),
                        pltpu.VMEM((N_SLABS, nb, LANES), F32),
                        slab, slab, slab],
        compiler_params=_cparams("arbitrary"),
        name="ssm",
    )(u, u, sg, *ssm_params)


def _sum_row(n_keys):
    r = lax.broadcasted_iota(jnp.int32, (SUM_ROWS, n_keys), 0)
    return jnp.where(r == 0, 1.0, 0.0).astype(BF16)


def _prep_keys_values(kv_ref, ks_s, vs_s, seq):
    means = []
    for c in range(seq // PREP_ROWS):
        rows = slice(c * PREP_ROWS, (c + 1) * PREP_ROWS)
        kvb = kv_ref[0, rows, :].astype(F32)
        kf = kvb[:, 0:KV_WIDTH]
        means.append(jnp.mean(kf, axis=0, keepdims=True))
        ks_s[0, rows, :] = _half_select(kf, 0).astype(BF16)
        ks_s[1, rows, :] = _half_select(kf, 1).astype(BF16)
        vs_s[:, rows] = jnp.transpose(kvb[:, KV_WIDTH:2 * KV_WIDTH]).astype(BF16)
    return means


def _stacked_keys(ks_s, lo, hi):
    return jnp.concatenate([ks_s[0, lo:hi, :], ks_s[1, lo:hi, :]], axis=0)


def _weighted_values(vs_s, lo, hi, p, sum_row):
    nk = hi - lo
    outs, sums = [], []
    for h in range(N_KV_HEADS):
        v_t = jnp.concatenate([vs_s[h * HEAD_DIM:(h + 1) * HEAD_DIM, lo:hi], sum_row], axis=0)
        pv = jnp.dot(v_t, p[h * nk:(h + 1) * nk], preferred_element_type=F32)
        outs.append(pv[0:HEAD_DIM])
        sums.append(pv[HEAD_DIM:HEAD_DIM + 1])
    return outs, sums


def _queries_t(q_ref, rows):
    return jnp.concatenate(
        [jnp.transpose(q_ref[0, rows, j * LANES:(j + 1) * LANES].astype(F32))
         for j in range(N_Q_TILES)], axis=1).astype(BF16)


def _store_gated(o_ref, g_ref, rows, o_t, width):
    for j in range(N_Q_TILES):
        cols = slice(j * LANES, (j + 1) * LANES)
        o = jnp.transpose(o_t[:, j * width:(j + 1) * width]) * g_ref[0, rows, cols].astype(F32)
        o_ref[0, rows, cols] = o.astype(BF16)


_ATTN_SCRATCH = lambda seq: [pltpu.VMEM((N_KV_HEADS, seq, KV_WIDTH), BF16),
                             pltpu.VMEM((KV_WIDTH, seq), BF16)]


def _swa_kernel(nblk, layer, shifted, sink_ref, q_ref, g_ref, kv_ref, o_ref, ks_s, vs_s):
    w = SWA_WINDOW
    nq = N_Q_TILES * w
    _prep_keys_values(kv_ref, ks_s, vs_s, nblk * w)
    lane = lax.broadcasted_iota(jnp.int32, (1, nq), 1)
    sink2 = []
    for h in range(N_KV_HEADS):
        a = [sink_ref[layer, j + N_Q_TILES * h] * LOG2E for j in range(N_Q_TILES)]
        sink2.append(jnp.where(lane < w, a[0], jnp.where(lane < 2 * w, a[1], a[2])))

    def window_mask(n_keys):
        kpos = (lax.broadcasted_iota(jnp.int32, (2 * n_keys, nq), 0) & (n_keys - 1)) - (n_keys - w)
        qpos = lax.broadcasted_iota(jnp.int32, (2 * n_keys, nq), 1) & (w - 1)
        rel = qpos - kpos
        return (rel >= 0) & (rel < w)

    masks = {w: window_mask(w), 2 * w: window_mask(2 * w)}
    sums = {w: _sum_row(w), 2 * w: _sum_row(2 * w)}

    for n in range(nblk):
        rows = slice(n * w, (n + 1) * w)
        lo = max(n - 1, 0) * w
        hi = (n + 1) * w
        nk = hi - lo
        q_t = _queries_t(q_ref, rows)
        s = jnp.where(masks[nk],
                      jnp.dot(_stacked_keys(ks_s, lo, hi), q_t, preferred_element_type=F32), -jnp.inf)
        if shifted:
            m = [jnp.maximum(jnp.max(s[h * nk:(h + 1) * nk], axis=0, keepdims=True), sink2[h])
                 for h in range(N_KV_HEADS)]
            p = jnp.concatenate([jnp.exp2(s[h * nk:(h + 1) * nk] - m[h])
                                 for h in range(N_KV_HEADS)], axis=0).astype(BF16)
            sink_p = [jnp.exp2(sink2[h] - m[h]) for h in range(N_KV_HEADS)]
        else:
            p = jnp.exp2(s).astype(BF16)
            sink_p = [jnp.exp2(sink2[h]) for h in range(N_KV_HEADS)]
        outs, sums_p = _weighted_values(vs_s, lo, hi, p, sums[nk])
        o_t = jnp.concatenate([outs[h] * (1.0 / (sums_p[h] + sink_p[h]))
                               for h in range(N_KV_HEADS)], axis=0)
        _store_gated(o_ref, g_ref, rows, o_t, w)


def _swa(q, g, kv, sink, layer, shifted):
    b, l, _ = q.shape
    qspec = pl.BlockSpec((1, l, Q_WIDTH), lambda i: (i, 0, 0))
    return pl.pallas_call(
        functools.partial(_swa_kernel, l // SWA_WINDOW, layer, shifted),
        grid=(b,),
        in_specs=[pl.BlockSpec(memory_space=pltpu.SMEM),
                  qspec, qspec,
                  pl.BlockSpec((1, l, 2 * KV_WIDTH), lambda i: (i, 0, 0))],
        out_specs=qspec,
        out_shape=jax.ShapeDtypeStruct((b, l, Q_WIDTH), BF16),
        scratch_shapes=_ATTN_SCRATCH(l),
        compiler_params=_cparams("parallel"),
        name="swa",
    )(sink, q, g, kv)


def _moba_kernel(nblk, shifted, q_ref, qs_ref, g_ref, kv_ref, o_ref, ks_s, vs_s):
    blk = MOBA_BLOCK
    assert blk == PREP_ROWS
    nq = N_Q_TILES * blk
    kmean = jnp.concatenate(_prep_keys_values(kv_ref, ks_s, vs_s, nblk * blk), axis=0)
    sum_row = _sum_row(blk)
    if not shifted:
        r = lax.broadcasted_iota(jnp.int32, (2 * blk, LANES), 0)
        c = lax.broadcasted_iota(jnp.int32, (2 * blk, LANES), 1)
        head_cols = jnp.where(((c == 0) & (r < blk)) | ((c == 1) & (r >= blk)), 1.0, 0.0).astype(BF16)
        bias_row = lax.broadcasted_iota(jnp.int32, (SUM_ROWS, nq), 0)
        bias_pad = jnp.zeros((LANES - SUM_ROWS, nq), BF16)

    blk_id = lax.broadcasted_iota(jnp.int32, (nblk, blk), 0)
    ki = lax.broadcasted_iota(jnp.int32, (2 * blk, nq), 0) & (blk - 1)
    qi = lax.broadcasted_iota(jnp.int32, (2 * blk, nq), 1) & (blk - 1)
    causal = ki <= qi

    for i in range(nblk):
        rows = slice(i * blk, (i + 1) * blk)
        q_t = _queries_t(q_ref, rows)
        qsum = qs_ref[0, rows, :]

        past = blk_id < i
        sel = []
        for h in range(N_KV_HEADS):
            gate = lax.dot_general(_half_select(kmean, h), _half_select(qsum, h),
                                   (((1,), (1,)), ((), ())),
                                   precision=lax.Precision.HIGHEST,
                                   preferred_element_type=F32)
            gate = jnp.where(past, gate, -jnp.inf)
            rank = jnp.zeros((nblk, blk), jnp.int32)
            for n2 in range(nblk):
                other = gate[n2:n2 + 1, :]
                ahead = (other > gate) | ((other == gate) & (n2 < blk_id))
                rank = rank + ahead.astype(jnp.int32)
            chosen = jnp.where((rank < MOBA_TOPK) & past, 1.0, 0.0)
            sel.append(jnp.concatenate([chosen] * N_Q_TILES, axis=1))

        kst = _stacked_keys(ks_s, i * blk, (i + 1) * blk)
        s = jnp.where(causal, jnp.dot(kst, q_t, preferred_element_type=F32), -jnp.inf)
        if shifted:
            m = [jnp.max(s[h * blk:(h + 1) * blk], axis=0, keepdims=True)
                 for h in range(N_KV_HEADS)]
            p = jnp.concatenate([jnp.exp2(s[h * blk:(h + 1) * blk] - m[h])
                                 for h in range(N_KV_HEADS)], axis=0).astype(BF16)
        else:
            p = jnp.exp2(s).astype(BF16)
        acc, den = _weighted_values(vs_s, i * blk, (i + 1) * blk, p, sum_row)

        for n in range(i):
            kst = _stacked_keys(ks_s, n * blk, (n + 1) * blk)
            ok = [sel[h][n:n + 1, :] > 0.5 for h in range(N_KV_HEADS)]
            if shifted:
                s = jnp.dot(kst, q_t, preferred_element_type=F32)
                alpha = []
                parts = []
                for h in range(N_KV_HEADS):
                    s_h = s[h * blk:(h + 1) * blk]
                    m_new = jnp.where(ok[h], jnp.maximum(m[h], jnp.max(s_h, axis=0, keepdims=True)),
                                      m[h])
                    alpha.append(jnp.exp2(m[h] - m_new))
                    parts.append(jnp.exp2(s_h - jnp.where(ok[h], m_new, jnp.inf)))
                    m[h] = m_new
                p = jnp.concatenate(parts, axis=0).astype(BF16)
                outs, sums_p = _weighted_values(vs_s, n * blk, (n + 1) * blk, p, sum_row)
                acc = [acc[h] * alpha[h] + outs[h] for h in range(N_KV_HEADS)]
                den = [den[h] * alpha[h] + sums_p[h] for h in range(N_KV_HEADS)]
            else:
                bias = jnp.where(bias_row == 0, jnp.where(ok[0], 0.0, MASK_BIAS),
                                 jnp.where(bias_row == 1, jnp.where(ok[1], 0.0, MASK_BIAS), 0.0))
                q_ext = jnp.concatenate([q_t, bias.astype(BF16), bias_pad], axis=0)
                k_ext = jnp.concatenate([kst, head_cols], axis=1)
                p = jnp.exp2(jnp.dot(k_ext, q_ext, preferred_element_type=F32)).astype(BF16)
                outs, sums_p = _weighted_values(vs_s, n * blk, (n + 1) * blk, p, sum_row)
                acc = [acc[h] + outs[h] for h in range(N_KV_HEADS)]
                den = [den[h] + sums_p[h] for h in range(N_KV_HEADS)]

        o_t = jnp.concatenate([acc[h] * (1.0 / den[h]) for h in range(N_KV_HEADS)], axis=0)
        _store_gated(o_ref, g_ref, rows, o_t, blk)


def _moba(q, qsum, g, kv, shifted):
    b, l, _ = q.shape
    qspec = pl.BlockSpec((1, l, Q_WIDTH), lambda i: (i, 0, 0))
    return pl.pallas_call(
        functools.partial(_moba_kernel, l // MOBA_BLOCK, shifted),
        grid=(b,),
        in_specs=[qspec,
                  pl.BlockSpec((1, l, LANES), lambda i: (i, 0, 0)),
                  qspec,
                  pl.BlockSpec((1, l, 2 * KV_WIDTH), lambda i: (i, 0, 0))],
        out_specs=qspec,
        out_shape=jax.ShapeDtypeStruct((b, l, Q_WIDTH), BF16),
        scratch_shapes=_ATTN_SCRATCH(l),
        compiler_params=_cparams("parallel"),
        name="moba",
    )(q, qsum, g, kv)


def _head_tiles(w):
    lead = w.shape[:-1]
    w = w.reshape(*lead, N_KV_HEADS, N_Q_TILES, HEAD_DIM)
    return jnp.swapaxes(w, -3, -2).reshape(*lead, Q_WIDTH)


def _permute_w_in(w):
    s, q, k = SSM_WIDTH, Q_WIDTH, KV_WIDTH
    bounds = [0]
    for n in (s, s, q, k, k, q, q, k, k, q):
        bounds.append(bounds[-1] + n)
    s_u, s_g, a_q, a_k, a_v, a_g, m_q, m_k, m_v, m_g = [
        w[..., bounds[i]:bounds[i + 1]] for i in range(10)]
    return jnp.concatenate(
        [s_u, s_g, _head_tiles(a_q), _head_tiles(a_g), _head_tiles(m_q), _head_tiles(m_g),
         a_k, a_v, m_k, m_v], axis=-1).astype(BF16)


def _permute_w_out(w):
    s, q = SSM_WIDTH, Q_WIDTH
    rows_t = lambda part: jnp.swapaxes(_head_tiles(jnp.swapaxes(part, -1, -2)), -1, -2)
    return jnp.concatenate([w[:, 0:s], rows_t(w[:, s:s + q]), rows_t(w[:, s + q:])],
                           axis=1).astype(BF16)


def _block_diag_in(bb):
    d, g, h, p = bb.shape
    eye = jnp.eye(g, dtype=bb.dtype)[None, :, None, :, None]
    return (bb[:, :, :, None, :] * eye).reshape(d, g * h, g * p)


def _block_diag_out(c):
    d, g, h, p = c.shape
    eye = jnp.eye(g, dtype=c.dtype)[None, :, None, :, None]
    return (jnp.swapaxes(c, -1, -2)[:, :, :, None, :] * eye).reshape(d, g * p, g * h)


def _logit_bound(q_gain, k_gain):
    return ((HEAD_DIM * ATTN_SCALE * LOG2E) * jnp.max(jnp.abs(q_gain), axis=-1)
            * jnp.max(jnp.abs(k_gain), axis=-1))


def kernel(x, norm_g, w_in, ssm_lam_re, ssm_lam_im, ssm_log_dt, ssm_b_re, ssm_b_im,
           ssm_c_re, ssm_c_im, ssm_d, ssm_glu_w, ssm_glu_b, swa_q_norm, swa_k_norm,
           swa_sink, moba_q_norm, moba_k_norm, w_out):
    b, l, d = x.shape
    depth = norm_g.shape[0]
    tl, tc = _tiles(l)
    a_re, a_im, bb_re, bb_im = _ssm_prep(ssm_lam_re, ssm_lam_im, ssm_log_dt, ssm_b_re, ssm_b_im)
    two = lambda v: jnp.concatenate([v, v], axis=-1).reshape(depth, 1, LANES).astype(F32)
    edge_params = {
        "norm_g": norm_g.reshape(depth, 1, d).astype(F32),
        "w_in": _permute_w_in(w_in),
        "w_out": _permute_w_out(w_out),
        "head_norms": [two(v) for v in (swa_q_norm, swa_k_norm, moba_q_norm, moba_k_norm)],
    }
    by_slab_cols = lambda m: m.reshape(depth, SSM_WIDTH, RE_SLABS, LANES)
    bmat = jnp.concatenate([by_slab_cols(_block_diag_in(bb_re)), by_slab_cols(_block_diag_in(bb_im))],
                           axis=-1).transpose(0, 2, 1, 3).astype(BF16)
    by_slab_rows = lambda m: m.reshape(depth, RE_SLABS, LANES, SSM_WIDTH)
    cmat = jnp.concatenate([by_slab_rows(_block_diag_out(ssm_c_re)),
                            -by_slab_rows(_block_diag_out(ssm_c_im))], axis=2).astype(BF16)
    ssm_params = (
        bmat, a_re.reshape(depth, 1, N_STATE), a_im.reshape(depth, 1, N_STATE), cmat,
        ssm_d.reshape(depth, 1, SSM_WIDTH).astype(F32), ssm_glu_w.astype(BF16),
        ssm_glu_b.reshape(depth, 1, SSM_WIDTH).astype(F32))
    sink = swa_sink.astype(F32)
    small = ((_logit_bound(swa_q_norm, swa_k_norm) <= LOGIT_BOUND)
             & (jnp.max(jnp.abs(sink), axis=-1) * LOG2E <= LOGIT_BOUND)
             & (_logit_bound(moba_q_norm, moba_k_norm) <= LOGIT_BOUND))

    def attention(layer, shifted, a_q, a_g, a_kv, m_q, m_qsum, m_g, m_kv, sink):
        return (_swa(a_q, a_g, a_kv, sink, layer, shifted), _moba(m_q, m_qsum, m_g, m_kv, shifted))

    proj = _layer_edge(x, None, edge_params, None, 0, tl)
    for layer in range(depth):
        s_u, s_g, a_q, a_g, m_q, m_qsum, m_g, a_kv, m_kv = proj
        y_swa, y_moba = lax.cond(small[layer], functools.partial(attention, layer, False),
                                 functools.partial(attention, layer, True),
                                 a_q, a_g, a_kv, m_q, m_qsum, m_g, m_kv, sink)
        branch_outs = (_ssm(s_u, s_g, ssm_params, layer, tc), y_swa, y_moba)
        nxt = layer + 1 if layer + 1 < depth else None
        x, *proj = _layer_edge(x, branch_outs, edge_params, layer, nxt, tl)
    return x
```

```python
import functools
import math

import jax
import jax.numpy as jnp
from jax import lax
from jax.experimental import pallas as pl
from jax.experimental.pallas import tpu as pltpu

F32 = jnp.float32
BF16 = jnp.bfloat16

HEAD_DIM = 64
SSM_WIDTH = 256
SSM_GROUPS = 16
SSM_STATE = 64
N_STATE = SSM_GROUPS * SSM_STATE
N_Q_HEADS = 6
N_KV_HEADS = 2
Q_WIDTH = N_Q_HEADS * HEAD_DIM
KV_WIDTH = N_KV_HEADS * HEAD_DIM
SWA_WINDOW = 128
MOBA_BLOCK = 256
MOBA_TOPK = 3
RMS_EPS = 1e-6
ATTN_SCALE = HEAD_DIM ** -0.5
LOG2E = math.log2(math.e)
LANES = 128
N_Q_TILES = Q_WIDTH // LANES
VMEM_LIMIT = 56 * 1024 * 1024
SUM_ROWS = 16
PREP_ROWS = 256
OFF_U, OFF_SG = 0, SSM_WIDTH
OFF_AQ = 2 * SSM_WIDTH
OFF_AG = OFF_AQ + Q_WIDTH
OFF_MQ = OFF_AG + Q_WIDTH
OFF_MG = OFF_MQ + Q_WIDTH
OFF_AKV = OFF_MG + Q_WIDTH
OFF_MKV = OFF_AKV + 2 * KV_WIDTH
PROJ_CHUNK = 512
LOGIT_BOUND = 60.0
MASK_BIAS = -300.0


def _tiles(seq):
    return {"proj_rows": min(1024, seq), "fused_proj_rows": min(512, seq), "scan_steps": min(64, seq)}


def _cparams(*sem):
    return pltpu.CompilerParams(dimension_semantics=sem, vmem_limit_bytes=VMEM_LIMIT)


def _silu(t):
    return t * jax.nn.sigmoid(t)


def _ssm_prep_kernel(lr_ref, li_ref, ldt_ref, br_ref, bi_ref,
                     are_ref, aim_ref, bbr_ref, bbi_ref):
    lr = lr_ref[...]
    li = li_ref[...]
    dt = jnp.exp(ldt_ref[...])
    mag = jnp.exp(lr * dt)
    a_re = mag * jnp.cos(li * dt)
    a_im = mag * jnp.sin(li * dt)
    den = lr * lr + li * li
    nr = a_re - 1.0
    ni = a_im
    cr = (nr * lr + ni * li) / den
    ci = (ni * lr - nr * li) / den
    br = br_ref[...]
    bi = bi_ref[...]
    are_ref[...] = a_re
    aim_ref[...] = a_im
    bbr_ref[...] = cr * br - ci * bi
    bbi_ref[...] = cr * bi + ci * br


def _ssm_prep(lam_re, lam_im, log_dt, b_re, b_im):
    d, g, p = lam_re.shape
    h = b_re.shape[-1]
    shp = (d * g, h, p)
    bc = lambda a: jnp.broadcast_to(a.reshape(d * g, 1, -1), shp).astype(F32)
    tr = lambda a: jnp.swapaxes(a, -1, -2).reshape(shp).astype(F32)
    outs = pl.pallas_call(
        _ssm_prep_kernel,
        out_shape=[jax.ShapeDtypeStruct(shp, F32)] * 4,
        name="ssm_prep",
    )(bc(lam_re), bc(lam_im), bc(log_dt[..., None]), tr(b_re), tr(b_im))
    a_re, a_im, bb_re, bb_im = outs
    return (a_re[:, 0, :].reshape(d, g, p), a_im[:, 0, :].reshape(d, g, p),
            bb_re.reshape(d, g, h, p), bb_im.reshape(d, g, h, p))


def _pair_rms_norm(t, gain2):
    lane = lax.broadcasted_iota(jnp.int32, t.shape, 1)
    lo = lane < HEAD_DIM
    sq = t * t
    s_lo = jnp.sum(jnp.where(lo, sq, 0.0), axis=-1, keepdims=True)
    s_hi = jnp.sum(jnp.where(lo, 0.0, sq), axis=-1, keepdims=True)
    r_lo = lax.rsqrt(s_lo * (1.0 / HEAD_DIM) + RMS_EPS)
    r_hi = lax.rsqrt(s_hi * (1.0 / HEAD_DIM) + RMS_EPS)
    return t * jnp.where(lo, r_lo, r_hi) * gain2


def _half_select(t, kv_head):
    lane = lax.broadcasted_iota(jnp.int32, t.shape, 1)
    keep = (lane < HEAD_DIM) if kv_head == 0 else (lane >= HEAD_DIM)
    return jnp.where(keep, t, 0.0)


def _project(x, g_ref, w_ref, aqn_ref, akn_ref, mqn_ref, mkn_ref,
             u_ref, sg_ref, aq_ref, ag_ref, mq_ref, mqs_ref, mg_ref, akv_ref, mkv_ref):
    def lanes(j):
        return slice(j * LANES, (j + 1) * LANES)

    half = x.shape[0] // 2
    for r in range(2):
        rs = slice(r * half, (r + 1) * half)
        xr = x[rs]
        ms = jnp.mean(xr * xr, axis=-1, keepdims=True)
        h = (xr * lax.rsqrt(ms + RMS_EPS) * g_ref[...]).astype(BF16)
        chunks = {}

        def tile(col):
            c, k = divmod(col, PROJ_CHUNK)
            if c not in chunks:
                chunks[c] = jnp.dot(h, w_ref[:, c * PROJ_CHUNK:(c + 1) * PROJ_CHUNK],
                                    preferred_element_type=F32)
            return chunks[c][:, k:k + LANES]

        for off, kn_ref, kv_ref in ((OFF_MKV, mkn_ref, mkv_ref), (OFF_AKV, akn_ref, akv_ref)):
            kv_ref[0, rs, lanes(0)] = _pair_rms_norm(tile(off), kn_ref[...]).astype(BF16)
            kv_ref[0, rs, lanes(1)] = tile(off + KV_WIDTH).astype(BF16)

        qsum = jnp.zeros((half, LANES), F32)
        for j in reversed(range(N_Q_TILES)):
            mg_ref[0, rs, lanes(j)] = _silu(tile(OFF_MG + j * LANES)).astype(BF16)
            mqn = _pair_rms_norm(tile(OFF_MQ + j * LANES), mqn_ref[...])
            qsum = qsum + mqn
            mq_ref[0, rs, lanes(j)] = (mqn * (ATTN_SCALE * LOG2E)).astype(BF16)
        mqs_ref[0, rs, :] = qsum
        for j in reversed(range(N_Q_TILES)):
            ag_ref[0, rs, lanes(j)] = _silu(tile(OFF_AG + j * LANES)).astype(BF16)
            aq_ref[0, rs, lanes(j)] = (_pair_rms_norm(tile(OFF_AQ + j * LANES), aqn_ref[...])
                                       * (ATTN_SCALE * LOG2E)).astype(BF16)

        for j in range(SSM_WIDTH // LANES):
            sg_ref[0, rs, lanes(j)] = _silu(tile(OFF_SG + j * LANES)).astype(BF16)
            u_ref[0, rs, lanes(j)] = tile(OFF_U + j * LANES).astype(BF16)


def _residual_out(x_ref, ys_ref, ya_ref, ym_ref, wo_ref):
    y = jnp.concatenate([ys_ref[0], ya_ref[0], ym_ref[0]], axis=1)
    return x_ref[0] + jnp.dot(y, wo_ref[...], preferred_element_type=F32)


def _inproj_kernel(x_ref, *refs):
    _project(x_ref[0], *refs)


def _outproj_kernel(x_ref, ys_ref, ya_ref, ym_ref, wo_ref, o_ref):
    o_ref[0] = _residual_out(x_ref, ys_ref, ya_ref, ym_ref, wo_ref)


def _outin_kernel(x_ref, ys_ref, ya_ref, ym_ref, wo_ref, *refs):
    proj_in, proj_out = refs[:6], refs[7:]
    x = _residual_out(x_ref, ys_ref, ya_ref, ym_ref, wo_ref)
    refs[6][0] = x
    _project(x, *proj_in, *proj_out)


def _layer_edge(x, branch_outs, params, out_layer, in_layer, tiles):
    b, l, d = x.shape
    fused = out_layer is not None and in_layer is not None
    tl = tiles["fused_proj_rows" if fused else "proj_rows"]
    row = lambda width: pl.BlockSpec((1, tl, width), lambda i, t: (i, t, 0))
    shape = lambda width, dtype: jax.ShapeDtypeStruct((b, l, width), dtype)

    def layer_block(a, layer):
        return pl.BlockSpec((None,) + a.shape[1:], lambda i, t: (layer,) + (0,) * (a.ndim - 1))

    args, in_specs, out_specs, out_shape = [x], [row(d)], [], []
    if out_layer is not None:
        args += list(branch_outs) + [params["w_out"]]
        in_specs += [row(SSM_WIDTH), row(Q_WIDTH), row(Q_WIDTH),
                     layer_block(params["w_out"], out_layer)]
        out_specs.append(row(d))
        out_shape.append(shape(d, F32))
    if in_layer is not None:
        proj_params = [params["norm_g"], params["w_in"]] + params["head_norms"]
        args += proj_params
        in_specs += [layer_block(a, in_layer) for a in proj_params]
        out_specs += [row(SSM_WIDTH), row(SSM_WIDTH), row(Q_WIDTH), row(Q_WIDTH), row(Q_WIDTH),
                      row(LANES), row(Q_WIDTH), row(2 * KV_WIDTH), row(2 * KV_WIDTH)]
        out_shape += [shape(SSM_WIDTH, BF16), shape(SSM_WIDTH, BF16), shape(Q_WIDTH, BF16),
                      shape(Q_WIDTH, BF16), shape(Q_WIDTH, BF16), shape(LANES, F32),
                      shape(Q_WIDTH, BF16), shape(2 * KV_WIDTH, BF16), shape(2 * KV_WIDTH, BF16)]
    body, name = {(False, True): (_inproj_kernel, "inproj"),
                  (True, False): (_outproj_kernel, "outproj"),
                  (True, True): (_outin_kernel, "outin")}[(out_layer is not None, in_layer is not None)]
    outs = pl.pallas_call(
        body,
        grid=(b, l // tl),
        in_specs=in_specs,
        out_specs=out_specs,
        out_shape=out_shape,
        compiler_params=_cparams("parallel", "parallel"),
        name=name,
    )(*args)
    return outs


ROW_PAD = 4
N_SLABS = 2 * N_STATE // LANES
RE_SLABS = N_SLABS // 2
N_STAGES = 3


def _ssm_kernel(nb, tc, u_ref, u2_ref, sg2_ref, bmat_ref, are_ref, aim_ref, cmat_ref,
                d_ref, gw_ref, gb_ref, y_ref, upad_a, upad_c, ypad_s, st_s, *slots):
    pitch = tc + ROW_PAD
    i = pl.program_id(0)

    @pl.when(i == 0)
    def _():
        st_s[...] = jnp.zeros_like(st_s)
        upad_a[...] = jnp.zeros_like(upad_a)
        upad_c[...] = jnp.zeros_like(upad_c)
        for s in slots:
            s[...] = jnp.zeros_like(s)

    pieces = RE_SLABS
    steps = tc // pieces

    def stages(in_s, scan_s, out_s):
        for b in range(nb):
            upad_a[pl.ds(b * pitch, tc), :] = u_ref[b].astype(F32)
            upad_c[pl.ds(b * pitch, tc), :] = u2_ref[b].astype(F32)
        u_in = upad_a[...].astype(BF16)
        a_re = [jnp.broadcast_to(are_ref[:, j * LANES:(j + 1) * LANES], (nb, LANES))
                for j in range(RE_SLABS)]
        a_im = [jnp.broadcast_to(aim_ref[:, j * LANES:(j + 1) * LANES], (nb, LANES))
                for j in range(RE_SLABS)]
        state = [st_s[j] for j in range(N_SLABS)]
        y = d_ref[...] * upad_c[...]
        for k in range(pieces):
            bu = jnp.dot(u_in, bmat_ref[k], preferred_element_type=F32)
            in_s[k] = bu[:, 0:LANES]
            in_s[RE_SLABS + k] = bu[:, LANES:2 * LANES]
            for t in range(k * steps, (k + 1) * steps):
                rows = pl.ds(t, nb, stride=pitch)
                for j in range(RE_SLABS):
                    xr, xi = state[j], state[RE_SLABS + j]
                    nr = a_re[j] * xr - a_im[j] * xi + scan_s[j, rows, :]
                    ni = a_re[j] * xi + a_im[j] * xr + scan_s[RE_SLABS + j, rows, :]
                    scan_s[j, rows, :] = nr
                    scan_s[RE_SLABS + j, rows, :] = ni
                    state[j], state[RE_SLABS + j] = nr, ni
            xs = jnp.concatenate([out_s[k], out_s[RE_SLABS + k]], axis=1).astype(BF16)
            y = y + jnp.dot(xs, cmat_ref[k], preferred_element_type=F32)
        for j in range(N_SLABS):
            st_s[j] = state[j]
        y = jax.nn.gelu(y)
        z = jnp.dot(y.astype(BF16), gw_ref[...], preferred_element_type=F32) + gb_ref[...]
        ypad_s[...] = y * jax.nn.sigmoid(z)
        for b in range(nb):
            y_ref[b] = (ypad_s[pl.ds(b * pitch, tc), :] * sg2_ref[b].astype(F32)).astype(BF16)

    for r in range(N_STAGES):
        @pl.when(lax.rem(i, N_STAGES) == r)
        def _(r=r):
            stages(slots[r], slots[(r + 2) % N_STAGES], slots[(r + 1) % N_STAGES])


def _ssm(u, sg, ssm_params, layer, tc):
    nb, l, _ = u.shape
    rows = nb * (tc + ROW_PAD)
    n_chunks = l // tc
    chunk = lambda lag: pl.BlockSpec(
        (nb, tc, SSM_WIDTH), lambda i: (0, jnp.clip(i - lag, 0, n_chunks - 1), 0))
    full = lambda a: pl.BlockSpec((None,) + a.shape[1:], lambda i: (layer,) + (0,) * (a.ndim - 1))
    slab = pltpu.VMEM((N_SLABS, rows, LANES), F32)
    return pl.pallas_call(
        functools.partial(_ssm_kernel, nb, tc),
        grid=(n_chunks + N_STAGES - 1,),
        in_specs=[chunk(0), chunk(2), chunk(2)] + [full(a) for a in ssm_params],
        out_specs=chunk(2),
        out_shape=jax.ShapeDtypeStruct((nb, l, SSM_WIDTH), BF16),
        scratch_shapes=[pltpu.VMEM((rows, SSM_WIDTH), F32),
                        pltpu.VMEM((rows, SSM_WIDTH), F32),
                        pltpu.VMEM((rows, SSM_WIDTH), F32),
                        pltpu.VMEM((N_SLABS, nb, LANES), F32),
                        slab, slab, slab],
        compiler_params=_cparams("arbitrary"),
        name="ssm",
    )(u, u, sg, *ssm_params)


def _sum_row(n_keys):
    r = lax.broadcasted_iota(jnp.int32, (SUM_ROWS, n_keys), 0)
    return jnp.where(r == 0, 1.0, 0.0).astype(BF16)


def _prep_keys_values(kv_ref, ks_s, vs_s, seq):
    means = []
    for c in range(seq // PREP_ROWS):
        rows = slice(c * PREP_ROWS, (c + 1) * PREP_ROWS)
        kvb = kv_ref[0, rows, :].astype(F32)
        kf = kvb[:, 0:KV_WIDTH]
        means.append(jnp.mean(kf, axis=0, keepdims=True))
        ks_s[0, rows, :] = _half_select(kf, 0).astype(BF16)
        ks_s[1, rows, :] = _half_select(kf, 1).astype(BF16)
        vs_s[:, rows] = jnp.transpose(kvb[:, KV_WIDTH:2 * KV_WIDTH]).astype(BF16)
    return means


def _stacked_keys(ks_s, lo, hi):
    return jnp.concatenate([ks_s[0, lo:hi, :], ks_s[1, lo:hi, :]], axis=0)


def _weighted_values(vs_s, lo, hi, p, sum_row):
    nk = hi - lo
    outs, sums = [], []
    for h in range(N_KV_HEADS):
        v_t = jnp.concatenate([vs_s[h * HEAD_DIM:(h + 1) * HEAD_DIM, lo:hi], sum_row], axis=0)
        pv = jnp.dot(v_t, p[h * nk:(h + 1) * nk], preferred_element_type=F32)
        outs.append(pv[0:HEAD_DIM])
        sums.append(pv[HEAD_DIM:HEAD_DIM + 1])
    return outs, sums


def _queries_t(q_ref, rows):
    return jnp.concatenate(
        [jnp.transpose(q_ref[0, rows, j * LANES:(j + 1) * LANES]) for j in range(N_Q_TILES)], axis=1)


def _store_gated(o_ref, g_ref, rows, o_t, width):
    for j in range(N_Q_TILES):
        cols = slice(j * LANES, (j + 1) * LANES)
        o = jnp.transpose(o_t[:, j * width:(j + 1) * width]) * g_ref[0, rows, cols].astype(F32)
        o_ref[0, rows, cols] = o.astype(BF16)


_ATTN_SCRATCH = lambda seq: [pltpu.VMEM((N_KV_HEADS, seq, KV_WIDTH), BF16),
                             pltpu.VMEM((KV_WIDTH, seq), BF16)]


def _swa_kernel(nblk, layer, shifted, sink_ref, q_ref, g_ref, kv_ref, o_ref, ks_s, vs_s):
    w = SWA_WINDOW
    nq = N_Q_TILES * w
    _prep_keys_values(kv_ref, ks_s, vs_s, nblk * w)
    lane = lax.broadcasted_iota(jnp.int32, (1, nq), 1)
    sink2 = []
    for h in range(N_KV_HEADS):
        a = [sink_ref[layer, j + N_Q_TILES * h] * LOG2E for j in range(N_Q_TILES)]
        sink2.append(jnp.where(lane < w, a[0], jnp.where(lane < 2 * w, a[1], a[2])))

    def window_mask(n_keys):
        kpos = (lax.broadcasted_iota(jnp.int32, (2 * n_keys, nq), 0) & (n_keys - 1)) - (n_keys - w)
        qpos = lax.broadcasted_iota(jnp.int32, (2 * n_keys, nq), 1) & (w - 1)
        rel = qpos - kpos
        return (rel >= 0) & (rel < w)

    masks = {w: window_mask(w), 2 * w: window_mask(2 * w)}
    sums = {w: _sum_row(w), 2 * w: _sum_row(2 * w)}

    for n in range(nblk):
        rows = slice(n * w, (n + 1) * w)
        lo = max(n - 1, 0) * w
        hi = (n + 1) * w
        nk = hi - lo
        q_t = _queries_t(q_ref, rows)
        s = jnp.where(masks[nk],
                      jnp.dot(_stacked_keys(ks_s, lo, hi), q_t, preferred_element_type=F32), -jnp.inf)
        if shifted:
            m = [jnp.maximum(jnp.max(s[h * nk:(h + 1) * nk], axis=0, keepdims=True), sink2[h])
                 for h in range(N_KV_HEADS)]
            p = jnp.concatenate([jnp.exp2(s[h * nk:(h + 1) * nk] - m[h])
                                 for h in range(N_KV_HEADS)], axis=0).astype(BF16)
            sink_p = [jnp.exp2(sink2[h] - m[h]) for h in range(N_KV_HEADS)]
        else:
            p = jnp.exp2(s).astype(BF16)
            sink_p = [jnp.exp2(sink2[h]) for h in range(N_KV_HEADS)]
        outs, sums_p = _weighted_values(vs_s, lo, hi, p, sums[nk])
        o_t = jnp.concatenate([outs[h] * (1.0 / (sums_p[h] + sink_p[h]))
                               for h in range(N_KV_HEADS)], axis=0)
        _store_gated(o_ref, g_ref, rows, o_t, w)


def _swa(q, g, kv, sink, layer, shifted):
    b, l, _ = q.shape
    qspec = pl.BlockSpec((1, l, Q_WIDTH), lambda i: (i, 0, 0))
    return pl.pallas_call(
        functools.partial(_swa_kernel, l // SWA_WINDOW, layer, shifted),
        grid=(b,),
        in_specs=[pl.BlockSpec(memory_space=pltpu.SMEM),
                  qspec, qspec,
                  pl.BlockSpec((1, l, 2 * KV_WIDTH), lambda i: (i, 0, 0))],
        out_specs=qspec,
        out_shape=jax.ShapeDtypeStruct((b, l, Q_WIDTH), BF16),
        scratch_shapes=_ATTN_SCRATCH(l),
        compiler_params=_cparams("parallel"),
        name="swa",
    )(sink, q, g, kv)


def _moba_kernel(nblk, shifted, q_ref, qs_ref, g_ref, kv_ref, o_ref, ks_s, vs_s):
    blk = MOBA_BLOCK
    assert blk == PREP_ROWS
    nq = N_Q_TILES * blk
    kmean = jnp.concatenate(_prep_keys_values(kv_ref, ks_s, vs_s, nblk * blk), axis=0)
    sum_row = _sum_row(blk)
    if not shifted:
        r = lax.broadcasted_iota(jnp.int32, (2 * blk, LANES), 0)
        c = lax.broadcasted_iota(jnp.int32, (2 * blk, LANES), 1)
        head_cols = jnp.where(((c == 0) & (r < blk)) | ((c == 1) & (r >= blk)), 1.0, 0.0).astype(BF16)
        bias_row = lax.broadcasted_iota(jnp.int32, (SUM_ROWS, nq), 0)
        bias_pad = jnp.zeros((LANES - SUM_ROWS, nq), BF16)

    blk_id = lax.broadcasted_iota(jnp.int32, (nblk, blk), 0)
    ki = lax.broadcasted_iota(jnp.int32, (2 * blk, nq), 0) & (blk - 1)
    qi = lax.broadcasted_iota(jnp.int32, (2 * blk, nq), 1) & (blk - 1)
    causal = ki <= qi

    for i in range(nblk):
        rows = slice(i * blk, (i + 1) * blk)
        q_t = _queries_t(q_ref, rows)
        qsum = qs_ref[0, rows, :]

        past = blk_id < i
        sel = []
        for h in range(N_KV_HEADS):
            gate = lax.dot_general(_half_select(kmean, h), _half_select(qsum, h),
                                   (((1,), (1,)), ((), ())),
                                   precision=lax.Precision.HIGHEST,
                                   preferred_element_type=F32)
            gate = jnp.where(past, gate, -jnp.inf)
            rank = jnp.zeros((nblk, blk), jnp.int32)
            for n2 in range(nblk):
                other = gate[n2:n2 + 1, :]
                ahead = (other > gate) | ((other == gate) & (n2 < blk_id))
                rank = rank + ahead.astype(jnp.int32)
            chosen = jnp.where((rank < MOBA_TOPK) & past, 1.0, 0.0)
            sel.append(jnp.concatenate([chosen] * N_Q_TILES, axis=1))

        kst = _stacked_keys(ks_s, i * blk, (i + 1) * blk)
        s = jnp.where(causal, jnp.dot(kst, q_t, preferred_element_type=F32), -jnp.inf)
        if shifted:
            m = [jnp.max(s[h * blk:(h + 1) * blk], axis=0, keepdims=True)
                 for h in range(N_KV_HEADS)]
            p = jnp.concatenate([jnp.exp2(s[h * blk:(h + 1) * blk] - m[h])
                                 for h in range(N_KV_HEADS)], axis=0).astype(BF16)
        else:
            p = jnp.exp2(s).astype(BF16)
        acc, den = _weighted_values(vs_s, i * blk, (i + 1) * blk, p, sum_row)

        for n in range(i):
            kst = _stacked_keys(ks_s, n * blk, (n + 1) * blk)
            ok = [sel[h][n:n + 1, :] > 0.5 for h in range(N_KV_HEADS)]
            if shifted:
                s = jnp.dot(kst, q_t, preferred_element_type=F32)
                alpha = []
                parts = []
                for h in range(N_KV_HEADS):
                    s_h = s[h * blk:(h + 1) * blk]
                    m_new = jnp.where(ok[h], jnp.maximum(m[h], jnp.max(s_h, axis=0, keepdims=True)),
                                      m[h])
                    alpha.append(jnp.exp2(m[h] - m_new))
                    parts.append(jnp.exp2(s_h - jnp.where(ok[h], m_new, jnp.inf)))
                    m[h] = m_new
                p = jnp.concatenate(parts, axis=0).astype(BF16)
                outs, sums_p = _weighted_values(vs_s, n * blk, (n + 1) * blk, p, sum_row)
                acc = [acc[h] * alpha[h] + outs[h] for h in range(N_KV_HEADS)]
                den = [den[h] * alpha[h] + sums_p[h] for h in range(N_KV_HEADS)]
            else:
                bias = jnp.where(bias_row == 0, jnp.where(ok[0], 0.0, MASK_BIAS),
                                 jnp.where(bias_row == 1, jnp.where(ok[1], 0.0, MASK_BIAS), 0.0))
                q_ext = jnp.concatenate([q_t, bias.astype(BF16), bias_pad], axis=0)
                k_ext = jnp.concatenate([kst, head_cols], axis=1)
                p = jnp.exp2(jnp.dot(k_ext, q_ext, preferred_element_type=F32)).astype(BF16)
                outs, sums_p = _weighted_values(vs_s, n * blk, (n + 1) * blk, p, sum_row)
                acc = [acc[h] + outs[h] for h in range(N_KV_HEADS)]
                den = [den[h] + sums_p[h] for h in range(N_KV_HEADS)]

        o_t = jnp.concatenate([acc[h] * (1.0 / den[h]) for h in range(N_KV_HEADS)], axis=0)
        _store_gated(o_ref, g_ref, rows, o_t, blk)


def _moba(q, qsum, g, kv, shifted):
    b, l, _ = q.shape
    qspec = pl.BlockSpec((1, l, Q_WIDTH), lambda i: (i, 0, 0))
    return pl.pallas_call(
        functools.partial(_moba_kernel, l // MOBA_BLOCK, shifted),
        grid=(b,),
        in_specs=[qspec,
                  pl.BlockSpec((1, l, LANES), lambda i: (i, 0, 0)),
                  qspec,
                  pl.BlockSpec((1, l, 2 * KV_WIDTH), lambda i: (i, 0, 0))],
        out_specs=qspec,
        out_shape=jax.ShapeDtypeStruct((b, l, Q_WIDTH), BF16),
        scratch_shapes=_ATTN_SCRATCH(l),
        compiler_params=_cparams("parallel"),
        name="moba",
    )(q, qsum, g, kv)


def _head_tiles(w):
    lead = w.shape[:-1]
    w = w.reshape(*lead, N_KV_HEADS, N_Q_TILES, HEAD_DIM)
    return jnp.swapaxes(w, -3, -2).reshape(*lead, Q_WIDTH)


def _permute_w_in(w):
    s, q, k = SSM_WIDTH, Q_WIDTH, KV_WIDTH
    bounds = [0]
    for n in (s, s, q, k, k, q, q, k, k, q):
        bounds.append(bounds[-1] + n)
    s_u, s_g, a_q, a_k, a_v, a_g, m_q, m_k, m_v, m_g = [
        w[..., bounds[i]:bounds[i + 1]] for i in range(10)]
    return jnp.concatenate(
        [s_u, s_g, _head_tiles(a_q), _head_tiles(a_g), _head_tiles(m_q), _head_tiles(m_g),
         a_k, a_v, m_k, m_v], axis=-1).astype(BF16)


def _permute_w_out(w):
    s, q = SSM_WIDTH, Q_WIDTH
    rows_t = lambda part: jnp.swapaxes(_head_tiles(jnp.swapaxes(part, -1, -2)), -1, -2)
    return jnp.concatenate([w[:, 0:s], rows_t(w[:, s:s + q]), rows_t(w[:, s + q:])],
                           axis=1).astype(BF16)


def _block_diag_in(bb):
    d, g, h, p = bb.shape
    eye = jnp.eye(g, dtype=bb.dtype)[None, :, None, :, None]
    return (bb[:, :, :, None, :] * eye).reshape(d, g * h, g * p)


def _block_diag_out(c):
    d, g, h, p = c.shape
    eye = jnp.eye(g, dtype=c.dtype)[None, :, None, :, None]
    return (jnp.swapaxes(c, -1, -2)[:, :, :, None, :] * eye).reshape(d, g * p, g * h)


def _logit_bound(q_gain, k_gain):
    return ((HEAD_DIM * ATTN_SCALE * LOG2E) * jnp.max(jnp.abs(q_gain), axis=-1)
            * jnp.max(jnp.abs(k_gain), axis=-1))


def kernel(x, norm_g, w_in, ssm_lam_re, ssm_lam_im, ssm_log_dt, ssm_b_re, ssm_b_im,
           ssm_c_re, ssm_c_im, ssm_d, ssm_glu_w, ssm_glu_b, swa_q_norm, swa_k_norm,
           swa_sink, moba_q_norm, moba_k_norm, w_out):
    b, l, d = x.shape
    depth = norm_g.shape[0]
    tiles = _tiles(l)
    a_re, a_im, bb_re, bb_im = _ssm_prep(ssm_lam_re, ssm_lam_im, ssm_log_dt, ssm_b_re, ssm_b_im)
    two = lambda v: jnp.concatenate([v, v], axis=-1).reshape(depth, 1, LANES).astype(F32)
    edge_params = {
        "norm_g": norm_g.reshape(depth, 1, d).astype(F32),
        "w_in": _permute_w_in(w_in),
        "w_out": _permute_w_out(w_out),
        "head_norms": [two(v) for v in (swa_q_norm, swa_k_norm, moba_q_norm, moba_k_norm)],
    }
    by_slab_cols = lambda m: m.reshape(depth, SSM_WIDTH, RE_SLABS, LANES)
    bmat = jnp.concatenate([by_slab_cols(_block_diag_in(bb_re)), by_slab_cols(_block_diag_in(bb_im))],
                           axis=-1).transpose(0, 2, 1, 3).astype(BF16)
    by_slab_rows = lambda m: m.reshape(depth, RE_SLABS, LANES, SSM_WIDTH)
    cmat = jnp.concatenate([by_slab_rows(_block_diag_out(ssm_c_re)),
                            -by_slab_rows(_block_diag_out(ssm_c_im))], axis=2).astype(BF16)
    ssm_params = (
        bmat, a_re.reshape(depth, 1, N_STATE), a_im.reshape(depth, 1, N_STATE), cmat,
        ssm_d.reshape(depth, 1, SSM_WIDTH).astype(F32), ssm_glu_w.astype(BF16),
        ssm_glu_b.reshape(depth, 1, SSM_WIDTH).astype(F32))
    sink = swa_sink.astype(F32)
    small = ((_logit_bound(swa_q_norm, swa_k_norm) <= LOGIT_BOUND)
             & (jnp.max(jnp.abs(sink), axis=-1) * LOG2E <= LOGIT_BOUND)
             & (_logit_bound(moba_q_norm, moba_k_norm) <= LOGIT_BOUND))

    def attention(layer, shifted, a_q, a_g, a_kv, m_q, m_qsum, m_g, m_kv, sink):
        return (_swa(a_q, a_g, a_kv, sink, layer, shifted), _moba(m_q, m_qsum, m_g, m_kv, shifted))

    proj = _layer_edge(x, None, edge_params, None, 0, tiles)
    for layer in range(depth):
        s_u, s_g, a_q, a_g, m_q, m_qsum, m_g, a_kv, m_kv = proj
        y_swa, y_moba = lax.cond(small[layer], functools.partial(attention, layer, False),
                                 functools.partial(attention, layer, True),
                                 a_q, a_g, a_kv, m_q, m_qsum, m_g, m_kv, sink)
        branch_outs = (_ssm(s_u, s_g, ssm_params, layer, tiles["scan_steps"]), y_swa, y_moba)
        nxt = layer + 1 if layer + 1 < depth else None
        x, *proj = _layer_edge(x, branch_outs, edge_params, layer, nxt, tiles)
    return x
```

```python
import functools
import math

import jax
import jax.numpy as jnp
from jax import lax
from jax.experimental import pallas as pl
from jax.experimental.pallas import tpu as pltpu

F32 = jnp.float32
BF16 = jnp.bfloat16

HEAD_DIM = 64
SSM_WIDTH = 256
SSM_GROUPS = 16
SSM_STATE = 64
N_STATE = SSM_GROUPS * SSM_STATE
N_Q_HEADS = 6
N_KV_HEADS = 2
Q_WIDTH = N_Q_HEADS * HEAD_DIM
KV_WIDTH = N_KV_HEADS * HEAD_DIM
SWA_WINDOW = 128
MOBA_BLOCK = 256
MOBA_TOPK = 3
RMS_EPS = 1e-6
ATTN_SCALE = HEAD_DIM ** -0.5
LOG2E = math.log2(math.e)
LANES = 128
N_Q_TILES = Q_WIDTH // LANES
VMEM_LIMIT = 56 * 1024 * 1024
SUM_ROWS = 16
OFF_U, OFF_SG = 0, SSM_WIDTH
OFF_AQ = 2 * SSM_WIDTH
OFF_AG = OFF_AQ + Q_WIDTH
OFF_MQ = OFF_AG + Q_WIDTH
OFF_MG = OFF_MQ + Q_WIDTH
OFF_AKV = OFF_MG + Q_WIDTH
OFF_MKV = OFF_AKV + 2 * KV_WIDTH
PROJ_CHUNK = 512
LOGIT_BOUND = 60.0


def _tiles(seq):
    return {"proj_rows": min(1024, seq), "fused_proj_rows": min(1024, seq), "scan_steps": min(64, seq)}


def _cparams(*sem):
    return pltpu.CompilerParams(dimension_semantics=sem, vmem_limit_bytes=VMEM_LIMIT)


def _silu(t):
    return t * jax.nn.sigmoid(t)


def _ssm_prep_kernel(lr_ref, li_ref, ldt_ref, br_ref, bi_ref,
                     are_ref, aim_ref, bbr_ref, bbi_ref):
    lr = lr_ref[...]
    li = li_ref[...]
    dt = jnp.exp(ldt_ref[...])
    mag = jnp.exp(lr * dt)
    a_re = mag * jnp.cos(li * dt)
    a_im = mag * jnp.sin(li * dt)
    den = lr * lr + li * li
    nr = a_re - 1.0
    ni = a_im
    cr = (nr * lr + ni * li) / den
    ci = (ni * lr - nr * li) / den
    br = br_ref[...]
    bi = bi_ref[...]
    are_ref[...] = a_re
    aim_ref[...] = a_im
    bbr_ref[...] = cr * br - ci * bi
    bbi_ref[...] = cr * bi + ci * br


def _ssm_prep(lam_re, lam_im, log_dt, b_re, b_im):
    d, g, p = lam_re.shape
    h = b_re.shape[-1]
    shp = (d * g, h, p)
    bc = lambda a: jnp.broadcast_to(a.reshape(d * g, 1, -1), shp).astype(F32)
    tr = lambda a: jnp.swapaxes(a, -1, -2).reshape(shp).astype(F32)
    outs = pl.pallas_call(
        _ssm_prep_kernel,
        out_shape=[jax.ShapeDtypeStruct(shp, F32)] * 4,
        name="ssm_prep",
    )(bc(lam_re), bc(lam_im), bc(log_dt[..., None]), tr(b_re), tr(b_im))
    a_re, a_im, bb_re, bb_im = outs
    return (a_re[:, 0, :].reshape(d, g, p), a_im[:, 0, :].reshape(d, g, p),
            bb_re.reshape(d, g, h, p), bb_im.reshape(d, g, h, p))


def _pair_rms_norm(t, gain2):
    lane = lax.broadcasted_iota(jnp.int32, t.shape, 1)
    lo = lane < HEAD_DIM
    sq = t * t
    s_lo = jnp.sum(jnp.where(lo, sq, 0.0), axis=-1, keepdims=True)
    s_hi = jnp.sum(jnp.where(lo, 0.0, sq), axis=-1, keepdims=True)
    r_lo = lax.rsqrt(s_lo * (1.0 / HEAD_DIM) + RMS_EPS)
    r_hi = lax.rsqrt(s_hi * (1.0 / HEAD_DIM) + RMS_EPS)
    return t * jnp.where(lo, r_lo, r_hi) * gain2


def _half_select(t, kv_head):
    lane = lax.broadcasted_iota(jnp.int32, t.shape, 1)
    keep = (lane < HEAD_DIM) if kv_head == 0 else (lane >= HEAD_DIM)
    return jnp.where(keep, t, 0.0)


def _project(x, g_ref, w_ref, aqn_ref, akn_ref, mqn_ref, mkn_ref,
             u_ref, sg_ref, aq_ref, ag_ref, mq_ref, mqs_ref, mg_ref,
             ak0_ref, ak1_ref, avt_ref, mk0_ref, mk1_ref, mvt_ref):
    def lanes(j):
        return slice(j * LANES, (j + 1) * LANES)

    half = x.shape[0] // 2
    for r in range(2):
        rs = slice(r * half, (r + 1) * half)
        xr = x[rs]
        ms = jnp.mean(xr * xr, axis=-1, keepdims=True)
        h = (xr * lax.rsqrt(ms + RMS_EPS) * g_ref[...]).astype(BF16)
        chunks = {}

        def tile(col):
            c, k = divmod(col, PROJ_CHUNK)
            if c not in chunks:
                chunks[c] = jnp.dot(h, w_ref[:, c * PROJ_CHUNK:(c + 1) * PROJ_CHUNK],
                                    preferred_element_type=F32)
            return chunks[c][:, k:k + LANES]

        for off, kn_ref, k0_ref, k1_ref, vt_ref in ((OFF_MKV, mkn_ref, mk0_ref, mk1_ref, mvt_ref),
                                                    (OFF_AKV, akn_ref, ak0_ref, ak1_ref, avt_ref)):
            k = _pair_rms_norm(tile(off), kn_ref[...])
            k0_ref[0, rs, :] = _half_select(k, 0).astype(BF16)
            k1_ref[0, rs, :] = _half_select(k, 1).astype(BF16)
            vt_ref[0, :, rs] = jnp.transpose(tile(off + KV_WIDTH)).astype(BF16)

        qsum = jnp.zeros((half, LANES), F32)
        for j in reversed(range(N_Q_TILES)):
            mg_ref[0, rs, lanes(j)] = _silu(tile(OFF_MG + j * LANES)).astype(BF16)
            mqn = _pair_rms_norm(tile(OFF_MQ + j * LANES), mqn_ref[...])
            qsum = qsum + mqn
            mq_ref[0, rs, lanes(j)] = (mqn * (ATTN_SCALE * LOG2E)).astype(BF16)
        mqs_ref[0, rs, :] = qsum
        for j in reversed(range(N_Q_TILES)):
            ag_ref[0, rs, lanes(j)] = _silu(tile(OFF_AG + j * LANES)).astype(BF16)
            aq_ref[0, rs, lanes(j)] = (_pair_rms_norm(tile(OFF_AQ + j * LANES), aqn_ref[...])
                                       * (ATTN_SCALE * LOG2E)).astype(BF16)

        for j in range(SSM_WIDTH // LANES):
            sg_ref[0, rs, lanes(j)] = _silu(tile(OFF_SG + j * LANES)).astype(BF16)
            u_ref[0, rs, lanes(j)] = tile(OFF_U + j * LANES).astype(BF16)


def _residual_out(x_ref, ys_ref, ya_ref, ym_ref, wo_ref):
    y = jnp.concatenate([ys_ref[0], ya_ref[0], ym_ref[0]], axis=1)
    return x_ref[0] + jnp.dot(y, wo_ref[...], preferred_element_type=F32)


def _inproj_kernel(x_ref, *refs):
    _project(x_ref[0], *refs)


def _outproj_kernel(x_ref, ys_ref, ya_ref, ym_ref, wo_ref, o_ref):
    o_ref[0] = _residual_out(x_ref, ys_ref, ya_ref, ym_ref, wo_ref)


def _outin_kernel(x_ref, ys_ref, ya_ref, ym_ref, wo_ref, *refs):
    proj_in, proj_out = refs[:6], refs[7:]
    x = _residual_out(x_ref, ys_ref, ya_ref, ym_ref, wo_ref)
    refs[6][0] = x
    _project(x, *proj_in, *proj_out)


def _layer_edge(x, branch_outs, params, out_layer, in_layer, tiles):
    b, l, d = x.shape
    fused = out_layer is not None and in_layer is not None
    tl = tiles["fused_proj_rows" if fused else "proj_rows"]
    row = lambda width: pl.BlockSpec((1, tl, width), lambda i, t: (i, t, 0))
    shape = lambda width, dtype: jax.ShapeDtypeStruct((b, l, width), dtype)

    def layer_block(a, layer):
        return pl.BlockSpec((None,) + a.shape[1:], lambda i, t: (layer,) + (0,) * (a.ndim - 1),
                            pipeline_mode=pl.Buffered(1))

    args, in_specs, out_specs, out_shape = [x], [row(d)], [], []
    if out_layer is not None:
        args += list(branch_outs) + [params["w_out"]]
        in_specs += [row(SSM_WIDTH), row(Q_WIDTH), row(Q_WIDTH),
                     layer_block(params["w_out"], out_layer)]
        out_specs.append(row(d))
        out_shape.append(shape(d, F32))
    if in_layer is not None:
        proj_params = [params["norm_g"], params["w_in"]] + params["head_norms"]
        args += proj_params
        in_specs += [layer_block(a, in_layer) for a in proj_params]
        keys_values = ([row(KV_WIDTH), row(KV_WIDTH),
                        pl.BlockSpec((1, KV_WIDTH, tl), lambda i, t: (i, 0, t))],
                       [shape(KV_WIDTH, BF16), shape(KV_WIDTH, BF16),
                        jax.ShapeDtypeStruct((b, KV_WIDTH, l), BF16)])
        out_specs += [row(SSM_WIDTH), row(SSM_WIDTH), row(Q_WIDTH), row(Q_WIDTH), row(Q_WIDTH),
                      row(LANES), row(Q_WIDTH)] + 2 * keys_values[0]
        out_shape += [shape(SSM_WIDTH, BF16), shape(SSM_WIDTH, BF16), shape(Q_WIDTH, BF16),
                      shape(Q_WIDTH, BF16), shape(Q_WIDTH, BF16), shape(LANES, F32),
                      shape(Q_WIDTH, BF16)] + 2 * keys_values[1]
    body, name = {(False, True): (_inproj_kernel, "inproj"),
                  (True, False): (_outproj_kernel, "outproj"),
                  (True, True): (_outin_kernel, "outin")}[(out_layer is not None, in_layer is not None)]
    outs = pl.pallas_call(
        body,
        grid=(b, l // tl),
        in_specs=in_specs,
        out_specs=out_specs,
        out_shape=out_shape,
        compiler_params=_cparams("parallel", "parallel"),
        name=name,
    )(*args)
    return outs


ROW_PAD = 4
N_SLABS = 2 * N_STATE // LANES
RE_SLABS = N_SLABS // 2
N_STAGES = 3


def _ssm_kernel(nb, tc, u_ref, u2_ref, sg2_ref, bmat_ref, are_ref, aim_ref, cmat_ref,
                d_ref, gw_ref, gb_ref, y_ref, upad_a, upad_c, ypad_s, st_s, *slots):
    pitch = tc + ROW_PAD
    i = pl.program_id(0)

    @pl.when(i == 0)
    def _():
        st_s[...] = jnp.zeros_like(st_s)
        upad_a[...] = jnp.zeros_like(upad_a)
        upad_c[...] = jnp.zeros_like(upad_c)
        for s in slots:
            s[...] = jnp.zeros_like(s)

    pieces = RE_SLABS
    steps = tc // pieces

    def stages(in_s, scan_s, out_s):
        for b in range(nb):
            upad_a[pl.ds(b * pitch, tc), :] = u_ref[b].astype(F32)
            upad_c[pl.ds(b * pitch, tc), :] = u2_ref[b].astype(F32)
        u_in = upad_a[...].astype(BF16)
        a_re = [jnp.broadcast_to(are_ref[:, j * LANES:(j + 1) * LANES], (nb, LANES))
                for j in range(RE_SLABS)]
        a_im = [jnp.broadcast_to(aim_ref[:, j * LANES:(j + 1) * LANES], (nb, LANES))
                for j in range(RE_SLABS)]
        state = [st_s[j] for j in range(N_SLABS)]
        y = d_ref[...] * upad_c[...]
        for k in range(pieces):
            bu = jnp.dot(u_in, bmat_ref[k], preferred_element_type=F32)
            in_s[k] = bu[:, 0:LANES]
            in_s[RE_SLABS + k] = bu[:, LANES:2 * LANES]
            for t in range(k * steps, (k + 1) * steps):
                rows = pl.ds(t, nb, stride=pitch)
                for j in range(RE_SLABS):
                    xr, xi = state[j], state[RE_SLABS + j]
                    nr = a_re[j] * xr - a_im[j] * xi + scan_s[j, rows, :]
                    ni = a_re[j] * xi + a_im[j] * xr + scan_s[RE_SLABS + j, rows, :]
                    scan_s[j, rows, :] = nr
                    scan_s[RE_SLABS + j, rows, :] = ni
                    state[j], state[RE_SLABS + j] = nr, ni
            xs = jnp.concatenate([out_s[k], out_s[RE_SLABS + k]], axis=1).astype(BF16)
            y = y + jnp.dot(xs, cmat_ref[k], preferred_element_type=F32)
        for j in range(N_SLABS):
            st_s[j] = state[j]
        y = jax.nn.gelu(y)
        z = jnp.dot(y.astype(BF16), gw_ref[...], preferred_element_type=F32) + gb_ref[...]
        ypad_s[...] = y * jax.nn.sigmoid(z)
        for b in range(nb):
            y_ref[b] = (ypad_s[pl.ds(b * pitch, tc), :] * sg2_ref[b].astype(F32)).astype(BF16)

    for r in range(N_STAGES):
        @pl.when(lax.rem(i, N_STAGES) == r)
        def _(r=r):
            stages(slots[r], slots[(r + 2) % N_STAGES], slots[(r + 1) % N_STAGES])


def _ssm(u, sg, ssm_params, layer, tc):
    nb, l, _ = u.shape
    rows = nb * (tc + ROW_PAD)
    n_chunks = l // tc
    chunk = lambda lag: pl.BlockSpec(
        (nb, tc, SSM_WIDTH), lambda i: (0, jnp.clip(i - lag, 0, n_chunks - 1), 0))
    full = lambda a: pl.BlockSpec((None,) + a.shape[1:], lambda i: (layer,) + (0,) * (a.ndim - 1))
    slab = pltpu.VMEM((N_SLABS, rows, LANES), F32)
    return pl.pallas_call(
        functools.partial(_ssm_kernel, nb, tc),
        grid=(n_chunks + N_STAGES - 1,),
        in_specs=[chunk(0), chunk(2), chunk(2)] + [full(a) for a in ssm_params],
        out_specs=chunk(2),
        out_shape=jax.ShapeDtypeStruct((nb, l, SSM_WIDTH), BF16),
        scratch_shapes=[pltpu.VMEM((rows, SSM_WIDTH), F32),
                        pltpu.VMEM((rows, SSM_WIDTH), F32),
                        pltpu.VMEM((rows, SSM_WIDTH), F32),
                        pltpu.VMEM((N_SLABS, nb, LANES), F32),
                        slab, slab, slab],
        compiler_params=_cparams("arbitrary"),
        name="ssm",
    )(u, u, sg, *ssm_params)


def _sum_row(n_keys):
    r = lax.broadcasted_iota(jnp.int32, (SUM_ROWS, n_keys), 0)
    return jnp.where(r == 0, 1.0, 0.0).astype(BF16)


def _stacked_keys(k0_ref, k1_ref, lo, hi):
    return jnp.concatenate([k0_ref[0, lo:hi, :], k1_ref[0, lo:hi, :]], axis=0)


def _weighted_values(vt_ref, lo, hi, p, sum_row):
    nk = hi - lo
    outs, sums = [], []
    for h in range(N_KV_HEADS):
        v_t = jnp.concatenate([vt_ref[0, h * HEAD_DIM:(h + 1) * HEAD_DIM, lo:hi], sum_row], axis=0)
        pv = jnp.dot(v_t, p[h * nk:(h + 1) * nk], preferred_element_type=F32)
        outs.append(pv[0:HEAD_DIM])
        sums.append(pv[HEAD_DIM:HEAD_DIM + 1])
    return outs, sums


def _queries_t(q_ref, rows):
    return jnp.concatenate(
        [jnp.transpose(q_ref[0, rows, j * LANES:(j + 1) * LANES]) for j in range(N_Q_TILES)], axis=1)


def _store_gated(o_ref, g_ref, rows, o_t, width):
    for j in range(N_Q_TILES):
        cols = slice(j * LANES, (j + 1) * LANES)
        o = jnp.transpose(o_t[:, j * width:(j + 1) * width]) * g_ref[0, rows, cols].astype(F32)
        o_ref[0, rows, cols] = o.astype(BF16)


def _keys_values_specs(seq):
    keys = pl.BlockSpec((1, seq, KV_WIDTH), lambda i: (i, 0, 0))
    return [keys, keys, pl.BlockSpec((1, KV_WIDTH, seq), lambda i: (i, 0, 0))]


def _swa_kernel(nblk, layer, shifted, sink_ref, q_ref, g_ref, k0_ref, k1_ref, vt_ref, o_ref):
    w = SWA_WINDOW
    nq = N_Q_TILES * w
    lane = lax.broadcasted_iota(jnp.int32, (1, nq), 1)
    sink2 = []
    for h in range(N_KV_HEADS):
        a = [sink_ref[layer, j + N_Q_TILES * h] * LOG2E for j in range(N_Q_TILES)]
        sink2.append(jnp.where(lane < w, a[0], jnp.where(lane < 2 * w, a[1], a[2])))

    def window_mask(n_keys):
        kpos = (lax.broadcasted_iota(jnp.int32, (2 * n_keys, nq), 0) & (n_keys - 1)) - (n_keys - w)
        qpos = lax.broadcasted_iota(jnp.int32, (2 * n_keys, nq), 1) & (w - 1)
        rel = qpos - kpos
        return (rel >= 0) & (rel < w)

    masks = {w: window_mask(w), 2 * w: window_mask(2 * w)}
    sums = {w: _sum_row(w), 2 * w: _sum_row(2 * w)}

    for n in range(nblk):
        rows = slice(n * w, (n + 1) * w)
        lo = max(n - 1, 0) * w
        hi = (n + 1) * w
        nk = hi - lo
        q_t = _queries_t(q_ref, rows)
        s = jnp.where(masks[nk], jnp.dot(_stacked_keys(k0_ref, k1_ref, lo, hi), q_t,
                                         preferred_element_type=F32), -jnp.inf)
        if shifted:
            m = [jnp.maximum(jnp.max(s[h * nk:(h + 1) * nk], axis=0, keepdims=True), sink2[h])
                 for h in range(N_KV_HEADS)]
            p = jnp.concatenate([jnp.exp2(s[h * nk:(h + 1) * nk] - m[h])
                                 for h in range(N_KV_HEADS)], axis=0).astype(BF16)
            sink_p = [jnp.exp2(sink2[h] - m[h]) for h in range(N_KV_HEADS)]
        else:
            p = jnp.exp2(s).astype(BF16)
            sink_p = [jnp.exp2(sink2[h]) for h in range(N_KV_HEADS)]
        outs, sums_p = _weighted_values(vt_ref, lo, hi, p, sums[nk])
        o_t = jnp.concatenate([outs[h] * (1.0 / (sums_p[h] + sink_p[h]))
                               for h in range(N_KV_HEADS)], axis=0)
        _store_gated(o_ref, g_ref, rows, o_t, w)


def _swa(q, g, k0, k1, v_t, sink, layer, shifted):
    b, l, _ = q.shape
    qspec = pl.BlockSpec((1, l, Q_WIDTH), lambda i: (i, 0, 0))
    return pl.pallas_call(
        functools.partial(_swa_kernel, l // SWA_WINDOW, layer, shifted),
        grid=(b,),
        in_specs=[pl.BlockSpec(memory_space=pltpu.SMEM), qspec, qspec] + _keys_values_specs(l),
        out_specs=qspec,
        out_shape=jax.ShapeDtypeStruct((b, l, Q_WIDTH), BF16),
        compiler_params=_cparams("parallel"),
        name="swa",
    )(sink, q, g, k0, k1, v_t)


def _moba_kernel(nblk, shifted, q_ref, qs_ref, g_ref, k0_ref, k1_ref, vt_ref, o_ref):
    blk = MOBA_BLOCK
    nq = N_Q_TILES * blk
    kmean = jnp.concatenate(
        [jnp.mean(k0_ref[0, n * blk:(n + 1) * blk, :].astype(F32)
                  + k1_ref[0, n * blk:(n + 1) * blk, :].astype(F32), axis=0, keepdims=True)
         for n in range(nblk)], axis=0)
    sum_row = _sum_row(blk)

    blk_id = lax.broadcasted_iota(jnp.int32, (nblk, blk), 0)
    ki = lax.broadcasted_iota(jnp.int32, (2 * blk, nq), 0) & (blk - 1)
    qi = lax.broadcasted_iota(jnp.int32, (2 * blk, nq), 1) & (blk - 1)
    causal = ki <= qi

    for i in range(nblk):
        rows = slice(i * blk, (i + 1) * blk)
        q_t = _queries_t(q_ref, rows)
        qsum = qs_ref[0, rows, :]

        past = blk_id < i
        sel = []
        for h in range(N_KV_HEADS):
            gate = lax.dot_general(_half_select(kmean, h), _half_select(qsum, h),
                                   (((1,), (1,)), ((), ())),
                                   precision=lax.Precision.HIGHEST,
                                   preferred_element_type=F32)
            gate = jnp.where(past, gate, -jnp.inf)
            rank = jnp.zeros((nblk, blk), jnp.int32)
            for n2 in range(nblk):
                other = gate[n2:n2 + 1, :]
                ahead = (other > gate) | ((other == gate) & (n2 < blk_id))
                rank = rank + ahead.astype(jnp.int32)
            chosen = jnp.where((rank < MOBA_TOPK) & past, 1.0, 0.0)
            sel.append(jnp.concatenate([chosen] * N_Q_TILES, axis=1))

        kst = _stacked_keys(k0_ref, k1_ref, i * blk, (i + 1) * blk)
        s = jnp.where(causal, jnp.dot(kst, q_t, preferred_element_type=F32), -jnp.inf)
        if shifted:
            m = [jnp.max(s[h * blk:(h + 1) * blk], axis=0, keepdims=True)
                 for h in range(N_KV_HEADS)]
            p = jnp.concatenate([jnp.exp2(s[h * blk:(h + 1) * blk] - m[h])
                                 for h in range(N_KV_HEADS)], axis=0).astype(BF16)
        else:
            p = jnp.exp2(s).astype(BF16)
        acc, den = _weighted_values(vt_ref, i * blk, (i + 1) * blk, p, sum_row)

        for n in range(i):
            kst = _stacked_keys(k0_ref, k1_ref, n * blk, (n + 1) * blk)
            ok = [sel[h][n:n + 1, :] > 0.5 for h in range(N_KV_HEADS)]
            s = jnp.dot(kst, q_t, preferred_element_type=F32)
            if shifted:
                alpha = []
                parts = []
                for h in range(N_KV_HEADS):
                    s_h = s[h * blk:(h + 1) * blk]
                    m_new = jnp.where(ok[h], jnp.maximum(m[h], jnp.max(s_h, axis=0, keepdims=True)),
                                      m[h])
                    alpha.append(jnp.exp2(m[h] - m_new))
                    parts.append(jnp.exp2(s_h - jnp.where(ok[h], m_new, jnp.inf)))
                    m[h] = m_new
                p = jnp.concatenate(parts, axis=0).astype(BF16)
                outs, sums_p = _weighted_values(vt_ref, n * blk, (n + 1) * blk, p, sum_row)
                acc = [acc[h] * alpha[h] + outs[h] for h in range(N_KV_HEADS)]
                den = [den[h] * alpha[h] + sums_p[h] for h in range(N_KV_HEADS)]
            else:
                p = jnp.exp2(s).astype(BF16)
                outs, sums_p = _weighted_values(vt_ref, n * blk, (n + 1) * blk, p, sum_row)
                acc = [acc[h] + jnp.where(ok[h], outs[h], 0.0) for h in range(N_KV_HEADS)]
                den = [den[h] + jnp.where(ok[h], sums_p[h], 0.0) for h in range(N_KV_HEADS)]

        o_t = jnp.concatenate([acc[h] * (1.0 / den[h]) for h in range(N_KV_HEADS)], axis=0)
        _store_gated(o_ref, g_ref, rows, o_t, blk)


def _moba(q, qsum, g, k0, k1, v_t, shifted):
    b, l, _ = q.shape
    qspec = pl.BlockSpec((1, l, Q_WIDTH), lambda i: (i, 0, 0))
    return pl.pallas_call(
        functools.partial(_moba_kernel, l // MOBA_BLOCK, shifted),
        grid=(b,),
        in_specs=[qspec, pl.BlockSpec((1, l, LANES), lambda i: (i, 0, 0)), qspec]
                 + _keys_values_specs(l),
        out_specs=qspec,
        out_shape=jax.ShapeDtypeStruct((b, l, Q_WIDTH), BF16),
        compiler_params=_cparams("parallel"),
        name="moba",
    )(q, qsum, g, k0, k1, v_t)


def _head_tiles(w):
    lead = w.shape[:-1]
    w = w.reshape(*lead, N_KV_HEADS, N_Q_TILES, HEAD_DIM)
    return jnp.swapaxes(w, -3, -2).reshape(*lead, Q_WIDTH)


def _permute_w_in(w):
    s, q, k = SSM_WIDTH, Q_WIDTH, KV_WIDTH
    bounds = [0]
    for n in (s, s, q, k, k, q, q, k, k, q):
        bounds.append(bounds[-1] + n)
    s_u, s_g, a_q, a_k, a_v, a_g, m_q, m_k, m_v, m_g = [
        w[..., bounds[i]:bounds[i + 1]] for i in range(10)]
    return jnp.concatenate(
        [s_u, s_g, _head_tiles(a_q), _head_tiles(a_g), _head_tiles(m_q), _head_tiles(m_g),
         a_k, a_v, m_k, m_v], axis=-1).astype(BF16)


def _permute_w_out(w):
    s, q = SSM_WIDTH, Q_WIDTH
    rows_t = lambda part: jnp.swapaxes(_head_tiles(jnp.swapaxes(part, -1, -2)), -1, -2)
    return jnp.concatenate([w[:, 0:s], rows_t(w[:, s:s + q]), rows_t(w[:, s + q:])],
                           axis=1).astype(BF16)


def _block_diag_in(bb):
    d, g, h, p = bb.shape
    eye = jnp.eye(g, dtype=bb.dtype)[None, :, None, :, None]
    return (bb[:, :, :, None, :] * eye).reshape(d, g * h, g * p)


def _block_diag_out(c):
    d, g, h, p = c.shape
    eye = jnp.eye(g, dtype=c.dtype)[None, :, None, :, None]
    return (jnp.swapaxes(c, -1, -2)[:, :, :, None, :] * eye).reshape(d, g * p, g * h)


def _logit_bound(q_gain, k_gain):
    return ((HEAD_DIM * ATTN_SCALE * LOG2E) * jnp.max(jnp.abs(q_gain), axis=-1)
            * jnp.max(jnp.abs(k_gain), axis=-1))


def kernel(x, norm_g, w_in, ssm_lam_re, ssm_lam_im, ssm_log_dt, ssm_b_re, ssm_b_im,
           ssm_c_re, ssm_c_im, ssm_d, ssm_glu_w, ssm_glu_b, swa_q_norm, swa_k_norm,
           swa_sink, moba_q_norm, moba_k_norm, w_out):
    b, l, d = x.shape
    depth = norm_g.shape[0]
    tiles = _tiles(l)
    a_re, a_im, bb_re, bb_im = _ssm_prep(ssm_lam_re, ssm_lam_im, ssm_log_dt, ssm_b_re, ssm_b_im)
    two = lambda v: jnp.concatenate([v, v], axis=-1).reshape(depth, 1, LANES).astype(F32)
    edge_params = {
        "norm_g": norm_g.reshape(depth, 1, d).astype(F32),
        "w_in": _permute_w_in(w_in),
        "w_out": _permute_w_out(w_out),
        "head_norms": [two(v) for v in (swa_q_norm, swa_k_norm, moba_q_norm, moba_k_norm)],
    }
    by_slab_cols = lambda m: m.reshape(depth, SSM_WIDTH, RE_SLABS, LANES)
    bmat = jnp.concatenate([by_slab_cols(_block_diag_in(bb_re)), by_slab_cols(_block_diag_in(bb_im))],
                           axis=-1).transpose(0, 2, 1, 3).astype(BF16)
    by_slab_rows = lambda m: m.reshape(depth, RE_SLABS, LANES, SSM_WIDTH)
    cmat = jnp.concatenate([by_slab_rows(_block_diag_out(ssm_c_re)),
                            -by_slab_rows(_block_diag_out(ssm_c_im))], axis=2).astype(BF16)
    ssm_params = (
        bmat, a_re.reshape(depth, 1, N_STATE), a_im.reshape(depth, 1, N_STATE), cmat,
        ssm_d.reshape(depth, 1, SSM_WIDTH).astype(F32), ssm_glu_w.astype(BF16),
        ssm_glu_b.reshape(depth, 1, SSM_WIDTH).astype(F32))
    sink = swa_sink.astype(F32)
    small = ((_logit_bound(swa_q_norm, swa_k_norm) <= LOGIT_BOUND)
             & (jnp.max(jnp.abs(sink), axis=-1) * LOG2E <= LOGIT_BOUND)
             & (_logit_bound(moba_q_norm, moba_k_norm) <= LOGIT_BOUND))

    def attention(layer, shifted, a_q, a_g, a_kv, m_q, m_qsum, m_g, m_kv, sink):
        return (_swa(a_q, a_g, *a_kv, sink, layer, shifted),
                _moba(m_q, m_qsum, m_g, *m_kv, shifted))

    proj = _layer_edge(x, None, edge_params, None, 0, tiles)
    for layer in range(depth):
        s_u, s_g, a_q, a_g, m_q, m_qsum, m_g, *kvs = proj
        a_kv, m_kv = tuple(kvs[0:3]), tuple(kvs[3:6])
        y_swa, y_moba = lax.cond(small[layer], functools.partial(attention, layer, False),
                                 functools.partial(attention, layer, True),
                                 a_q, a_g, a_kv, m_q, m_qsum, m_g, m_kv, sink)
        branch_outs = (_ssm(s_u, s_g, ssm_params, layer, tiles["scan_steps"]), y_swa, y_moba)
        nxt = layer + 1 if layer + 1 < depth else None
        x, *proj = _layer_edge(x, branch_outs, edge_params, layer, nxt, tiles)
    return x
```

```python
import functools
import math

import jax
import jax.numpy as jnp
from jax import lax
from jax.experimental import pallas as pl
from jax.experimental.pallas import tpu as pltpu

F32 = jnp.float32
BF16 = jnp.bfloat16

HEAD_DIM = 64
SSM_WIDTH = 256
SSM_GROUPS = 16
SSM_STATE = 64
N_STATE = SSM_GROUPS * SSM_STATE
N_Q_HEADS = 6
N_KV_HEADS = 2
Q_WIDTH = N_Q_HEADS * HEAD_DIM
KV_WIDTH = N_KV_HEADS * HEAD_DIM
SWA_WINDOW = 128
MOBA_BLOCK = 256
MOBA_TOPK = 3
RMS_EPS = 1e-6
ATTN_SCALE = HEAD_DIM ** -0.5
LOG2E = math.log2(math.e)
LANES = 128
N_Q_TILES = Q_WIDTH // LANES
VMEM_LIMIT = 56 * 1024 * 1024
SUM_ROWS = 16
PREP_ROWS = 256
OFF_U, OFF_SG = 0, SSM_WIDTH
OFF_AQ = 2 * SSM_WIDTH
OFF_AG = OFF_AQ + Q_WIDTH
OFF_MQ = OFF_AG + Q_WIDTH
OFF_MG = OFF_MQ + Q_WIDTH
OFF_AKV = OFF_MG + Q_WIDTH
OFF_MKV = OFF_AKV + 2 * KV_WIDTH
PROJ_CHUNK = 512
LOGIT_BOUND = 60.0
MASK_BIAS = -300.0


def _tiles(seq):
    return {"proj_rows": min(1024, seq), "fused_proj_rows": min(1024, seq), "scan_steps": min(64, seq)}


def _cparams(*sem):
    return pltpu.CompilerParams(dimension_semantics=sem, vmem_limit_bytes=VMEM_LIMIT)


def _silu(t):
    return t * jax.nn.sigmoid(t)


def _ssm_prep_kernel(lr_ref, li_ref, ldt_ref, br_ref, bi_ref,
                     are_ref, aim_ref, bbr_ref, bbi_ref):
    lr = lr_ref[...]
    li = li_ref[...]
    dt = jnp.exp(ldt_ref[...])
    mag = jnp.exp(lr * dt)
    a_re = mag * jnp.cos(li * dt)
    a_im = mag * jnp.sin(li * dt)
    den = lr * lr + li * li
    nr = a_re - 1.0
    ni = a_im
    cr = (nr * lr + ni * li) / den
    ci = (ni * lr - nr * li) / den
    br = br_ref[...]
    bi = bi_ref[...]
    are_ref[...] = a_re
    aim_ref[...] = a_im
    bbr_ref[...] = cr * br - ci * bi
    bbi_ref[...] = cr * bi + ci * br


def _ssm_prep(lam_re, lam_im, log_dt, b_re, b_im):
    d, g, p = lam_re.shape
    h = b_re.shape[-1]
    shp = (d * g, h, p)
    bc = lambda a: jnp.broadcast_to(a.reshape(d * g, 1, -1), shp).astype(F32)
    tr = lambda a: jnp.swapaxes(a, -1, -2).reshape(shp).astype(F32)
    outs = pl.pallas_call(
        _ssm_prep_kernel,
        out_shape=[jax.ShapeDtypeStruct(shp, F32)] * 4,
        name="ssm_prep",
    )(bc(lam_re), bc(lam_im), bc(log_dt[..., None]), tr(b_re), tr(b_im))
    a_re, a_im, bb_re, bb_im = outs
    return (a_re[:, 0, :].reshape(d, g, p), a_im[:, 0, :].reshape(d, g, p),
            bb_re.reshape(d, g, h, p), bb_im.reshape(d, g, h, p))


def _pair_rms_norm(t, gain2):
    lane = lax.broadcasted_iota(jnp.int32, t.shape, 1)
    lo = lane < HEAD_DIM
    sq = t * t
    s_lo = jnp.sum(jnp.where(lo, sq, 0.0), axis=-1, keepdims=True)
    s_hi = jnp.sum(jnp.where(lo, 0.0, sq), axis=-1, keepdims=True)
    r_lo = lax.rsqrt(s_lo * (1.0 / HEAD_DIM) + RMS_EPS)
    r_hi = lax.rsqrt(s_hi * (1.0 / HEAD_DIM) + RMS_EPS)
    return t * jnp.where(lo, r_lo, r_hi) * gain2


def _half_select(t, kv_head):
    lane = lax.broadcasted_iota(jnp.int32, t.shape, 1)
    keep = (lane < HEAD_DIM) if kv_head == 0 else (lane >= HEAD_DIM)
    return jnp.where(keep, t, 0.0)


def _project(x, g_ref, w_ref, aqn_ref, akn_ref, mqn_ref, mkn_ref,
             u_ref, sg_ref, aq_ref, ag_ref, mq_ref, mqs_ref, mg_ref, akv_ref, mkv_ref):
    def lanes(j):
        return slice(j * LANES, (j + 1) * LANES)

    half = x.shape[0] // 2
    for r in range(2):
        rs = slice(r * half, (r + 1) * half)
        xr = x[rs]
        ms = jnp.mean(xr * xr, axis=-1, keepdims=True)
        h = (xr * lax.rsqrt(ms + RMS_EPS) * g_ref[...]).astype(BF16)
        chunks = {}

        def tile(col):
            c, k = divmod(col, PROJ_CHUNK)
            if c not in chunks:
                chunks[c] = jnp.dot(h, w_ref[:, c * PROJ_CHUNK:(c + 1) * PROJ_CHUNK],
                                    preferred_element_type=F32)
            return chunks[c][:, k:k + LANES]

        for off, kn_ref, kv_ref in ((OFF_MKV, mkn_ref, mkv_ref), (OFF_AKV, akn_ref, akv_ref)):
            kv_ref[0, rs, lanes(0)] = _pair_rms_norm(tile(off), kn_ref[...]).astype(BF16)
            kv_ref[0, rs, lanes(1)] = tile(off + KV_WIDTH).astype(BF16)

        qsum = jnp.zeros((half, LANES), F32)
        for j in reversed(range(N_Q_TILES)):
            mg_ref[0, rs, lanes(j)] = _silu(tile(OFF_MG + j * LANES)).astype(BF16)
            mqn = _pair_rms_norm(tile(OFF_MQ + j * LANES), mqn_ref[...])
            qsum = qsum + mqn
            mq_ref[0, rs, lanes(j)] = (mqn * (ATTN_SCALE * LOG2E)).astype(BF16)
        mqs_ref[0, rs, :] = qsum
        for j in reversed(range(N_Q_TILES)):
            ag_ref[0, rs, lanes(j)] = _silu(tile(OFF_AG + j * LANES)).astype(BF16)
            aq_ref[0, rs, lanes(j)] = (_pair_rms_norm(tile(OFF_AQ + j * LANES), aqn_ref[...])
                                       * (ATTN_SCALE * LOG2E)).astype(BF16)

        for j in range(SSM_WIDTH // LANES):
            sg_ref[0, rs, lanes(j)] = _silu(tile(OFF_SG + j * LANES)).astype(BF16)
            u_ref[0, rs, lanes(j)] = tile(OFF_U + j * LANES).astype(BF16)


def _residual_out(x_ref, ys_ref, ya_ref, ym_ref, wo_ref):
    y = jnp.concatenate([ys_ref[0], ya_ref[0], ym_ref[0]], axis=1)
    return x_ref[0] + jnp.dot(y, wo_ref[...], preferred_element_type=F32)


def _inproj_kernel(x_ref, *refs):
    _project(x_ref[0], *refs)


def _outproj_kernel(x_ref, ys_ref, ya_ref, ym_ref, wo_ref, o_ref):
    o_ref[0] = _residual_out(x_ref, ys_ref, ya_ref, ym_ref, wo_ref)


def _outin_kernel(x_ref, ys_ref, ya_ref, ym_ref, wo_ref, *refs):
    proj_in, proj_out = refs[:6], refs[7:]
    x = _residual_out(x_ref, ys_ref, ya_ref, ym_ref, wo_ref)
    refs[6][0] = x
    _project(x, *proj_in, *proj_out)


def _layer_edge(x, branch_outs, params, out_layer, in_layer, tiles):
    b, l, d = x.shape
    fused = out_layer is not None and in_layer is not None
    tl = tiles["fused_proj_rows" if fused else "proj_rows"]
    row = lambda width: pl.BlockSpec((1, tl, width), lambda i, t: (i, t, 0))
    shape = lambda width, dtype: jax.ShapeDtypeStruct((b, l, width), dtype)

    def layer_block(a, layer):
        return pl.BlockSpec((None,) + a.shape[1:], lambda i, t: (layer,) + (0,) * (a.ndim - 1),
                            pipeline_mode=pl.Buffered(1))

    args, in_specs, out_specs, out_shape = [x], [row(d)], [], []
    if out_layer is not None:
        args += list(branch_outs) + [params["w_out"]]
        in_specs += [row(SSM_WIDTH), row(Q_WIDTH), row(Q_WIDTH),
                     layer_block(params["w_out"], out_layer)]
        out_specs.append(row(d))
        out_shape.append(shape(d, F32))
    if in_layer is not None:
        proj_params = [params["norm_g"], params["w_in"]] + params["head_norms"]
        args += proj_params
        in_specs += [layer_block(a, in_layer) for a in proj_params]
        out_specs += [row(SSM_WIDTH), row(SSM_WIDTH), row(Q_WIDTH), row(Q_WIDTH), row(Q_WIDTH),
                      row(LANES), row(Q_WIDTH), row(2 * KV_WIDTH), row(2 * KV_WIDTH)]
        out_shape += [shape(SSM_WIDTH, BF16), shape(SSM_WIDTH, BF16), shape(Q_WIDTH, BF16),
                      shape(Q_WIDTH, BF16), shape(Q_WIDTH, BF16), shape(LANES, F32),
                      shape(Q_WIDTH, BF16), shape(2 * KV_WIDTH, BF16), shape(2 * KV_WIDTH, BF16)]
    body, name = {(False, True): (_inproj_kernel, "inproj"),
                  (True, False): (_outproj_kernel, "outproj"),
                  (True, True): (_outin_kernel, "outin")}[(out_layer is not None, in_layer is not None)]
    outs = pl.pallas_call(
        body,
        grid=(b, l // tl),
        in_specs=in_specs,
        out_specs=out_specs,
        out_shape=out_shape,
        compiler_params=_cparams("parallel", "parallel"),
        name=name,
    )(*args)
    return outs


ROW_PAD = 4
N_SLABS = 2 * N_STATE // LANES
RE_SLABS = N_SLABS // 2
N_STAGES = 3


def _ssm_kernel(nb, tc, u_ref, u2_ref, sg2_ref, bmat_ref, are_ref, aim_ref, cmat_ref,
                d_ref, gw_ref, gb_ref, y_ref, upad_a, upad_c, ypad_s, st_s, *slots):
    pitch = tc + ROW_PAD
    i = pl.program_id(0)

    @pl.when(i == 0)
    def _():
        st_s[...] = jnp.zeros_like(st_s)
        upad_a[...] = jnp.zeros_like(upad_a)
        upad_c[...] = jnp.zeros_like(upad_c)
        for s in slots:
            s[...] = jnp.zeros_like(s)

    pieces = RE_SLABS
    steps = tc // pieces

    def stages(in_s, scan_s, out_s):
        for b in range(nb):
            upad_a[pl.ds(b * pitch, tc), :] = u_ref[b].astype(F32)
            upad_c[pl.ds(b * pitch, tc), :] = u2_ref[b].astype(F32)
        u_in = upad_a[...].astype(BF16)
        a_re = [jnp.broadcast_to(are_ref[:, j * LANES:(j + 1) * LANES], (nb, LANES))
                for j in range(RE_SLABS)]
        a_im = [jnp.broadcast_to(aim_ref[:, j * LANES:(j + 1) * LANES], (nb, LANES))
                for j in range(RE_SLABS)]
        state = [st_s[j] for j in range(N_SLABS)]
        y = d_ref[...] * upad_c[...]
        for k in range(pieces):
            bu = jnp.dot(u_in, bmat_ref[k], preferred_element_type=F32)
            in_s[k] = bu[:, 0:LANES]
            in_s[RE_SLABS + k] = bu[:, LANES:2 * LANES]
            for t in range(k * steps, (k + 1) * steps):
                rows = pl.ds(t, nb, stride=pitch)
                for j in range(RE_SLABS):
                    xr, xi = state[j], state[RE_SLABS + j]
                    nr = a_re[j] * xr - a_im[j] * xi + scan_s[j, rows, :]
                    ni = a_re[j] * xi + a_im[j] * xr + scan_s[RE_SLABS + j, rows, :]
                    scan_s[j, rows, :] = nr
                    scan_s[RE_SLABS + j, rows, :] = ni
                    state[j], state[RE_SLABS + j] = nr, ni
            xs = jnp.concatenate([out_s[k], out_s[RE_SLABS + k]], axis=1).astype(BF16)
            y = y + jnp.dot(xs, cmat_ref[k], preferred_element_type=F32)
        for j in range(N_SLABS):
            st_s[j] = state[j]
        y = jax.nn.gelu(y)
        z = jnp.dot(y.astype(BF16), gw_ref[...], preferred_element_type=F32) + gb_ref[...]
        ypad_s[...] = y * jax.nn.sigmoid(z)
        for b in range(nb):
            y_ref[b] = (ypad_s[pl.ds(b * pitch, tc), :] * sg2_ref[b].astype(F32)).astype(BF16)

    for r in range(N_STAGES):
        @pl.when(lax.rem(i, N_STAGES) == r)
        def _(r=r):
            stages(slots[r], slots[(r + 2) % N_STAGES], slots[(r + 1) % N_STAGES])


def _ssm(u, sg, ssm_params, layer, tc):
    nb, l, _ = u.shape
    rows = nb * (tc + ROW_PAD)
    n_chunks = l // tc
    chunk = lambda lag: pl.BlockSpec(
        (nb, tc, SSM_WIDTH), lambda i: (0, jnp.clip(i - lag, 0, n_chunks - 1), 0))
    full = lambda a: pl.BlockSpec((None,) + a.shape[1:], lambda i: (layer,) + (0,) * (a.ndim - 1))
    slab = pltpu.VMEM((N_SLABS, rows, LANES), F32)
    return pl.pallas_call(
        functools.partial(_ssm_kernel, nb, tc),
        grid=(n_chunks + N_STAGES - 1,),
        in_specs=[chunk(0), chunk(2), chunk(2)] + [full(a) for a in ssm_params],
        out_specs=chunk(2),
        out_shape=jax.ShapeDtypeStruct((nb, l, SSM_WIDTH), BF16),
        scratch_shapes=[pltpu.VMEM((rows, SSM_WIDTH), F32),
                        pltpu.VMEM((rows, SSM_WIDTH), F32),
                        pltpu.VMEM((rows, SSM_WIDTH), F32),
                        pltpu.VMEM((N_SLABS, nb, LANES), F32),
                        slab, slab, slab],
        compiler_params=_cparams("arbitrary"),
        name="ssm",
    )(u, u, sg, *ssm_params)


def _sum_row(n_keys):
    r = lax.broadcasted_iota(jnp.int32, (SUM_ROWS, n_keys), 0)
    return jnp.where(r == 0, 1.0, 0.0).astype(BF16)


def _prep_keys_values(kv_ref, ks_s, vs_s, seq):
    means = []
    for c in range(seq // PREP_ROWS):
        rows = slice(c * PREP_ROWS, (c + 1) * PREP_ROWS)
        kvb = kv_ref[0, rows, :].astype(F32)
        kf = kvb[:, 0:KV_WIDTH]
        means.append(jnp.mean(kf, axis=0, keepdims=True))
        ks_s[0, rows, :] = _half_select(kf, 0).astype(BF16)
        ks_s[1, rows, :] = _half_select(kf, 1).astype(BF16)
        vs_s[:, rows] = jnp.transpose(kvb[:, KV_WIDTH:2 * KV_WIDTH]).astype(BF16)
    return means


def _stacked_keys(ks_s, lo, hi):
    return jnp.concatenate([ks_s[0, lo:hi, :], ks_s[1, lo:hi, :]], axis=0)


def _weighted_values(vs_s, lo, hi, p, sum_row):
    nk = hi - lo
    outs, sums = [], []
    for h in range(N_KV_HEADS):
        v_t = jnp.concatenate([vs_s[h * HEAD_DIM:(h + 1) * HEAD_DIM, lo:hi], sum_row], axis=0)
        pv = jnp.dot(v_t, p[h * nk:(h + 1) * nk], preferred_element_type=F32)
        outs.append(pv[0:HEAD_DIM])
        sums.append(pv[HEAD_DIM:HEAD_DIM + 1])
    return outs, sums


def _queries_t(q_ref, rows):
    return jnp.concatenate(
        [jnp.transpose(q_ref[0, rows, j * LANES:(j + 1) * LANES]) for j in range(N_Q_TILES)], axis=1)


def _store_gated(o_ref, g_ref, rows, o_t, width):
    for j in range(N_Q_TILES):
        cols = slice(j * LANES, (j + 1) * LANES)
        o = jnp.transpose(o_t[:, j * width:(j + 1) * width]) * g_ref[0, rows, cols].astype(F32)
        o_ref[0, rows, cols] = o.astype(BF16)


_ATTN_SCRATCH = lambda seq: [pltpu.VMEM((N_KV_HEADS, seq, KV_WIDTH), BF16),
                             pltpu.VMEM((KV_WIDTH, seq), BF16)]


def _swa_kernel(nblk, layer, shifted, sink_ref, q_ref, g_ref, kv_ref, o_ref, ks_s, vs_s):
    w = SWA_WINDOW
    nq = N_Q_TILES * w
    _prep_keys_values(kv_ref, ks_s, vs_s, nblk * w)
    lane = lax.broadcasted_iota(jnp.int32, (1, nq), 1)
    sink2 = []
    for h in range(N_KV_HEADS):
        a = [sink_ref[layer, j + N_Q_TILES * h] * LOG2E for j in range(N_Q_TILES)]
        sink2.append(jnp.where(lane < w, a[0], jnp.where(lane < 2 * w, a[1], a[2])))

    def window_mask(n_keys):
        kpos = (lax.broadcasted_iota(jnp.int32, (2 * n_keys, nq), 0) & (n_keys - 1)) - (n_keys - w)
        qpos = lax.broadcasted_iota(jnp.int32, (2 * n_keys, nq), 1) & (w - 1)
        rel = qpos - kpos
        return (rel >= 0) & (rel < w)

    masks = {w: window_mask(w), 2 * w: window_mask(2 * w)}
    sums = {w: _sum_row(w), 2 * w: _sum_row(2 * w)}

    def scores(n):
        lo = max(n - 1, 0) * w
        hi = (n + 1) * w
        q_t = _queries_t(q_ref, slice(n * w, (n + 1) * w))
        return jnp.where(masks[hi - lo], jnp.dot(_stacked_keys(ks_s, lo, hi), q_t,
                                                 preferred_element_type=F32), -jnp.inf)

    s_next = scores(0)
    for n in range(nblk):
        rows = slice(n * w, (n + 1) * w)
        lo = max(n - 1, 0) * w
        hi = (n + 1) * w
        nk = hi - lo
        s = s_next
        if n + 1 < nblk:
            s_next = scores(n + 1)
        if shifted:
            m = [jnp.maximum(jnp.max(s[h * nk:(h + 1) * nk], axis=0, keepdims=True), sink2[h])
                 for h in range(N_KV_HEADS)]
            p = jnp.concatenate([jnp.exp2(s[h * nk:(h + 1) * nk] - m[h])
                                 for h in range(N_KV_HEADS)], axis=0).astype(BF16)
            sink_p = [jnp.exp2(sink2[h] - m[h]) for h in range(N_KV_HEADS)]
        else:
            p = jnp.exp2(s).astype(BF16)
            sink_p = [jnp.exp2(sink2[h]) for h in range(N_KV_HEADS)]
        outs, sums_p = _weighted_values(vs_s, lo, hi, p, sums[nk])
        o_t = jnp.concatenate([outs[h] * (1.0 / (sums_p[h] + sink_p[h]))
                               for h in range(N_KV_HEADS)], axis=0)
        _store_gated(o_ref, g_ref, rows, o_t, w)


def _swa(q, g, kv, sink, layer, shifted):
    b, l, _ = q.shape
    qspec = pl.BlockSpec((1, l, Q_WIDTH), lambda i: (i, 0, 0))
    return pl.pallas_call(
        functools.partial(_swa_kernel, l // SWA_WINDOW, layer, shifted),
        grid=(b,),
        in_specs=[pl.BlockSpec(memory_space=pltpu.SMEM),
                  qspec, qspec,
                  pl.BlockSpec((1, l, 2 * KV_WIDTH), lambda i: (i, 0, 0))],
        out_specs=qspec,
        out_shape=jax.ShapeDtypeStruct((b, l, Q_WIDTH), BF16),
        scratch_shapes=_ATTN_SCRATCH(l),
        compiler_params=_cparams("parallel"),
        name="swa",
    )(sink, q, g, kv)


def _moba_kernel(nblk, shifted, q_ref, qs_ref, g_ref, kv_ref, o_ref, ks_s, vs_s):
    blk = MOBA_BLOCK
    assert blk == PREP_ROWS
    nq = N_Q_TILES * blk
    kmean = jnp.concatenate(_prep_keys_values(kv_ref, ks_s, vs_s, nblk * blk), axis=0)
    sum_row = _sum_row(blk)
    if not shifted:
        r = lax.broadcasted_iota(jnp.int32, (2 * blk, LANES), 0)
        c = lax.broadcasted_iota(jnp.int32, (2 * blk, LANES), 1)
        head_cols = jnp.where(((c == 0) & (r < blk)) | ((c == 1) & (r >= blk)), 1.0, 0.0).astype(BF16)
        bias_row = lax.broadcasted_iota(jnp.int32, (SUM_ROWS, nq), 0)
        bias_pad = jnp.zeros((LANES - SUM_ROWS, nq), BF16)

    blk_id = lax.broadcasted_iota(jnp.int32, (nblk, blk), 0)
    ki = lax.broadcasted_iota(jnp.int32, (2 * blk, nq), 0) & (blk - 1)
    qi = lax.broadcasted_iota(jnp.int32, (2 * blk, nq), 1) & (blk - 1)
    causal = ki <= qi

    for i in range(nblk):
        rows = slice(i * blk, (i + 1) * blk)
        q_t = _queries_t(q_ref, rows)
        qsum = qs_ref[0, rows, :]

        past = blk_id < i
        sel = []
        for h in range(N_KV_HEADS):
            gate = lax.dot_general(_half_select(kmean, h), _half_select(qsum, h),
                                   (((1,), (1,)), ((), ())),
                                   precision=lax.Precision.HIGHEST,
                                   preferred_element_type=F32)
            gate = jnp.where(past, gate, -jnp.inf)
            rank = jnp.zeros((nblk, blk), jnp.int32)
            for n2 in range(nblk):
                other = gate[n2:n2 + 1, :]
                ahead = (other > gate) | ((other == gate) & (n2 < blk_id))
                rank = rank + ahead.astype(jnp.int32)
            chosen = jnp.where((rank < MOBA_TOPK) & past, 1.0, 0.0)
            sel.append(jnp.concatenate([chosen] * N_Q_TILES, axis=1))

        def past_scores(n):
            kst = _stacked_keys(ks_s, n * blk, (n + 1) * blk)
            if shifted:
                return jnp.dot(kst, q_t, preferred_element_type=F32)
            ok = [sel[h][n:n + 1, :] > 0.5 for h in range(N_KV_HEADS)]
            bias = jnp.where(bias_row == 0, jnp.where(ok[0], 0.0, MASK_BIAS),
                             jnp.where(bias_row == 1, jnp.where(ok[1], 0.0, MASK_BIAS), 0.0))
            q_ext = jnp.concatenate([q_t, bias.astype(BF16), bias_pad], axis=0)
            k_ext = jnp.concatenate([kst, head_cols], axis=1)
            return jnp.dot(k_ext, q_ext, preferred_element_type=F32)

        kst = _stacked_keys(ks_s, i * blk, (i + 1) * blk)
        s = jnp.where(causal, jnp.dot(kst, q_t, preferred_element_type=F32), -jnp.inf)
        s_next = past_scores(0) if i > 0 else None
        if shifted:
            m = [jnp.max(s[h * blk:(h + 1) * blk], axis=0, keepdims=True)
                 for h in range(N_KV_HEADS)]
            p = jnp.concatenate([jnp.exp2(s[h * blk:(h + 1) * blk] - m[h])
                                 for h in range(N_KV_HEADS)], axis=0).astype(BF16)
        else:
            p = jnp.exp2(s).astype(BF16)
        acc, den = _weighted_values(vs_s, i * blk, (i + 1) * blk, p, sum_row)

        for n in range(i):
            s = s_next
            if n + 1 < i:
                s_next = past_scores(n + 1)
            if shifted:
                ok = [sel[h][n:n + 1, :] > 0.5 for h in range(N_KV_HEADS)]
                alpha = []
                parts = []
                for h in range(N_KV_HEADS):
                    s_h = s[h * blk:(h + 1) * blk]
                    m_new = jnp.where(ok[h], jnp.maximum(m[h], jnp.max(s_h, axis=0, keepdims=True)),
                                      m[h])
                    alpha.append(jnp.exp2(m[h] - m_new))
                    parts.append(jnp.exp2(s_h - jnp.where(ok[h], m_new, jnp.inf)))
                    m[h] = m_new
                p = jnp.concatenate(parts, axis=0).astype(BF16)
                outs, sums_p = _weighted_values(vs_s, n * blk, (n + 1) * blk, p, sum_row)
                acc = [acc[h] * alpha[h] + outs[h] for h in range(N_KV_HEADS)]
                den = [den[h] * alpha[h] + sums_p[h] for h in range(N_KV_HEADS)]
            else:
                p = jnp.exp2(s).astype(BF16)
                outs, sums_p = _weighted_values(vs_s, n * blk, (n + 1) * blk, p, sum_row)
                acc = [acc[h] + outs[h] for h in range(N_KV_HEADS)]
                den = [den[h] + sums_p[h] for h in range(N_KV_HEADS)]

        o_t = jnp.concatenate([acc[h] * (1.0 / den[h]) for h in range(N_KV_HEADS)], axis=0)
        _store_gated(o_ref, g_ref, rows, o_t, blk)


def _moba(q, qsum, g, kv, shifted):
    b, l, _ = q.shape
    qspec = pl.BlockSpec((1, l, Q_WIDTH), lambda i: (i, 0, 0))
    return pl.pallas_call(
        functools.partial(_moba_kernel, l // MOBA_BLOCK, shifted),
        grid=(b,),
        in_specs=[qspec,
                  pl.BlockSpec((1, l, LANES), lambda i: (i, 0, 0)),
                  qspec,
                  pl.BlockSpec((1, l, 2 * KV_WIDTH), lambda i: (i, 0, 0))],
        out_specs=qspec,
        out_shape=jax.ShapeDtypeStruct((b, l, Q_WIDTH), BF16),
        scratch_shapes=_ATTN_SCRATCH(l),
        compiler_params=_cparams("parallel"),
        name="moba",
    )(q, qsum, g, kv)


def _head_tiles(w):
    lead = w.shape[:-1]
    w = w.reshape(*lead, N_KV_HEADS, N_Q_TILES, HEAD_DIM)
    return jnp.swapaxes(w, -3, -2).reshape(*lead, Q_WIDTH)


def _permute_w_in(w):
    s, q, k = SSM_WIDTH, Q_WIDTH, KV_WIDTH
    bounds = [0]
    for n in (s, s, q, k, k, q, q, k, k, q):
        bounds.append(bounds[-1] + n)
    s_u, s_g, a_q, a_k, a_v, a_g, m_q, m_k, m_v, m_g = [
        w[..., bounds[i]:bounds[i + 1]] for i in range(10)]
    return jnp.concatenate(
        [s_u, s_g, _head_tiles(a_q), _head_tiles(a_g), _head_tiles(m_q), _head_tiles(m_g),
         a_k, a_v, m_k, m_v], axis=-1).astype(BF16)


def _permute_w_out(w):
    s, q = SSM_WIDTH, Q_WIDTH
    rows_t = lambda part: jnp.swapaxes(_head_tiles(jnp.swapaxes(part, -1, -2)), -1, -2)
    return jnp.concatenate([w[:, 0:s], rows_t(w[:, s:s + q]), rows_t(w[:, s + q:])],
                           axis=1).astype(BF16)


def _block_diag_in(bb):
    d, g, h, p = bb.shape
    eye = jnp.eye(g, dtype=bb.dtype)[None, :, None, :, None]
    return (bb[:, :, :, None, :] * eye).reshape(d, g * h, g * p)


def _block_diag_out(c):
    d, g, h, p = c.shape
    eye = jnp.eye(g, dtype=c.dtype)[None, :, None, :, None]
    return (jnp.swapaxes(c, -1, -2)[:, :, :, None, :] * eye).reshape(d, g * p, g * h)


def _logit_bound(q_gain, k_gain):
    return ((HEAD_DIM * ATTN_SCALE * LOG2E) * jnp.max(jnp.abs(q_gain), axis=-1)
            * jnp.max(jnp.abs(k_gain), axis=-1))


def kernel(x, norm_g, w_in, ssm_lam_re, ssm_lam_im, ssm_log_dt, ssm_b_re, ssm_b_im,
           ssm_c_re, ssm_c_im, ssm_d, ssm_glu_w, ssm_glu_b, swa_q_norm, swa_k_norm,
           swa_sink, moba_q_norm, moba_k_norm, w_out):
    b, l, d = x.shape
    depth = norm_g.shape[0]
    tiles = _tiles(l)
    a_re, a_im, bb_re, bb_im = _ssm_prep(ssm_lam_re, ssm_lam_im, ssm_log_dt, ssm_b_re, ssm_b_im)
    two = lambda v: jnp.concatenate([v, v], axis=-1).reshape(depth, 1, LANES).astype(F32)
    edge_params = {
        "norm_g": norm_g.reshape(depth, 1, d).astype(F32),
        "w_in": _permute_w_in(w_in),
        "w_out": _permute_w_out(w_out),
        "head_norms": [two(v) for v in (swa_q_norm, swa_k_norm, moba_q_norm, moba_k_norm)],
    }
    by_slab_cols = lambda m: m.reshape(depth, SSM_WIDTH, RE_SLABS, LANES)
    bmat = jnp.concatenate([by_slab_cols(_block_diag_in(bb_re)), by_slab_cols(_block_diag_in(bb_im))],
                           axis=-1).transpose(0, 2, 1, 3).astype(BF16)
    by_slab_rows = lambda m: m.reshape(depth, RE_SLABS, LANES, SSM_WIDTH)
    cmat = jnp.concatenate([by_slab_rows(_block_diag_out(ssm_c_re)),
                            -by_slab_rows(_block_diag_out(ssm_c_im))], axis=2).astype(BF16)
    ssm_params = (
        bmat, a_re.reshape(depth, 1, N_STATE), a_im.reshape(depth, 1, N_STATE), cmat,
        ssm_d.reshape(depth, 1, SSM_WIDTH).astype(F32), ssm_glu_w.astype(BF16),
        ssm_glu_b.reshape(depth, 1, SSM_WIDTH).astype(F32))
    sink = swa_sink.astype(F32)
    small = ((_logit_bound(swa_q_norm, swa_k_norm) <= LOGIT_BOUND)
             & (jnp.max(jnp.abs(sink), axis=-1) * LOG2E <= LOGIT_BOUND)
             & (_logit_bound(moba_q_norm, moba_k_norm) <= LOGIT_BOUND))

    def attention(layer, shifted, a_q, a_g, a_kv, m_q, m_qsum, m_g, m_kv, sink):
        return (_swa(a_q, a_g, a_kv, sink, layer, shifted), _moba(m_q, m_qsum, m_g, m_kv, shifted))

    proj = _layer_edge(x, None, edge_params, None, 0, tiles)
    for layer in range(depth):
        s_u, s_g, a_q, a_g, m_q, m_qsum, m_g, a_kv, m_kv = proj
        y_swa, y_moba = lax.cond(small[layer], functools.partial(attention, layer, False),
                                 functools.partial(attention, layer, True),
                                 a_q, a_g, a_kv, m_q, m_qsum, m_g, m_kv, sink)
        branch_outs = (_ssm(s_u, s_g, ssm_params, layer, tiles["scan_steps"]), y_swa, y_moba)
        nxt = layer + 1 if layer + 1 < depth else None
        x, *proj = _layer_edge(x, branch_outs, edge_params, layer, nxt, tiles)
    return x
```

```python
import functools
import math

import jax
import jax.numpy as jnp
from jax import lax
from jax.experimental import pallas as pl
from jax.experimental.pallas import tpu as pltpu

F32 = jnp.float32
BF16 = jnp.bfloat16

HEAD_DIM = 64
SSM_WIDTH = 256
SSM_GROUPS = 16
SSM_STATE = 64
N_STATE = SSM_GROUPS * SSM_STATE
N_Q_HEADS = 6
N_KV_HEADS = 2
Q_WIDTH = N_Q_HEADS * HEAD_DIM
KV_WIDTH = N_KV_HEADS * HEAD_DIM
SWA_WINDOW = 128
MOBA_BLOCK = 256
MOBA_TOPK = 3
RMS_EPS = 1e-6
ATTN_SCALE = HEAD_DIM ** -0.5
LOG2E = math.log2(math.e)
LANES = 128
N_Q_TILES = Q_WIDTH // LANES
VMEM_LIMIT = 56 * 1024 * 1024
SUM_ROWS = 16
PREP_ROWS = 256
OFF_U, OFF_SG = 0, SSM_WIDTH
OFF_AQ = 2 * SSM_WIDTH
OFF_AG = OFF_AQ + Q_WIDTH
OFF_MQ = OFF_AG + Q_WIDTH
OFF_MG = OFF_MQ + Q_WIDTH
OFF_AKV = OFF_MG + Q_WIDTH
OFF_MKV = OFF_AKV + 2 * KV_WIDTH
PROJ_CHUNK = 512
LOGIT_BOUND = 60.0
MASK_BIAS = -300.0


def _tiles(seq):
    return {"proj_rows": min(1024, seq), "fused_proj_rows": min(1024, seq), "scan_steps": min(64, seq)}


def _cparams(*sem):
    return pltpu.CompilerParams(dimension_semantics=sem, vmem_limit_bytes=VMEM_LIMIT)


def _silu(t):
    return t * jax.nn.sigmoid(t)


def _ssm_prep_kernel(lr_ref, li_ref, ldt_ref, br_ref, bi_ref,
                     are_ref, aim_ref, bbr_ref, bbi_ref):
    lr = lr_ref[...]
    li = li_ref[...]
    dt = jnp.exp(ldt_ref[...])
    mag = jnp.exp(lr * dt)
    a_re = mag * jnp.cos(li * dt)
    a_im = mag * jnp.sin(li * dt)
    den = lr * lr + li * li
    nr = a_re - 1.0
    ni = a_im
    cr = (nr * lr + ni * li) / den
    ci = (ni * lr - nr * li) / den
    br = br_ref[...]
    bi = bi_ref[...]
    are_ref[...] = a_re
    aim_ref[...] = a_im
    bbr_ref[...] = cr * br - ci * bi
    bbi_ref[...] = cr * bi + ci * br


def _ssm_prep(lam_re, lam_im, log_dt, b_re, b_im):
    d, g, p = lam_re.shape
    h = b_re.shape[-1]
    shp = (d * g, h, p)
    bc = lambda a: jnp.broadcast_to(a.reshape(d * g, 1, -1), shp).astype(F32)
    tr = lambda a: jnp.swapaxes(a, -1, -2).reshape(shp).astype(F32)
    outs = pl.pallas_call(
        _ssm_prep_kernel,
        out_shape=[jax.ShapeDtypeStruct(shp, F32)] * 4,
        name="ssm_prep",
    )(bc(lam_re), bc(lam_im), bc(log_dt[..., None]), tr(b_re), tr(b_im))
    a_re, a_im, bb_re, bb_im = outs
    return (a_re[:, 0, :].reshape(d, g, p), a_im[:, 0, :].reshape(d, g, p),
            bb_re.reshape(d, g, h, p), bb_im.reshape(d, g, h, p))


def _pair_rms_norm(t, gain2):
    lane = lax.broadcasted_iota(jnp.int32, t.shape, 1)
    lo = lane < HEAD_DIM
    sq = t * t
    s_lo = jnp.sum(jnp.where(lo, sq, 0.0), axis=-1, keepdims=True)
    s_hi = jnp.sum(jnp.where(lo, 0.0, sq), axis=-1, keepdims=True)
    r_lo = lax.rsqrt(s_lo * (1.0 / HEAD_DIM) + RMS_EPS)
    r_hi = lax.rsqrt(s_hi * (1.0 / HEAD_DIM) + RMS_EPS)
    return t * jnp.where(lo, r_lo, r_hi) * gain2


def _half_select(t, kv_head):
    lane = lax.broadcasted_iota(jnp.int32, t.shape, 1)
    keep = (lane < HEAD_DIM) if kv_head == 0 else (lane >= HEAD_DIM)
    return jnp.where(keep, t, 0.0)


def _project(x, g_ref, w_ref, aqn_ref, akn_ref, mqn_ref, mkn_ref,
             u_ref, sg_ref, aq_ref, ag_ref, mq_ref, mqs_ref, mg_ref, akv_ref, mkv_ref):
    def lanes(j):
        return slice(j * LANES, (j + 1) * LANES)

    half = x.shape[0] // 2
    for r in range(2):
        rs = slice(r * half, (r + 1) * half)
        xr = x[rs]
        ms = jnp.mean(xr * xr, axis=-1, keepdims=True)
        h = (xr * lax.rsqrt(ms + RMS_EPS) * g_ref[...]).astype(BF16)
        chunks = {}

        def tile(col):
            c, k = divmod(col, PROJ_CHUNK)
            if c not in chunks:
                chunks[c] = jnp.dot(h, w_ref[:, c * PROJ_CHUNK:(c + 1) * PROJ_CHUNK],
                                    preferred_element_type=F32)
            return chunks[c][:, k:k + LANES]

        for off, kn_ref, kv_ref in ((OFF_MKV, mkn_ref, mkv_ref), (OFF_AKV, akn_ref, akv_ref)):
            kv_ref[0, rs, lanes(0)] = _pair_rms_norm(tile(off), kn_ref[...]).astype(BF16)
            kv_ref[0, rs, lanes(1)] = tile(off + KV_WIDTH).astype(BF16)

        qsum = jnp.zeros((half, LANES), F32)
        for j in reversed(range(N_Q_TILES)):
            mg_ref[0, rs, lanes(j)] = _silu(tile(OFF_MG + j * LANES)).astype(BF16)
            mqn = _pair_rms_norm(tile(OFF_MQ + j * LANES), mqn_ref[...])
            qsum = qsum + mqn
            mq_ref[0, rs, lanes(j)] = (mqn * (ATTN_SCALE * LOG2E)).astype(BF16)
        mqs_ref[0, rs, :] = qsum
        for j in reversed(range(N_Q_TILES)):
            ag_ref[0, rs, lanes(j)] = _silu(tile(OFF_AG + j * LANES)).astype(BF16)
            aq_ref[0, rs, lanes(j)] = (_pair_rms_norm(tile(OFF_AQ + j * LANES), aqn_ref[...])
                                       * (ATTN_SCALE * LOG2E)).astype(BF16)

        for j in range(SSM_WIDTH // LANES):
            sg_ref[0, rs, lanes(j)] = _silu(tile(OFF_SG + j * LANES)).astype(BF16)
            u_ref[0, rs, lanes(j)] = tile(OFF_U + j * LANES).astype(BF16)


def _residual_out(x_ref, ys_ref, ya_ref, ym_ref, wo_ref):
    y = jnp.concatenate([ys_ref[0], ya_ref[0], ym_ref[0]], axis=1)
    return x_ref[0] + jnp.dot(y, wo_ref[...], preferred_element_type=F32)


def _inproj_kernel(x_ref, *refs):
    _project(x_ref[0], *refs)


def _outproj_kernel(x_ref, ys_ref, ya_ref, ym_ref, wo_ref, o_ref):
    o_ref[0] = _residual_out(x_ref, ys_ref, ya_ref, ym_ref, wo_ref)


def _outin_kernel(x_ref, ys_ref, ya_ref, ym_ref, wo_ref, *refs):
    proj_in, proj_out = refs[:6], refs[7:]
    x = _residual_out(x_ref, ys_ref, ya_ref, ym_ref, wo_ref)
    refs[6][0] = x
    _project(x, *proj_in, *proj_out)


def _layer_edge(x, branch_outs, params, out_layer, in_layer, tiles):
    b, l, d = x.shape
    fused = out_layer is not None and in_layer is not None
    tl = tiles["fused_proj_rows" if fused else "proj_rows"]
    row = lambda width: pl.BlockSpec((1, tl, width), lambda i, t: (i, t, 0))
    shape = lambda width, dtype: jax.ShapeDtypeStruct((b, l, width), dtype)

    def layer_block(a, layer):
        return pl.BlockSpec((None,) + a.shape[1:], lambda i, t: (layer,) + (0,) * (a.ndim - 1),
                            pipeline_mode=pl.Buffered(1))

    args, in_specs, out_specs, out_shape = [x], [row(d)], [], []
    if out_layer is not None:
        args += list(branch_outs) + [params["w_out"]]
        in_specs += [row(SSM_WIDTH), row(Q_WIDTH), row(Q_WIDTH),
                     layer_block(params["w_out"], out_layer)]
        out_specs.append(row(d))
        out_shape.append(shape(d, F32))
    if in_layer is not None:
        proj_params = [params["norm_g"], params["w_in"]] + params["head_norms"]
        args += proj_params
        in_specs += [layer_block(a, in_layer) for a in proj_params]
        out_specs += [row(SSM_WIDTH), row(SSM_WIDTH), row(Q_WIDTH), row(Q_WIDTH), row(Q_WIDTH),
                      row(LANES), row(Q_WIDTH), row(2 * KV_WIDTH), row(2 * KV_WIDTH)]
        out_shape += [shape(SSM_WIDTH, BF16), shape(SSM_WIDTH, BF16), shape(Q_WIDTH, BF16),
                      shape(Q_WIDTH, BF16), shape(Q_WIDTH, BF16), shape(LANES, F32),
                      shape(Q_WIDTH, BF16), shape(2 * KV_WIDTH, BF16), shape(2 * KV_WIDTH, BF16)]
    body, name = {(False, True): (_inproj_kernel, "inproj"),
                  (True, False): (_outproj_kernel, "outproj"),
                  (True, True): (_outin_kernel, "outin")}[(out_layer is not None, in_layer is not None)]
    outs = pl.pallas_call(
        body,
        grid=(b, l // tl),
        in_specs=in_specs,
        out_specs=out_specs,
        out_shape=out_shape,
        compiler_params=_cparams("parallel", "parallel"),
        name=name,
    )(*args)
    return outs


ROW_PAD = 4
N_SLABS = 2 * N_STATE // LANES
RE_SLABS = N_SLABS // 2
N_STAGES = 3


def _ssm_kernel(nb, tc, u_ref, u2_ref, sg2_ref, bmat_ref, are_ref, aim_ref, cmat_ref,
                d_ref, gw_ref, gb_ref, y_ref, upad_a, upad_c, ypad_s, st_s, *slots):
    pitch = tc + ROW_PAD
    i = pl.program_id(0)

    @pl.when(i == 0)
    def _():
        st_s[...] = jnp.zeros_like(st_s)
        upad_a[...] = jnp.zeros_like(upad_a)
        upad_c[...] = jnp.zeros_like(upad_c)
        for s in slots:
            s[...] = jnp.zeros_like(s)

    pieces = RE_SLABS
    steps = tc // pieces

    def stages(in_s, scan_s, out_s):
        for b in range(nb):
            upad_a[pl.ds(b * pitch, tc), :] = u_ref[b].astype(F32)
            upad_c[pl.ds(b * pitch, tc), :] = u2_ref[b].astype(F32)
        u_in = upad_a[...].astype(BF16)
        a_re = [jnp.broadcast_to(are_ref[:, j * LANES:(j + 1) * LANES], (nb, LANES))
                for j in range(RE_SLABS)]
        a_im = [jnp.broadcast_to(aim_ref[:, j * LANES:(j + 1) * LANES], (nb, LANES))
                for j in range(RE_SLABS)]
        state = [st_s[j] for j in range(N_SLABS)]
        y = d_ref[...] * upad_c[...]
        for k in range(pieces):
            bu = jnp.dot(u_in, bmat_ref[k], preferred_element_type=F32)
            in_s[k] = bu[:, 0:LANES]
            in_s[RE_SLABS + k] = bu[:, LANES:2 * LANES]
            for t in range(k * steps, (k + 1) * steps):
                rows = pl.ds(t, nb, stride=pitch)
                for j in range(RE_SLABS):
                    xr, xi = state[j], state[RE_SLABS + j]
                    nr = a_re[j] * xr - a_im[j] * xi + scan_s[j, rows, :]
                    ni = a_re[j] * xi + a_im[j] * xr + scan_s[RE_SLABS + j, rows, :]
                    scan_s[j, rows, :] = nr
                    scan_s[RE_SLABS + j, rows, :] = ni
                    state[j], state[RE_SLABS + j] = nr, ni
            xs = jnp.concatenate([out_s[k], out_s[RE_SLABS + k]], axis=1).astype(BF16)
            y = y + jnp.dot(xs, cmat_ref[k], preferred_element_type=F32)
        for j in range(N_SLABS):
            st_s[j] = state[j]
        y = jax.nn.gelu(y)
        z = jnp.dot(y.astype(BF16), gw_ref[...], preferred_element_type=F32) + gb_ref[...]
        ypad_s[...] = y * jax.nn.sigmoid(z)
        for b in range(nb):
            y_ref[b] = (ypad_s[pl.ds(b * pitch, tc), :] * sg2_ref[b].astype(F32)).astype(BF16)

    for r in range(N_STAGES):
        @pl.when(lax.rem(i, N_STAGES) == r)
        def _(r=r):
            stages(slots[r], slots[(r + 2) % N_STAGES], slots[(r + 1) % N_STAGES])


def _ssm(u, sg, ssm_params, layer, tc):
    nb, l, _ = u.shape
    rows = nb * (tc + ROW_PAD)
    n_chunks = l // tc
    chunk = lambda lag: pl.BlockSpec(
        (nb, tc, SSM_WIDTH), lambda i: (0, jnp.clip(i - lag, 0, n_chunks - 1), 0))
    full = lambda a: pl.BlockSpec((None,) + a.shape[1:], lambda i: (layer,) + (0,) * (a.ndim - 1))
    slab = pltpu.VMEM((N_SLABS, rows, LANES), F32)
    return pl.pallas_call(
        functools.partial(_ssm_kernel, nb, tc),
        grid=(n_chunks + N_STAGES - 1,),
        in_specs=[chunk(0), chunk(2), chunk(2)] + [full(a) for a in ssm_params],
        out_specs=chunk(2),
        out_shape=jax.ShapeDtypeStruct((nb, l, SSM_WIDTH), BF16),
        scratch_shapes=[pltpu.VMEM((rows, SSM_WIDTH), F32),
                        pltpu.VMEM((rows, SSM_WIDTH), F32),
                        pltpu.VMEM((rows, SSM_WIDTH), F32),
                        pltpu.VMEM((N_SLABS, nb, LANES), F32),
                        slab, slab, slab],
        compiler_params=_cparams("arbitrary"),
        name="ssm",
    )(u, u, sg, *ssm_params)


def _sum_row(n_keys):
    r = lax.broadcasted_iota(jnp.int32, (SUM_ROWS, n_keys), 0)
    return jnp.where(r == 0, 1.0, 0.0).astype(BF16)


def _prep_keys_values(kv_ref, ks_s, vs_s, seq):
    means = []
    for c in range(seq // PREP_ROWS):
        rows = slice(c * PREP_ROWS, (c + 1) * PREP_ROWS)
        kvb = kv_ref[0, rows, :].astype(F32)
        kf = kvb[:, 0:KV_WIDTH]
        means.append(jnp.mean(kf, axis=0, keepdims=True))
        ks_s[0, rows, :] = _half_select(kf, 0).astype(BF16)
        ks_s[1, rows, :] = _half_select(kf, 1).astype(BF16)
        vs_s[:, rows] = jnp.transpose(kvb[:, KV_WIDTH:2 * KV_WIDTH]).astype(BF16)
    return means


def _stacked_keys(ks_s, lo, hi):
    return jnp.concatenate([ks_s[0, lo:hi, :], ks_s[1, lo:hi, :]], axis=0)


def _weighted_values(vs_s, lo, hi, p, sum_row):
    nk = hi - lo
    outs, sums = [], []
    for h in range(N_KV_HEADS):
        v_t = jnp.concatenate([vs_s[h * HEAD_DIM:(h + 1) * HEAD_DIM, lo:hi], sum_row], axis=0)
        pv = jnp.dot(v_t, p[h * nk:(h + 1) * nk], preferred_element_type=F32)
        outs.append(pv[0:HEAD_DIM])
        sums.append(pv[HEAD_DIM:HEAD_DIM + 1])
    return outs, sums


def _queries_t(q_ref, rows):
    return jnp.concatenate(
        [jnp.transpose(q_ref[0, rows, j * LANES:(j + 1) * LANES]) for j in range(N_Q_TILES)], axis=1)


def _store_gated(o_ref, g_ref, rows, o_t, width):
    for j in range(N_Q_TILES):
        cols = slice(j * LANES, (j + 1) * LANES)
        o = jnp.transpose(o_t[:, j * width:(j + 1) * width]) * g_ref[0, rows, cols].astype(F32)
        o_ref[0, rows, cols] = o.astype(BF16)


_ATTN_SCRATCH = lambda seq: [pltpu.VMEM((N_KV_HEADS, seq, KV_WIDTH), BF16),
                             pltpu.VMEM((KV_WIDTH, seq), BF16)]


def _swa_kernel(nblk, layer, shifted, sink_ref, q_ref, g_ref, kv_ref, o_ref, ks_s, vs_s):
    w = SWA_WINDOW
    nq = N_Q_TILES * w
    _prep_keys_values(kv_ref, ks_s, vs_s, nblk * w)
    lane = lax.broadcasted_iota(jnp.int32, (1, nq), 1)
    sink2 = []
    for h in range(N_KV_HEADS):
        a = [sink_ref[layer, j + N_Q_TILES * h] * LOG2E for j in range(N_Q_TILES)]
        sink2.append(jnp.where(lane < w, a[0], jnp.where(lane < 2 * w, a[1], a[2])))

    def window_mask(n_keys):
        kpos = (lax.broadcasted_iota(jnp.int32, (2 * n_keys, nq), 0) & (n_keys - 1)) - (n_keys - w)
        qpos = lax.broadcasted_iota(jnp.int32, (2 * n_keys, nq), 1) & (w - 1)
        rel = qpos - kpos
        return (rel >= 0) & (rel < w)

    masks = {w: window_mask(w), 2 * w: window_mask(2 * w)}
    sums = {w: _sum_row(w), 2 * w: _sum_row(2 * w)}

    def scores(n):
        lo = max(n - 1, 0) * w
        hi = (n + 1) * w
        q_t = _queries_t(q_ref, slice(n * w, (n + 1) * w))
        return jnp.where(masks[hi - lo], jnp.dot(_stacked_keys(ks_s, lo, hi), q_t,
                                                 preferred_element_type=F32), -jnp.inf)

    s_next = scores(0)
    for n in range(nblk):
        rows = slice(n * w, (n + 1) * w)
        lo = max(n - 1, 0) * w
        hi = (n + 1) * w
        nk = hi - lo
        s = s_next
        if n + 1 < nblk:
            s_next = scores(n + 1)
        if shifted:
            m = [jnp.maximum(jnp.max(s[h * nk:(h + 1) * nk], axis=0, keepdims=True), sink2[h])
                 for h in range(N_KV_HEADS)]
            p = jnp.concatenate([jnp.exp2(s[h * nk:(h + 1) * nk] - m[h])
                                 for h in range(N_KV_HEADS)], axis=0).astype(BF16)
            sink_p = [jnp.exp2(sink2[h] - m[h]) for h in range(N_KV_HEADS)]
        else:
            p = jnp.exp2(s).astype(BF16)
            sink_p = [jnp.exp2(sink2[h]) for h in range(N_KV_HEADS)]
        outs, sums_p = _weighted_values(vs_s, lo, hi, p, sums[nk])
        o_t = jnp.concatenate([outs[h] * (1.0 / (sums_p[h] + sink_p[h]))
                               for h in range(N_KV_HEADS)], axis=0)
        _store_gated(o_ref, g_ref, rows, o_t, w)


def _swa(q, g, kv, sink, layer, shifted):
    b, l, _ = q.shape
    qspec = pl.BlockSpec((1, l, Q_WIDTH), lambda i: (i, 0, 0))
    return pl.pallas_call(
        functools.partial(_swa_kernel, l // SWA_WINDOW, layer, shifted),
        grid=(b,),
        in_specs=[pl.BlockSpec(memory_space=pltpu.SMEM),
                  qspec, qspec,
                  pl.BlockSpec((1, l, 2 * KV_WIDTH), lambda i: (i, 0, 0))],
        out_specs=qspec,
        out_shape=jax.ShapeDtypeStruct((b, l, Q_WIDTH), BF16),
        scratch_shapes=_ATTN_SCRATCH(l),
        compiler_params=_cparams("parallel"),
        name="swa",
    )(sink, q, g, kv)


def _moba_kernel(nblk, shifted, q_ref, qs_ref, g_ref, kv_ref, o_ref, ks_s, vs_s):
    blk = MOBA_BLOCK
    assert blk == PREP_ROWS
    nq = N_Q_TILES * blk
    kmean = jnp.concatenate(_prep_keys_values(kv_ref, ks_s, vs_s, nblk * blk), axis=0)
    sum_row = _sum_row(blk)
    if not shifted:
        r = lax.broadcasted_iota(jnp.int32, (2 * blk, LANES), 0)
        c = lax.broadcasted_iota(jnp.int32, (2 * blk, LANES), 1)
        head_cols = jnp.where(((c == 0) & (r < blk)) | ((c == 1) & (r >= blk)), 1.0, 0.0).astype(BF16)
        bias_row = lax.broadcasted_iota(jnp.int32, (SUM_ROWS, nq), 0)
        bias_pad = jnp.zeros((LANES - SUM_ROWS, nq), BF16)

    blk_id = lax.broadcasted_iota(jnp.int32, (nblk, blk), 0)
    ki = lax.broadcasted_iota(jnp.int32, (2 * blk, nq), 0) & (blk - 1)
    qi = lax.broadcasted_iota(jnp.int32, (2 * blk, nq), 1) & (blk - 1)
    causal = ki <= qi

    def prepare(i):
        rows = slice(i * blk, (i + 1) * blk)
        q_t = _queries_t(q_ref, rows)
        qsum = qs_ref[0, rows, :]

        past = blk_id < i
        sel = []
        for h in range(N_KV_HEADS):
            gate = lax.dot_general(_half_select(kmean, h), _half_select(qsum, h),
                                   (((1,), (1,)), ((), ())),
                                   precision=lax.Precision.HIGHEST,
                                   preferred_element_type=F32)
            gate = jnp.where(past, gate, -jnp.inf)
            rank = jnp.zeros((nblk, blk), jnp.int32)
            for n2 in range(nblk):
                other = gate[n2:n2 + 1, :]
                ahead = (other > gate) | ((other == gate) & (n2 < blk_id))
                rank = rank + ahead.astype(jnp.int32)
            chosen = jnp.where((rank < MOBA_TOPK) & past, 1.0, 0.0)
            sel.append(jnp.concatenate([chosen] * N_Q_TILES, axis=1))
        kst = _stacked_keys(ks_s, i * blk, (i + 1) * blk)
        s_own = jnp.where(causal, jnp.dot(kst, q_t, preferred_element_type=F32), -jnp.inf)
        return q_t, sel, s_own

    prepared = prepare(0)
    for i in range(nblk):
        rows = slice(i * blk, (i + 1) * blk)
        q_t, sel, s = prepared
        if i + 1 < nblk and i == 0:
            prepared = prepare(i + 1)

        def past_scores(n, q_t=q_t, sel=sel):
            kst = _stacked_keys(ks_s, n * blk, (n + 1) * blk)
            if shifted:
                return jnp.dot(kst, q_t, preferred_element_type=F32)
            ok = [sel[h][n:n + 1, :] > 0.5 for h in range(N_KV_HEADS)]
            bias = jnp.where(bias_row == 0, jnp.where(ok[0], 0.0, MASK_BIAS),
                             jnp.where(bias_row == 1, jnp.where(ok[1], 0.0, MASK_BIAS), 0.0))
            q_ext = jnp.concatenate([q_t, bias.astype(BF16), bias_pad], axis=0)
            k_ext = jnp.concatenate([kst, head_cols], axis=1)
            return jnp.dot(k_ext, q_ext, preferred_element_type=F32)

        s_next = past_scores(0) if i > 0 else None
        if shifted:
            m = [jnp.max(s[h * blk:(h + 1) * blk], axis=0, keepdims=True)
                 for h in range(N_KV_HEADS)]
            p = jnp.concatenate([jnp.exp2(s[h * blk:(h + 1) * blk] - m[h])
                                 for h in range(N_KV_HEADS)], axis=0).astype(BF16)
        else:
            p = jnp.exp2(s).astype(BF16)
        acc, den = _weighted_values(vs_s, i * blk, (i + 1) * blk, p, sum_row)

        for n in range(i):
            s = s_next
            if n + 1 < i:
                s_next = past_scores(n + 1)
            elif i + 1 < nblk:
                prepared = prepare(i + 1)
            if shifted:
                ok = [sel[h][n:n + 1, :] > 0.5 for h in range(N_KV_HEADS)]
                alpha = []
                parts = []
                for h in range(N_KV_HEADS):
                    s_h = s[h * blk:(h + 1) * blk]
                    m_new = jnp.where(ok[h], jnp.maximum(m[h], jnp.max(s_h, axis=0, keepdims=True)),
                                      m[h])
                    alpha.append(jnp.exp2(m[h] - m_new))
                    parts.append(jnp.exp2(s_h - jnp.where(ok[h], m_new, jnp.inf)))
                    m[h] = m_new
                p = jnp.concatenate(parts, axis=0).astype(BF16)
                outs, sums_p = _weighted_values(vs_s, n * blk, (n + 1) * blk, p, sum_row)
                acc = [acc[h] * alpha[h] + outs[h] for h in range(N_KV_HEADS)]
                den = [den[h] * alpha[h] + sums_p[h] for h in range(N_KV_HEADS)]
            else:
                p = jnp.exp2(s).astype(BF16)
                outs, sums_p = _weighted_values(vs_s, n * blk, (n + 1) * blk, p, sum_row)
                acc = [acc[h] + outs[h] for h in range(N_KV_HEADS)]
                den = [den[h] + sums_p[h] for h in range(N_KV_HEADS)]

        o_t = jnp.concatenate([acc[h] * (1.0 / den[h]) for h in range(N_KV_HEADS)], axis=0)
        _store_gated(o_ref, g_ref, rows, o_t, blk)


def _moba(q, qsum, g, kv, shifted):
    b, l, _ = q.shape
    qspec = pl.BlockSpec((1, l, Q_WIDTH), lambda i: (i, 0, 0))
    return pl.pallas_call(
        functools.partial(_moba_kernel, l // MOBA_BLOCK, shifted),
        grid=(b,),
        in_specs=[qspec,
                  pl.BlockSpec((1, l, LANES), lambda i: (i, 0, 0)),
                  qspec,
                  pl.BlockSpec((1, l, 2 * KV_WIDTH), lambda i: (i, 0, 0))],
        out_specs=qspec,
        out_shape=jax.ShapeDtypeStruct((b, l, Q_WIDTH), BF16),
        scratch_shapes=_ATTN_SCRATCH(l),
        compiler_params=_cparams("parallel"),
        name="moba",
    )(q, qsum, g, kv)


def _head_tiles(w):
    lead = w.shape[:-1]
    w = w.reshape(*lead, N_KV_HEADS, N_Q_TILES, HEAD_DIM)
    return jnp.swapaxes(w, -3, -2).reshape(*lead, Q_WIDTH)


def _permute_w_in(w):
    s, q, k = SSM_WIDTH, Q_WIDTH, KV_WIDTH
    bounds = [0]
    for n in (s, s, q, k, k, q, q, k, k, q):
        bounds.append(bounds[-1] + n)
    s_u, s_g, a_q, a_k, a_v, a_g, m_q, m_k, m_v, m_g = [
        w[..., bounds[i]:bounds[i + 1]] for i in range(10)]
    return jnp.concatenate(
        [s_u, s_g, _head_tiles(a_q), _head_tiles(a_g), _head_tiles(m_q), _head_tiles(m_g),
         a_k, a_v, m_k, m_v], axis=-1).astype(BF16)


def _permute_w_out(w):
    s, q = SSM_WIDTH, Q_WIDTH
    rows_t = lambda part: jnp.swapaxes(_head_tiles(jnp.swapaxes(part, -1, -2)), -1, -2)
    return jnp.concatenate([w[:, 0:s], rows_t(w[:, s:s + q]), rows_t(w[:, s + q:])],
                           axis=1).astype(BF16)


def _block_diag_in(bb):
    d, g, h, p = bb.shape
    eye = jnp.eye(g, dtype=bb.dtype)[None, :, None, :, None]
    return (bb[:, :, :, None, :] * eye).reshape(d, g * h, g * p)


def _block_diag_out(c):
    d, g, h, p = c.shape
    eye = jnp.eye(g, dtype=c.dtype)[None, :, None, :, None]
    return (jnp.swapaxes(c, -1, -2)[:, :, :, None, :] * eye).reshape(d, g * p, g * h)


def _logit_bound(q_gain, k_gain):
    return ((HEAD_DIM * ATTN_SCALE * LOG2E) * jnp.max(jnp.abs(q_gain), axis=-1)
            * jnp.max(jnp.abs(k_gain), axis=-1))


def kernel(x, norm_g, w_in, ssm_lam_re, ssm_lam_im, ssm_log_dt, ssm_b_re, ssm_b_im,
           ssm_c_re, ssm_c_im, ssm_d, ssm_glu_w, ssm_glu_b, swa_q_norm, swa_k_norm,
           swa_sink, moba_q_norm, moba_k_norm, w_out):
    b, l, d = x.shape
    depth = norm_g.shape[0]
    tiles = _tiles(l)
    a_re, a_im, bb_re, bb_im = _ssm_prep(ssm_lam_re, ssm_lam_im, ssm_log_dt, ssm_b_re, ssm_b_im)
    two = lambda v: jnp.concatenate([v, v], axis=-1).reshape(depth, 1, LANES).astype(F32)
    edge_params = {
        "norm_g": norm_g.reshape(depth, 1, d).astype(F32),
        "w_in": _permute_w_in(w_in),
        "w_out": _permute_w_out(w_out),
        "head_norms": [two(v) for v in (swa_q_norm, swa_k_norm, moba_q_norm, moba_k_norm)],
    }
    by_slab_cols = lambda m: m.reshape(depth, SSM_WIDTH, RE_SLABS, LANES)
    bmat = jnp.concatenate([by_slab_cols(_block_diag_in(bb_re)), by_slab_cols(_block_diag_in(bb_im))],
                           axis=-1).transpose(0, 2, 1, 3).astype(BF16)
    by_slab_rows = lambda m: m.reshape(depth, RE_SLABS, LANES, SSM_WIDTH)
    cmat = jnp.concatenate([by_slab_rows(_block_diag_out(ssm_c_re)),
                            -by_slab_rows(_block_diag_out(ssm_c_im))], axis=2).astype(BF16)
    ssm_params = (
        bmat, a_re.reshape(depth, 1, N_STATE), a_im.reshape(depth, 1, N_STATE), cmat,
        ssm_d.reshape(depth, 1, SSM_WIDTH).astype(F32), ssm_glu_w.astype(BF16),
        ssm_glu_b.reshape(depth, 1, SSM_WIDTH).astype(F32))
    sink = swa_sink.astype(F32)
    small = ((_logit_bound(swa_q_norm, swa_k_norm) <= LOGIT_BOUND)
             & (jnp.max(jnp.abs(sink), axis=-1) * LOG2E <= LOGIT_BOUND)
             & (_logit_bound(moba_q_norm, moba_k_norm) <= LOGIT_BOUND))

    def attention(layer, shifted, a_q, a_g, a_kv, m_q, m_qsum, m_g, m_kv, sink):
        return (_swa(a_q, a_g, a_kv, sink, layer, shifted), _moba(m_q, m_qsum, m_g, m_kv, shifted))

    proj = _layer_edge(x, None, edge_params, None, 0, tiles)
    for layer in range(depth):
        s_u, s_g, a_q, a_g, m_q, m_qsum, m_g, a_kv, m_kv = proj
        y_swa, y_moba = lax.cond(small[layer], functools.partial(attention, layer, False),
                                 functools.partial(attention, layer, True),
                                 a_q, a_g, a_kv, m_q, m_qsum, m_g, m_kv, sink)
        branch_outs = (_ssm(s_u, s_g, ssm_params, layer, tiles["scan_steps"]), y_swa, y_moba)
        nxt = layer + 1 if layer + 1 < depth else None
        x, *proj = _layer_edge(x, branch_outs, edge_params, layer, nxt, tiles)
    return x
```

```python
import functools
import math

import jax
import jax.numpy as jnp
from jax import lax
from jax.experimental import pallas as pl
from jax.experimental.pallas import tpu as pltpu

F32 = jnp.float32
BF16 = jnp.bfloat16

HEAD_DIM = 64
SSM_WIDTH = 256
SSM_GROUPS = 16
SSM_STATE = 64
N_STATE = SSM_GROUPS * SSM_STATE
N_Q_HEADS = 6
N_KV_HEADS = 2
Q_WIDTH = N_Q_HEADS * HEAD_DIM
KV_WIDTH = N_KV_HEADS * HEAD_DIM
SWA_WINDOW = 128
MOBA_BLOCK = 256
MOBA_TOPK = 3
RMS_EPS = 1e-6
ATTN_SCALE = HEAD_DIM ** -0.5
LOG2E = math.log2(math.e)
LANES = 128
N_Q_TILES = Q_WIDTH // LANES
VMEM_LIMIT = 56 * 1024 * 1024
SUM_ROWS = 16
PREP_ROWS = 256
OFF_U, OFF_SG = 0, SSM_WIDTH
OFF_AQ = 2 * SSM_WIDTH
OFF_AG = OFF_AQ + Q_WIDTH
OFF_MQ = OFF_AG + Q_WIDTH
OFF_MG = OFF_MQ + Q_WIDTH
OFF_AKV = OFF_MG + Q_WIDTH
OFF_MKV = OFF_AKV + 2 * KV_WIDTH
PROJ_CHUNK = 512
LOGIT_BOUND = 60.0
MASK_BIAS = -300.0


def _tiles(seq):
    return {"proj_rows": min(1024, seq), "fused_proj_rows": min(1024, seq), "scan_steps": min(64, seq)}


def _cparams(*sem):
    return pltpu.CompilerParams(dimension_semantics=sem, vmem_limit_bytes=VMEM_LIMIT)


def _silu(t):
    return t * jax.nn.sigmoid(t)


def _ssm_prep_kernel(lr_ref, li_ref, ldt_ref, br_ref, bi_ref,
                     are_ref, aim_ref, bbr_ref, bbi_ref):
    lr = lr_ref[...]
    li = li_ref[...]
    dt = jnp.exp(ldt_ref[...])
    mag = jnp.exp(lr * dt)
    a_re = mag * jnp.cos(li * dt)
    a_im = mag * jnp.sin(li * dt)
    den = lr * lr + li * li
    nr = a_re - 1.0
    ni = a_im
    cr = (nr * lr + ni * li) / den
    ci = (ni * lr - nr * li) / den
    br = br_ref[...]
    bi = bi_ref[...]
    are_ref[...] = a_re
    aim_ref[...] = a_im
    bbr_ref[...] = cr * br - ci * bi
    bbi_ref[...] = cr * bi + ci * br


def _ssm_prep(lam_re, lam_im, log_dt, b_re, b_im):
    d, g, p = lam_re.shape
    h = b_re.shape[-1]
    shp = (d * g, h, p)
    bc = lambda a: jnp.broadcast_to(a.reshape(d * g, 1, -1), shp).astype(F32)
    tr = lambda a: jnp.swapaxes(a, -1, -2).reshape(shp).astype(F32)
    outs = pl.pallas_call(
        _ssm_prep_kernel,
        out_shape=[jax.ShapeDtypeStruct(shp, F32)] * 4,
        name="ssm_prep",
    )(bc(lam_re), bc(lam_im), bc(log_dt[..., None]), tr(b_re), tr(b_im))
    a_re, a_im, bb_re, bb_im = outs
    return (a_re[:, 0, :].reshape(d, g, p), a_im[:, 0, :].reshape(d, g, p),
            bb_re.reshape(d, g, h, p), bb_im.reshape(d, g, h, p))


def _pair_rms_norm(t, gain2):
    lane = lax.broadcasted_iota(jnp.int32, t.shape, 1)
    lo = lane < HEAD_DIM
    sq = t * t
    s_lo = jnp.sum(jnp.where(lo, sq, 0.0), axis=-1, keepdims=True)
    s_hi = jnp.sum(jnp.where(lo, 0.0, sq), axis=-1, keepdims=True)
    r_lo = lax.rsqrt(s_lo * (1.0 / HEAD_DIM) + RMS_EPS)
    r_hi = lax.rsqrt(s_hi * (1.0 / HEAD_DIM) + RMS_EPS)
    return t * jnp.where(lo, r_lo, r_hi) * gain2


def _half_select(t, kv_head):
    lane = lax.broadcasted_iota(jnp.int32, t.shape, 1)
    keep = (lane < HEAD_DIM) if kv_head == 0 else (lane >= HEAD_DIM)
    return jnp.where(keep, t, 0.0)


def _project(x, g_ref, w_ref, aqn_ref, akn_ref, mqn_ref, mkn_ref,
             u_ref, sg_ref, aq_ref, ag_ref, mq_ref, mqs_ref, mg_ref, akv_ref, mkv_ref):
    def lanes(j):
        return slice(j * LANES, (j + 1) * LANES)

    half = x.shape[0] // 2
    for r in range(2):
        rs = slice(r * half, (r + 1) * half)
        xr = x[rs]
        ms = jnp.mean(xr * xr, axis=-1, keepdims=True)
        h = (xr * lax.rsqrt(ms + RMS_EPS) * g_ref[...]).astype(BF16)
        chunks = {}

        def tile(col):
            c, k = divmod(col, PROJ_CHUNK)
            if c not in chunks:
                chunks[c] = jnp.dot(h, w_ref[:, c * PROJ_CHUNK:(c + 1) * PROJ_CHUNK],
                                    preferred_element_type=F32)
            return chunks[c][:, k:k + LANES]

        for off, kn_ref, kv_ref in ((OFF_MKV, mkn_ref, mkv_ref), (OFF_AKV, akn_ref, akv_ref)):
            kv_ref[0, rs, lanes(0)] = _pair_rms_norm(tile(off), kn_ref[...]).astype(BF16)
            kv_ref[0, rs, lanes(1)] = tile(off + KV_WIDTH).astype(BF16)

        qsum = jnp.zeros((half, LANES), F32)
        for j in reversed(range(N_Q_TILES)):
            mg_ref[0, rs, lanes(j)] = _silu(tile(OFF_MG + j * LANES)).astype(BF16)
            mqn = _pair_rms_norm(tile(OFF_MQ + j * LANES), mqn_ref[...])
            qsum = qsum + mqn
            mq_ref[0, rs, lanes(j)] = (mqn * (ATTN_SCALE * LOG2E)).astype(BF16)
        mqs_ref[0, rs, :] = qsum
        for j in reversed(range(N_Q_TILES)):
            ag_ref[0, rs, lanes(j)] = _silu(tile(OFF_AG + j * LANES)).astype(BF16)
            aq_ref[0, rs, lanes(j)] = (_pair_rms_norm(tile(OFF_AQ + j * LANES), aqn_ref[...])
                                       * (ATTN_SCALE * LOG2E)).astype(BF16)

        for j in range(SSM_WIDTH // LANES):
            sg_ref[0, rs, lanes(j)] = _silu(tile(OFF_SG + j * LANES)).astype(BF16)
            u_ref[0, rs, lanes(j)] = tile(OFF_U + j * LANES).astype(BF16)


def _residual_out(x_ref, ys_ref, ya_ref, ym_ref, wo_ref):
    y = jnp.concatenate([ys_ref[0], ya_ref[0], ym_ref[0]], axis=1)
    return x_ref[0] + jnp.dot(y, wo_ref[...], preferred_element_type=F32)


def _inproj_kernel(x_ref, *refs):
    _project(x_ref[0], *refs)


def _outproj_kernel(x_ref, ys_ref, ya_ref, ym_ref, wo_ref, o_ref):
    o_ref[0] = _residual_out(x_ref, ys_ref, ya_ref, ym_ref, wo_ref)


def _outin_kernel(x_ref, ys_ref, ya_ref, ym_ref, wo_ref, *refs):
    proj_in, proj_out = refs[:6], refs[7:]
    x = _residual_out(x_ref, ys_ref, ya_ref, ym_ref, wo_ref)
    refs[6][0] = x
    _project(x, *proj_in, *proj_out)


def _layer_edge(x, branch_outs, params, out_layer, in_layer, tiles):
    b, l, d = x.shape
    fused = out_layer is not None and in_layer is not None
    tl = tiles["fused_proj_rows" if fused else "proj_rows"]
    row = lambda width: pl.BlockSpec((1, tl, width), lambda i, t: (i, t, 0))
    shape = lambda width, dtype: jax.ShapeDtypeStruct((b, l, width), dtype)

    def layer_block(a, layer):
        return pl.BlockSpec((None,) + a.shape[1:], lambda i, t: (layer,) + (0,) * (a.ndim - 1),
                            pipeline_mode=pl.Buffered(1))

    args, in_specs, out_specs, out_shape = [x], [row(d)], [], []
    if out_layer is not None:
        args += list(branch_outs) + [params["w_out"]]
        in_specs += [row(SSM_WIDTH), row(Q_WIDTH), row(Q_WIDTH),
                     layer_block(params["w_out"], out_layer)]
        out_specs.append(row(d))
        out_shape.append(shape(d, F32))
    if in_layer is not None:
        proj_params = [params["norm_g"], params["w_in"]] + params["head_norms"]
        args += proj_params
        in_specs += [layer_block(a, in_layer) for a in proj_params]
        out_specs += [row(SSM_WIDTH), row(SSM_WIDTH), row(Q_WIDTH), row(Q_WIDTH), row(Q_WIDTH),
                      row(LANES), row(Q_WIDTH), row(2 * KV_WIDTH), row(2 * KV_WIDTH)]
        out_shape += [shape(SSM_WIDTH, BF16), shape(SSM_WIDTH, BF16), shape(Q_WIDTH, BF16),
                      shape(Q_WIDTH, BF16), shape(Q_WIDTH, BF16), shape(LANES, F32),
                      shape(Q_WIDTH, BF16), shape(2 * KV_WIDTH, BF16), shape(2 * KV_WIDTH, BF16)]
    body, name = {(False, True): (_inproj_kernel, "inproj"),
                  (True, False): (_outproj_kernel, "outproj"),
                  (True, True): (_outin_kernel, "outin")}[(out_layer is not None, in_layer is not None)]
    outs = pl.pallas_call(
        body,
        grid=(b, l // tl),
        in_specs=in_specs,
        out_specs=out_specs,
        out_shape=out_shape,
        compiler_params=_cparams("parallel", "parallel"),
        name=name,
    )(*args)
    return outs


ROW_PAD = 4
N_SLABS = 2 * N_STATE // LANES
RE_SLABS = N_SLABS // 2
N_STAGES = 3


def _ssm_kernel(nb, tc, u_ref, u2_ref, sg2_ref, bmat_ref, are_ref, aim_ref, cmat_ref,
                d_ref, gw_ref, gb_ref, y_ref, upad_a, upad_c, ypad_s, st_s, *slots):
    pitch = tc + ROW_PAD
    i = pl.program_id(0)

    @pl.when(i == 0)
    def _():
        st_s[...] = jnp.zeros_like(st_s)
        upad_a[...] = jnp.zeros_like(upad_a)
        upad_c[...] = jnp.zeros_like(upad_c)
        for s in slots:
            s[...] = jnp.zeros_like(s)

    pieces = RE_SLABS
    steps = tc // pieces

    def stages(in_s, scan_s, out_s):
        for b in range(nb):
            upad_a[pl.ds(b * pitch, tc), :] = u_ref[b].astype(F32)
            upad_c[pl.ds(b * pitch, tc), :] = u2_ref[b].astype(F32)
        u_in = upad_a[...].astype(BF16)
        a_re = [jnp.broadcast_to(are_ref[:, j * LANES:(j + 1) * LANES], (nb, LANES))
                for j in range(RE_SLABS)]
        a_im = [jnp.broadcast_to(aim_ref[:, j * LANES:(j + 1) * LANES], (nb, LANES))
                for j in range(RE_SLABS)]
        state = [st_s[j] for j in range(N_SLABS)]
        y = d_ref[...] * upad_c[...]
        for k in range(pieces):
            bu = jnp.dot(u_in, bmat_ref[k], preferred_element_type=F32)
            in_s[k] = bu[:, 0:LANES]
            in_s[RE_SLABS + k] = bu[:, LANES:2 * LANES]
            for t in range(k * steps, (k + 1) * steps):
                rows = pl.ds(t, nb, stride=pitch)
                for j in range(RE_SLABS):
                    xr, xi = state[j], state[RE_SLABS + j]
                    nr = a_re[j] * xr - a_im[j] * xi + scan_s[j, rows, :]
                    ni = a_re[j] * xi + a_im[j] * xr + scan_s[RE_SLABS + j, rows, :]
                    scan_s[j, rows, :] = nr
                    scan_s[RE_SLABS + j, rows, :] = ni
                    state[j], state[RE_SLABS + j] = nr, ni
            xs = jnp.concatenate([out_s[k], out_s[RE_SLABS + k]], axis=1).astype(BF16)
            y = y + jnp.dot(xs, cmat_ref[k], preferred_element_type=F32)
        for j in range(N_SLABS):
            st_s[j] = state[j]
        y = jax.nn.gelu(y)
        z = jnp.dot(y.astype(BF16), gw_ref[...], preferred_element_type=F32) + gb_ref[...]
        ypad_s[...] = y * jax.nn.sigmoid(z)
        for b in range(nb):
            y_ref[b] = (ypad_s[pl.ds(b * pitch, tc), :] * sg2_ref[b].astype(F32)).astype(BF16)

    for r in range(N_STAGES):
        @pl.when(lax.rem(i, N_STAGES) == r)
        def _(r=r):
            stages(slots[r], slots[(r + 2) % N_STAGES], slots[(r + 1) % N_STAGES])


def _ssm(u, sg, ssm_params, layer, tc):
    nb, l, _ = u.shape
    rows = nb * (tc + ROW_PAD)
    n_chunks = l // tc
    chunk = lambda lag: pl.BlockSpec(
        (nb, tc, SSM_WIDTH), lambda i: (0, jnp.clip(i - lag, 0, n_chunks - 1), 0))
    full = lambda a: pl.BlockSpec((None,) + a.shape[1:], lambda i: (layer,) + (0,) * (a.ndim - 1))
    slab = pltpu.VMEM((N_SLABS, rows, LANES), F32)
    return pl.pallas_call(
        functools.partial(_ssm_kernel, nb, tc),
        grid=(n_chunks + N_STAGES - 1,),
        in_specs=[chunk(0), chunk(2), chunk(2)] + [full(a) for a in ssm_params],
        out_specs=chunk(2),
        out_shape=jax.ShapeDtypeStruct((nb, l, SSM_WIDTH), BF16),
        scratch_shapes=[pltpu.VMEM((rows, SSM_WIDTH), F32),
                        pltpu.VMEM((rows, SSM_WIDTH), F32),
                        pltpu.VMEM((rows, SSM_WIDTH), F32),
                        pltpu.VMEM((N_SLABS, nb, LANES), F32),
                        slab, slab, slab],
        compiler_params=_cparams("arbitrary"),
        name="ssm",
    )(u, u, sg, *ssm_params)


def _sum_row(n_keys):
    r = lax.broadcasted_iota(jnp.int32, (SUM_ROWS, n_keys), 0)
    return jnp.where(r == 0, 1.0, 0.0).astype(BF16)


def _prep_keys_values(kv_ref, ks_s, vs_s, seq):
    means = []
    for c in range(seq // PREP_ROWS):
        rows = slice(c * PREP_ROWS, (c + 1) * PREP_ROWS)
        kvb = kv_ref[0, rows, :].astype(F32)
        kf = kvb[:, 0:KV_WIDTH]
        means.append(jnp.mean(kf, axis=0, keepdims=True))
        ks_s[0, rows, :] = _half_select(kf, 0).astype(BF16)
        ks_s[1, rows, :] = _half_select(kf, 1).astype(BF16)
        vs_s[:, rows] = jnp.transpose(kvb[:, KV_WIDTH:2 * KV_WIDTH]).astype(BF16)
    return means


def _stacked_keys(ks_s, lo, hi):
    return jnp.concatenate([ks_s[0, lo:hi, :], ks_s[1, lo:hi, :]], axis=0)


def _weighted_values(vs_s, lo, hi, p, sum_row):
    nk = hi - lo
    outs, sums = [], []
    for h in range(N_KV_HEADS):
        v_t = jnp.concatenate([vs_s[h * HEAD_DIM:(h + 1) * HEAD_DIM, lo:hi], sum_row], axis=0)
        pv = jnp.dot(v_t, p[h * nk:(h + 1) * nk], preferred_element_type=F32)
        outs.append(pv[0:HEAD_DIM])
        sums.append(pv[HEAD_DIM:HEAD_DIM + 1])
    return outs, sums


def _queries_t(q_ref, rows):
    return jnp.concatenate(
        [jnp.transpose(q_ref[0, rows, j * LANES:(j + 1) * LANES]) for j in range(N_Q_TILES)], axis=1)


def _store_gated(o_ref, g_ref, rows, o_t, width):
    for j in range(N_Q_TILES):
        cols = slice(j * LANES, (j + 1) * LANES)
        o = jnp.transpose(o_t[:, j * width:(j + 1) * width]) * g_ref[0, rows, cols].astype(F32)
        o_ref[0, rows, cols] = o.astype(BF16)


_ATTN_SCRATCH = lambda seq: [pltpu.VMEM((N_KV_HEADS, seq, KV_WIDTH), BF16),
                             pltpu.VMEM((KV_WIDTH, seq), BF16)]


def _swa_kernel(nblk, layer, shifted, sink_ref, q_ref, g_ref, kv_ref, o_ref, ks_s, vs_s):
    w = SWA_WINDOW
    nq = N_Q_TILES * w
    _prep_keys_values(kv_ref, ks_s, vs_s, nblk * w)
    lane = lax.broadcasted_iota(jnp.int32, (1, nq), 1)
    sink2 = []
    for h in range(N_KV_HEADS):
        a = [sink_ref[layer, j + N_Q_TILES * h] * LOG2E for j in range(N_Q_TILES)]
        sink2.append(jnp.where(lane < w, a[0], jnp.where(lane < 2 * w, a[1], a[2])))

    def window_mask(n_keys):
        kpos = (lax.broadcasted_iota(jnp.int32, (2 * n_keys, nq), 0) & (n_keys - 1)) - (n_keys - w)
        qpos = lax.broadcasted_iota(jnp.int32, (2 * n_keys, nq), 1) & (w - 1)
        rel = qpos - kpos
        return (rel >= 0) & (rel < w)

    masks = {w: window_mask(w), 2 * w: window_mask(2 * w)}
    sums = {w: _sum_row(w), 2 * w: _sum_row(2 * w)}

    def scores(n):
        lo = max(n - 1, 0) * w
        hi = (n + 1) * w
        q_t = _queries_t(q_ref, slice(n * w, (n + 1) * w))
        return jnp.where(masks[hi - lo], jnp.dot(_stacked_keys(ks_s, lo, hi), q_t,
                                                 preferred_element_type=F32), -jnp.inf)

    s_next = scores(0)
    for n in range(nblk):
        rows = slice(n * w, (n + 1) * w)
        lo = max(n - 1, 0) * w
        hi = (n + 1) * w
        nk = hi - lo
        s = s_next
        if n + 1 < nblk:
            s_next = scores(n + 1)
        if shifted:
            m = [jnp.maximum(jnp.max(s[h * nk:(h + 1) * nk], axis=0, keepdims=True), sink2[h])
                 for h in range(N_KV_HEADS)]
            p = jnp.concatenate([jnp.exp2(s[h * nk:(h + 1) * nk] - m[h])
                                 for h in range(N_KV_HEADS)], axis=0).astype(BF16)
            sink_p = [jnp.exp2(sink2[h] - m[h]) for h in range(N_KV_HEADS)]
        else:
            p = jnp.exp2(s).astype(BF16)
            sink_p = [jnp.exp2(sink2[h]) for h in range(N_KV_HEADS)]
        outs, sums_p = _weighted_values(vs_s, lo, hi, p, sums[nk])
        o_t = jnp.concatenate([outs[h] * (1.0 / (sums_p[h] + sink_p[h]))
                               for h in range(N_KV_HEADS)], axis=0)
        _store_gated(o_ref, g_ref, rows, o_t, w)


def _swa(q, g, kv, sink, layer, shifted):
    b, l, _ = q.shape
    qspec = pl.BlockSpec((1, l, Q_WIDTH), lambda i: (i, 0, 0))
    return pl.pallas_call(
        functools.partial(_swa_kernel, l // SWA_WINDOW, layer, shifted),
        grid=(b,),
        in_specs=[pl.BlockSpec(memory_space=pltpu.SMEM),
                  qspec, qspec,
                  pl.BlockSpec((1, l, 2 * KV_WIDTH), lambda i: (i, 0, 0))],
        out_specs=qspec,
        out_shape=jax.ShapeDtypeStruct((b, l, Q_WIDTH), BF16),
        scratch_shapes=_ATTN_SCRATCH(l),
        compiler_params=_cparams("parallel"),
        name="swa",
    )(sink, q, g, kv)


def _moba_kernel(nblk, shifted, q_ref, qs_ref, g_ref, kv_ref, o_ref, ks_s, vs_s):
    blk = MOBA_BLOCK
    assert blk == PREP_ROWS
    nq = N_Q_TILES * blk
    kmean = jnp.concatenate(_prep_keys_values(kv_ref, ks_s, vs_s, nblk * blk), axis=0)
    sum_row = _sum_row(blk)
    if not shifted:
        r = lax.broadcasted_iota(jnp.int32, (2 * blk, LANES), 0)
        c = lax.broadcasted_iota(jnp.int32, (2 * blk, LANES), 1)
        head_cols = jnp.where(((c == 0) & (r < blk)) | ((c == 1) & (r >= blk)), 1.0, 0.0).astype(BF16)
        bias_row = lax.broadcasted_iota(jnp.int32, (SUM_ROWS, nq), 0)
        bias_pad = jnp.zeros((LANES - SUM_ROWS, nq), BF16)

    blk_id = lax.broadcasted_iota(jnp.int32, (nblk, blk), 0)
    ki = lax.broadcasted_iota(jnp.int32, (2 * blk, nq), 0) & (blk - 1)
    qi = lax.broadcasted_iota(jnp.int32, (2 * blk, nq), 1) & (blk - 1)
    causal = ki <= qi

    def prepare(i):
        rows = slice(i * blk, (i + 1) * blk)
        q_t = _queries_t(q_ref, rows)
        qsum = qs_ref[0, rows, :]

        past = blk_id < i
        sel = []
        for h in range(N_KV_HEADS):
            gate = lax.dot_general(_half_select(kmean, h), _half_select(qsum, h),
                                   (((1,), (1,)), ((), ())),
                                   precision=lax.Precision.HIGHEST,
                                   preferred_element_type=F32)
            gate = jnp.where(past, gate, -jnp.inf)
            rank = jnp.zeros((nblk, blk), jnp.int32)
            for n2 in range(nblk):
                other = gate[n2:n2 + 1, :]
                ahead = (other > gate) | ((other == gate) & (n2 < blk_id))
                rank = rank + ahead.astype(jnp.int32)
            chosen = jnp.where((rank < MOBA_TOPK) & past, 1.0, 0.0)
            sel.append(jnp.concatenate([chosen] * N_Q_TILES, axis=1))
        kst = _stacked_keys(ks_s, i * blk, (i + 1) * blk)
        s_own = jnp.where(causal, jnp.dot(kst, q_t, preferred_element_type=F32), -jnp.inf)
        return q_t, sel, s_own

    prepared = prepare(0)
    for i in range(nblk):
        rows = slice(i * blk, (i + 1) * blk)
        q_t, sel, s = prepared
        if i + 1 < nblk and i == 0:
            prepared = prepare(i + 1)

        def past_scores(n, q_t=q_t, sel=sel):
            kst = _stacked_keys(ks_s, n * blk, (n + 1) * blk)
            if shifted:
                return jnp.dot(kst, q_t, preferred_element_type=F32)
            ok = [sel[h][n:n + 1, :] > 0.5 for h in range(N_KV_HEADS)]
            bias = jnp.where(bias_row == 0, jnp.where(ok[0], 0.0, MASK_BIAS),
                             jnp.where(bias_row == 1, jnp.where(ok[1], 0.0, MASK_BIAS), 0.0))
            q_ext = jnp.concatenate([q_t, bias.astype(BF16), bias_pad], axis=0)
            k_ext = jnp.concatenate([kst, head_cols], axis=1)
            return jnp.dot(k_ext, q_ext, preferred_element_type=F32)

        s_next = past_scores(0) if i > 0 else None
        if shifted:
            m = [jnp.max(s[h * blk:(h + 1) * blk], axis=0, keepdims=True)
                 for h in range(N_KV_HEADS)]
            p = jnp.concatenate([jnp.exp2(s[h * blk:(h + 1) * blk] - m[h])
                                 for h in range(N_KV_HEADS)], axis=0).astype(BF16)
        else:
            p = jnp.exp2(s).astype(BF16)
        acc, den = _weighted_values(vs_s, i * blk, (i + 1) * blk, p, sum_row)

        for n in range(i):
            s = s_next
            if n + 1 < i:
                s_next = past_scores(n + 1)
            elif i + 1 < nblk:
                prepared = prepare(i + 1)
            if shifted:
                ok = [sel[h][n:n + 1, :] > 0.5 for h in range(N_KV_HEADS)]
                alpha = []
                parts = []
                for h in range(N_KV_HEADS):
                    s_h = s[h * blk:(h + 1) * blk]
                    m_new = jnp.where(ok[h], jnp.maximum(m[h], jnp.max(s_h, axis=0, keepdims=True)),
                                      m[h])
                    alpha.append(jnp.exp2(m[h] - m_new))
                    parts.append(jnp.exp2(s_h - jnp.where(ok[h], m_new, jnp.inf)))
                    m[h] = m_new
                p = jnp.concatenate(parts, axis=0).astype(BF16)
                outs, sums_p = _weighted_values(vs_s, n * blk, (n + 1) * blk, p, sum_row)
                acc = [acc[h] * alpha[h] + outs[h] for h in range(N_KV_HEADS)]
                den = [den[h] * alpha[h] + sums_p[h] for h in range(N_KV_HEADS)]
            else:
                p = jnp.exp2(s).astype(BF16)
                outs, sums_p = _weighted_values(vs_s, n * blk, (n + 1) * blk, p, sum_row)
                acc = [acc[h] + outs[h] for h in range(N_KV_HEADS)]
                den = [den[h] + sums_p[h] for h in range(N_KV_HEADS)]

        o_t = jnp.concatenate([acc[h] * (1.0 / den[h]) for h in range(N_KV_HEADS)], axis=0)
        _store_gated(o_ref, g_ref, rows, o_t, blk)


def _moba(q, qsum, g, kv, shifted):
    b, l, _ = q.shape
    qspec = pl.BlockSpec((1, l, Q_WIDTH), lambda i: (i, 0, 0))
    return pl.pallas_call(
        functools.partial(_moba_kernel, l // MOBA_BLOCK, shifted),
        grid=(b,),
        in_specs=[qspec,
                  pl.BlockSpec((1, l, LANES), lambda i: (i, 0, 0)),
                  qspec,
                  pl.BlockSpec((1, l, 2 * KV_WIDTH), lambda i: (i, 0, 0))],
        out_specs=qspec,
        out_shape=jax.ShapeDtypeStruct((b, l, Q_WIDTH), BF16),
        scratch_shapes=_ATTN_SCRATCH(l),
        compiler_params=_cparams("parallel"),
        name="moba",
    )(q, qsum, g, kv)


def _head_tiles(w):
    lead = w.shape[:-1]
    w = w.reshape(*lead, N_KV_HEADS, N_Q_TILES, HEAD_DIM)
    return jnp.swapaxes(w, -3, -2).reshape(*lead, Q_WIDTH)


def _permute_w_in(w):
    s, q, k = SSM_WIDTH, Q_WIDTH, KV_WIDTH
    bounds = [0]
    for n in (s, s, q, k, k, q, q, k, k, q):
        bounds.append(bounds[-1] + n)
    w = w.astype(BF16)
    s_u, s_g, a_q, a_k, a_v, a_g, m_q, m_k, m_v, m_g = [
        w[..., bounds[i]:bounds[i + 1]] for i in range(10)]
    return jnp.concatenate(
        [s_u, s_g, _head_tiles(a_q), _head_tiles(a_g), _head_tiles(m_q), _head_tiles(m_g),
         a_k, a_v, m_k, m_v], axis=-1)


def _permute_w_out(w):
    s, q = SSM_WIDTH, Q_WIDTH
    rows_t = lambda part: jnp.swapaxes(_head_tiles(jnp.swapaxes(part, -1, -2)), -1, -2)
    w = w.astype(BF16)
    return jnp.concatenate([w[:, 0:s], rows_t(w[:, s:s + q]), rows_t(w[:, s + q:])], axis=1)


def _block_diag_in(bb):
    d, g, h, p = bb.shape
    eye = jnp.eye(g, dtype=bb.dtype)[None, :, None, :, None]
    return (bb[:, :, :, None, :] * eye).reshape(d, g * h, g * p)


def _block_diag_out(c):
    d, g, h, p = c.shape
    eye = jnp.eye(g, dtype=c.dtype)[None, :, None, :, None]
    return (jnp.swapaxes(c, -1, -2)[:, :, :, None, :] * eye).reshape(d, g * p, g * h)


def _logit_bound(q_gain, k_gain):
    return ((HEAD_DIM * ATTN_SCALE * LOG2E) * jnp.max(jnp.abs(q_gain), axis=-1)
            * jnp.max(jnp.abs(k_gain), axis=-1))


def kernel(x, norm_g, w_in, ssm_lam_re, ssm_lam_im, ssm_log_dt, ssm_b_re, ssm_b_im,
           ssm_c_re, ssm_c_im, ssm_d, ssm_glu_w, ssm_glu_b, swa_q_norm, swa_k_norm,
           swa_sink, moba_q_norm, moba_k_norm, w_out):
    b, l, d = x.shape
    depth = norm_g.shape[0]
    tiles = _tiles(l)
    a_re, a_im, bb_re, bb_im = _ssm_prep(ssm_lam_re, ssm_lam_im, ssm_log_dt, ssm_b_re, ssm_b_im)
    two = lambda v: jnp.concatenate([v, v], axis=-1).reshape(depth, 1, LANES).astype(F32)
    edge_params = {
        "norm_g": norm_g.reshape(depth, 1, d).astype(F32),
        "w_in": _permute_w_in(w_in),
        "w_out": _permute_w_out(w_out),
        "head_norms": [two(v) for v in (swa_q_norm, swa_k_norm, moba_q_norm, moba_k_norm)],
    }
    by_slab_cols = lambda m: m.reshape(depth, SSM_WIDTH, RE_SLABS, LANES)
    bmat = jnp.concatenate([by_slab_cols(_block_diag_in(bb_re)), by_slab_cols(_block_diag_in(bb_im))],
                           axis=-1).transpose(0, 2, 1, 3).astype(BF16)
    by_slab_rows = lambda m: m.reshape(depth, RE_SLABS, LANES, SSM_WIDTH)
    cmat = jnp.concatenate([by_slab_rows(_block_diag_out(ssm_c_re)),
                            -by_slab_rows(_block_diag_out(ssm_c_im))], axis=2).astype(BF16)
    ssm_params = (
        bmat, a_re.reshape(depth, 1, N_STATE), a_im.reshape(depth, 1, N_STATE), cmat,
        ssm_d.reshape(depth, 1, SSM_WIDTH).astype(F32), ssm_glu_w.astype(BF16),
        ssm_glu_b.reshape(depth, 1, SSM_WIDTH).astype(F32))
    sink = swa_sink.astype(F32)
    small = ((_logit_bound(swa_q_norm, swa_k_norm) <= LOGIT_BOUND)
             & (jnp.max(jnp.abs(sink), axis=-1) * LOG2E <= LOGIT_BOUND)
             & (_logit_bound(moba_q_norm, moba_k_norm) <= LOGIT_BOUND))

    def attention(layer, shifted, a_q, a_g, a_kv, m_q, m_qsum, m_g, m_kv, sink):
        return (_swa(a_q, a_g, a_kv, sink, layer, shifted), _moba(m_q, m_qsum, m_g, m_kv, shifted))

    proj = _layer_edge(x, None, edge_params, None, 0, tiles)
    for layer in range(depth):
        s_u, s_g, a_q, a_g, m_q, m_qsum, m_g, a_kv, m_kv = proj
        y_swa, y_moba = lax.cond(small[layer], functools.partial(attention, layer, False),
                                 functools.partial(attention, layer, True),
                                 a_q, a_g, a_kv, m_q, m_qsum, m_g, m_kv, sink)
        branch_outs = (_ssm(s_u, s_g, ssm_params, layer, tiles["scan_steps"]), y_swa, y_moba)
        nxt = layer + 1 if layer + 1 < depth else None
        x, *proj = _layer_edge(x, branch_outs, edge_params, layer, nxt, tiles)
    return x
```

```python
import functools
import math

import jax
import jax.numpy as jnp
from jax import lax
from jax.experimental import pallas as pl
from jax.experimental.pallas import tpu as pltpu

F32 = jnp.float32
BF16 = jnp.bfloat16

HEAD_DIM = 64
SSM_WIDTH = 256
SSM_GROUPS = 16
SSM_STATE = 64
N_STATE = SSM_GROUPS * SSM_STATE
N_Q_HEADS = 6
N_KV_HEADS = 2
Q_WIDTH = N_Q_HEADS * HEAD_DIM
KV_WIDTH = N_KV_HEADS * HEAD_DIM
SWA_WINDOW = 128
MOBA_BLOCK = 256
MOBA_TOPK = 3
RMS_EPS = 1e-6
ATTN_SCALE = HEAD_DIM ** -0.5
LOG2E = math.log2(math.e)
LANES = 128
N_Q_TILES = Q_WIDTH // LANES
VMEM_LIMIT = 56 * 1024 * 1024
SUM_ROWS = 16
PREP_ROWS = 256
OFF_U, OFF_SG = 0, SSM_WIDTH
OFF_AQ = 2 * SSM_WIDTH
OFF_AG = OFF_AQ + Q_WIDTH
OFF_MQ = OFF_AG + Q_WIDTH
OFF_MG = OFF_MQ + Q_WIDTH
OFF_AKV = OFF_MG + Q_WIDTH
OFF_MKV = OFF_AKV + 2 * KV_WIDTH
PROJ_CHUNK = 512
LOGIT_BOUND = 60.0
MASK_BIAS = -300.0


def _tiles(seq):
    return {"proj_rows": min(1024, seq), "fused_proj_rows": min(1024, seq), "scan_steps": min(32, seq)}


def _cparams(*sem):
    return pltpu.CompilerParams(dimension_semantics=sem, vmem_limit_bytes=VMEM_LIMIT)


def _silu(t):
    return t * jax.nn.sigmoid(t)


def _ssm_prep_kernel(lr_ref, li_ref, ldt_ref, br_ref, bi_ref,
                     are_ref, aim_ref, bbr_ref, bbi_ref):
    lr = lr_ref[...]
    li = li_ref[...]
    dt = jnp.exp(ldt_ref[...])
    mag = jnp.exp(lr * dt)
    a_re = mag * jnp.cos(li * dt)
    a_im = mag * jnp.sin(li * dt)
    den = lr * lr + li * li
    nr = a_re - 1.0
    ni = a_im
    cr = (nr * lr + ni * li) / den
    ci = (ni * lr - nr * li) / den
    br = br_ref[...]
    bi = bi_ref[...]
    are_ref[...] = a_re
    aim_ref[...] = a_im
    bbr_ref[...] = cr * br - ci * bi
    bbi_ref[...] = cr * bi + ci * br


def _ssm_prep(lam_re, lam_im, log_dt, b_re, b_im):
    d, g, p = lam_re.shape
    h = b_re.shape[-1]
    shp = (d * g, h, p)
    bc = lambda a: jnp.broadcast_to(a.reshape(d * g, 1, -1), shp).astype(F32)
    tr = lambda a: jnp.swapaxes(a, -1, -2).reshape(shp).astype(F32)
    outs = pl.pallas_call(
        _ssm_prep_kernel,
        out_shape=[jax.ShapeDtypeStruct(shp, F32)] * 4,
        name="ssm_prep",
    )(bc(lam_re), bc(lam_im), bc(log_dt[..., None]), tr(b_re), tr(b_im))
    a_re, a_im, bb_re, bb_im = outs
    return (a_re[:, 0, :].reshape(d, g, p), a_im[:, 0, :].reshape(d, g, p),
            bb_re.reshape(d, g, h, p), bb_im.reshape(d, g, h, p))


def _pair_rms_norm(t, gain2):
    lane = lax.broadcasted_iota(jnp.int32, t.shape, 1)
    lo = lane < HEAD_DIM
    sq = t * t
    s_lo = jnp.sum(jnp.where(lo, sq, 0.0), axis=-1, keepdims=True)
    s_hi = jnp.sum(jnp.where(lo, 0.0, sq), axis=-1, keepdims=True)
    r_lo = lax.rsqrt(s_lo * (1.0 / HEAD_DIM) + RMS_EPS)
    r_hi = lax.rsqrt(s_hi * (1.0 / HEAD_DIM) + RMS_EPS)
    return t * jnp.where(lo, r_lo, r_hi) * gain2


def _half_select(t, kv_head):
    lane = lax.broadcasted_iota(jnp.int32, t.shape, 1)
    keep = (lane < HEAD_DIM) if kv_head == 0 else (lane >= HEAD_DIM)
    return jnp.where(keep, t, 0.0)


def _project(x, g_ref, w_ref, aqn_ref, akn_ref, mqn_ref, mkn_ref,
             u_ref, sg_ref, aq_ref, ag_ref, mq_ref, mqs_ref, mg_ref, akv_ref, mkv_ref):
    def lanes(j):
        return slice(j * LANES, (j + 1) * LANES)

    half = x.shape[0] // 2
    for r in range(2):
        rs = slice(r * half, (r + 1) * half)
        xr = x[rs]
        ms = jnp.mean(xr * xr, axis=-1, keepdims=True)
        h = (xr * lax.rsqrt(ms + RMS_EPS) * g_ref[...]).astype(BF16)
        chunks = {}

        def tile(col):
            c, k = divmod(col, PROJ_CHUNK)
            if c not in chunks:
                chunks[c] = jnp.dot(h, w_ref[:, c * PROJ_CHUNK:(c + 1) * PROJ_CHUNK],
                                    preferred_element_type=F32)
            return chunks[c][:, k:k + LANES]

        for off, kn_ref, kv_ref in ((OFF_MKV, mkn_ref, mkv_ref), (OFF_AKV, akn_ref, akv_ref)):
            kv_ref[0, rs, lanes(0)] = _pair_rms_norm(tile(off), kn_ref[...]).astype(BF16)
            kv_ref[0, rs, lanes(1)] = tile(off + KV_WIDTH).astype(BF16)

        qsum = jnp.zeros((half, LANES), F32)
        for j in reversed(range(N_Q_TILES)):
            mg_ref[0, rs, lanes(j)] = _silu(tile(OFF_MG + j * LANES)).astype(BF16)
            mqn = _pair_rms_norm(tile(OFF_MQ + j * LANES), mqn_ref[...])
            qsum = qsum + mqn
            mq_ref[0, rs, lanes(j)] = (mqn * (ATTN_SCALE * LOG2E)).astype(BF16)
        mqs_ref[0, rs, :] = qsum
        for j in reversed(range(N_Q_TILES)):
            ag_ref[0, rs, lanes(j)] = _silu(tile(OFF_AG + j * LANES)).astype(BF16)
            aq_ref[0, rs, lanes(j)] = (_pair_rms_norm(tile(OFF_AQ + j * LANES), aqn_ref[...])
                                       * (ATTN_SCALE * LOG2E)).astype(BF16)

        for j in range(SSM_WIDTH // LANES):
            sg_ref[0, rs, lanes(j)] = _silu(tile(OFF_SG + j * LANES)).astype(BF16)
            u_ref[0, rs, lanes(j)] = tile(OFF_U + j * LANES).astype(BF16)


def _residual_out(x_ref, ys_ref, ya_ref, ym_ref, wo_ref):
    y = jnp.concatenate([ys_ref[0], ya_ref[0], ym_ref[0]], axis=1)
    return x_ref[0] + jnp.dot(y, wo_ref[...], preferred_element_type=F32)


def _inproj_kernel(x_ref, *refs):
    _project(x_ref[0], *refs)


def _outproj_kernel(x_ref, ys_ref, ya_ref, ym_ref, wo_ref, o_ref):
    o_ref[0] = _residual_out(x_ref, ys_ref, ya_ref, ym_ref, wo_ref)


def _outin_kernel(x_ref, ys_ref, ya_ref, ym_ref, wo_ref, *refs):
    proj_in, proj_out = refs[:6], refs[7:]
    x = _residual_out(x_ref, ys_ref, ya_ref, ym_ref, wo_ref)
    refs[6][0] = x
    _project(x, *proj_in, *proj_out)


def _layer_edge(x, branch_outs, params, out_layer, in_layer, tiles):
    b, l, d = x.shape
    fused = out_layer is not None and in_layer is not None
    tl = tiles["fused_proj_rows" if fused else "proj_rows"]
    row = lambda width: pl.BlockSpec((1, tl, width), lambda i, t: (i, t, 0))
    shape = lambda width, dtype: jax.ShapeDtypeStruct((b, l, width), dtype)

    def layer_block(a, layer):
        return pl.BlockSpec((None,) + a.shape[1:], lambda i, t: (layer,) + (0,) * (a.ndim - 1),
                            pipeline_mode=pl.Buffered(1))

    args, in_specs, out_specs, out_shape = [x], [row(d)], [], []
    if out_layer is not None:
        args += list(branch_outs) + [params["w_out"]]
        in_specs += [row(SSM_WIDTH), row(Q_WIDTH), row(Q_WIDTH),
                     layer_block(params["w_out"], out_layer)]
        out_specs.append(row(d))
        out_shape.append(shape(d, F32))
    if in_layer is not None:
        proj_params = [params["norm_g"], params["w_in"]] + params["head_norms"]
        args += proj_params
        in_specs += [layer_block(a, in_layer) for a in proj_params]
        out_specs += [row(SSM_WIDTH), row(SSM_WIDTH), row(Q_WIDTH), row(Q_WIDTH), row(Q_WIDTH),
                      row(LANES), row(Q_WIDTH), row(2 * KV_WIDTH), row(2 * KV_WIDTH)]
        out_shape += [shape(SSM_WIDTH, BF16), shape(SSM_WIDTH, BF16), shape(Q_WIDTH, BF16),
                      shape(Q_WIDTH, BF16), shape(Q_WIDTH, BF16), shape(LANES, F32),
                      shape(Q_WIDTH, BF16), shape(2 * KV_WIDTH, BF16), shape(2 * KV_WIDTH, BF16)]
    body, name = {(False, True): (_inproj_kernel, "inproj"),
                  (True, False): (_outproj_kernel, "outproj"),
                  (True, True): (_outin_kernel, "outin")}[(out_layer is not None, in_layer is not None)]
    outs = pl.pallas_call(
        body,
        grid=(b, l // tl),
        in_specs=in_specs,
        out_specs=out_specs,
        out_shape=out_shape,
        compiler_params=_cparams("parallel", "parallel"),
        name=name,
    )(*args)
    return outs


ROW_PAD = 4
N_SLABS = 2 * N_STATE // LANES
RE_SLABS = N_SLABS // 2
N_STAGES = 3


def _ssm_kernel(nb, tc, u_ref, u2_ref, sg2_ref, bmat_ref, are_ref, aim_ref, cmat_ref,
                d_ref, gw_ref, gb_ref, y_ref, upad_a, upad_c, ypad_s, st_s, *slots):
    pitch = tc + ROW_PAD
    i = pl.program_id(0)

    @pl.when(i == 0)
    def _():
        st_s[...] = jnp.zeros_like(st_s)
        upad_a[...] = jnp.zeros_like(upad_a)
        upad_c[...] = jnp.zeros_like(upad_c)
        for s in slots:
            s[...] = jnp.zeros_like(s)

    pieces = RE_SLABS
    steps = tc // pieces

    def stages(in_s, scan_s, out_s):
        for b in range(nb):
            upad_a[pl.ds(b * pitch, tc), :] = u_ref[b].astype(F32)
            upad_c[pl.ds(b * pitch, tc), :] = u2_ref[b].astype(F32)
        u_in = upad_a[...].astype(BF16)
        a_re = [jnp.broadcast_to(are_ref[:, j * LANES:(j + 1) * LANES], (nb, LANES))
                for j in range(RE_SLABS)]
        a_im = [jnp.broadcast_to(aim_ref[:, j * LANES:(j + 1) * LANES], (nb, LANES))
                for j in range(RE_SLABS)]
        state = [st_s[j] for j in range(N_SLABS)]
        y = d_ref[...] * upad_c[...]
        for k in range(pieces):
            bu = jnp.dot(u_in, bmat_ref[k], preferred_element_type=F32)
            in_s[k] = bu[:, 0:LANES]
            in_s[RE_SLABS + k] = bu[:, LANES:2 * LANES]
            for t in range(k * steps, (k + 1) * steps):
                rows = pl.ds(t, nb, stride=pitch)
                for j in range(RE_SLABS):
                    xr, xi = state[j], state[RE_SLABS + j]
                    nr = a_re[j] * xr - a_im[j] * xi + scan_s[j, rows, :]
                    ni = a_re[j] * xi + a_im[j] * xr + scan_s[RE_SLABS + j, rows, :]
                    scan_s[j, rows, :] = nr
                    scan_s[RE_SLABS + j, rows, :] = ni
                    state[j], state[RE_SLABS + j] = nr, ni
            xs = jnp.concatenate([out_s[k], out_s[RE_SLABS + k]], axis=1).astype(BF16)
            y = y + jnp.dot(xs, cmat_ref[k], preferred_element_type=F32)
        for j in range(N_SLABS):
            st_s[j] = state[j]
        y = jax.nn.gelu(y)
        z = jnp.dot(y.astype(BF16), gw_ref[...], preferred_element_type=F32) + gb_ref[...]
        ypad_s[...] = y * jax.nn.sigmoid(z)
        for b in range(nb):
            y_ref[b] = (ypad_s[pl.ds(b * pitch, tc), :] * sg2_ref[b].astype(F32)).astype(BF16)

    for r in range(N_STAGES):
        @pl.when(lax.rem(i, N_STAGES) == r)
        def _(r=r):
            stages(slots[r], slots[(r + 2) % N_STAGES], slots[(r + 1) % N_STAGES])


def _ssm(u, sg, ssm_params, layer, tc):
    nb, l, _ = u.shape
    rows = nb * (tc + ROW_PAD)
    n_chunks = l // tc
    chunk = lambda lag: pl.BlockSpec(
        (nb, tc, SSM_WIDTH), lambda i: (0, jnp.clip(i - lag, 0, n_chunks - 1), 0))
    full = lambda a: pl.BlockSpec((None,) + a.shape[1:], lambda i: (layer,) + (0,) * (a.ndim - 1))
    slab = pltpu.VMEM((N_SLABS, rows, LANES), F32)
    return pl.pallas_call(
        functools.partial(_ssm_kernel, nb, tc),
        grid=(n_chunks + N_STAGES - 1,),
        in_specs=[chunk(0), chunk(2), chunk(2)] + [full(a) for a in ssm_params],
        out_specs=chunk(2),
        out_shape=jax.ShapeDtypeStruct((nb, l, SSM_WIDTH), BF16),
        scratch_shapes=[pltpu.VMEM((rows, SSM_WIDTH), F32),
                        pltpu.VMEM((rows, SSM_WIDTH), F32),
                        pltpu.VMEM((rows, SSM_WIDTH), F32),
                        pltpu.VMEM((N_SLABS, nb, LANES), F32),
                        slab, slab, slab],
        compiler_params=_cparams("arbitrary"),
        name="ssm",
    )(u, u, sg, *ssm_params)


def _sum_row(n_keys):
    r = lax.broadcasted_iota(jnp.int32, (SUM_ROWS, n_keys), 0)
    return jnp.where(r == 0, 1.0, 0.0).astype(BF16)


def _prep_keys_values(kv_ref, ks_s, vs_s, seq):
    means = []
    for c in range(seq // PREP_ROWS):
        rows = slice(c * PREP_ROWS, (c + 1) * PREP_ROWS)
        kvb = kv_ref[0, rows, :].astype(F32)
        kf = kvb[:, 0:KV_WIDTH]
        means.append(jnp.mean(kf, axis=0, keepdims=True))
        ks_s[0, rows, :] = _half_select(kf, 0).astype(BF16)
        ks_s[1, rows, :] = _half_select(kf, 1).astype(BF16)
        vs_s[:, rows] = jnp.transpose(kvb[:, KV_WIDTH:2 * KV_WIDTH]).astype(BF16)
    return means


def _stacked_keys(ks_s, lo, hi):
    return jnp.concatenate([ks_s[0, lo:hi, :], ks_s[1, lo:hi, :]], axis=0)


def _weighted_values(vs_s, lo, hi, p, sum_row):
    nk = hi - lo
    outs, sums = [], []
    for h in range(N_KV_HEADS):
        v_t = jnp.concatenate([vs_s[h * HEAD_DIM:(h + 1) * HEAD_DIM, lo:hi], sum_row], axis=0)
        pv = jnp.dot(v_t, p[h * nk:(h + 1) * nk], preferred_element_type=F32)
        outs.append(pv[0:HEAD_DIM])
        sums.append(pv[HEAD_DIM:HEAD_DIM + 1])
    return outs, sums


def _queries_t(q_ref, rows):
    return jnp.concatenate(
        [jnp.transpose(q_ref[0, rows, j * LANES:(j + 1) * LANES]) for j in range(N_Q_TILES)], axis=1)


def _store_gated(o_ref, g_ref, rows, o_t, width):
    for j in range(N_Q_TILES):
        cols = slice(j * LANES, (j + 1) * LANES)
        o = jnp.transpose(o_t[:, j * width:(j + 1) * width]) * g_ref[0, rows, cols].astype(F32)
        o_ref[0, rows, cols] = o.astype(BF16)


_ATTN_SCRATCH = lambda seq: [pltpu.VMEM((N_KV_HEADS, seq, KV_WIDTH), BF16),
                             pltpu.VMEM((KV_WIDTH, seq), BF16)]


def _swa_kernel(nblk, layer, shifted, sink_ref, q_ref, g_ref, kv_ref, o_ref, ks_s, vs_s):
    w = SWA_WINDOW
    nq = N_Q_TILES * w
    _prep_keys_values(kv_ref, ks_s, vs_s, nblk * w)
    lane = lax.broadcasted_iota(jnp.int32, (1, nq), 1)
    sink2 = []
    for h in range(N_KV_HEADS):
        a = [sink_ref[layer, j + N_Q_TILES * h] * LOG2E for j in range(N_Q_TILES)]
        sink2.append(jnp.where(lane < w, a[0], jnp.where(lane < 2 * w, a[1], a[2])))

    def window_mask(n_keys):
        kpos = (lax.broadcasted_iota(jnp.int32, (2 * n_keys, nq), 0) & (n_keys - 1)) - (n_keys - w)
        qpos = lax.broadcasted_iota(jnp.int32, (2 * n_keys, nq), 1) & (w - 1)
        rel = qpos - kpos
        return (rel >= 0) & (rel < w)

    masks = {w: window_mask(w), 2 * w: window_mask(2 * w)}
    sums = {w: _sum_row(w), 2 * w: _sum_row(2 * w)}

    def scores(n):
        lo = max(n - 1, 0) * w
        hi = (n + 1) * w
        q_t = _queries_t(q_ref, slice(n * w, (n + 1) * w))
        return jnp.where(masks[hi - lo], jnp.dot(_stacked_keys(ks_s, lo, hi), q_t,
                                                 preferred_element_type=F32), -jnp.inf)

    s_next = scores(0)
    for n in range(nblk):
        rows = slice(n * w, (n + 1) * w)
        lo = max(n - 1, 0) * w
        hi = (n + 1) * w
        nk = hi - lo
        s = s_next
        if n + 1 < nblk:
            s_next = scores(n + 1)
        if shifted:
            m = [jnp.maximum(jnp.max(s[h * nk:(h + 1) * nk], axis=0, keepdims=True), sink2[h])
                 for h in range(N_KV_HEADS)]
            p = jnp.concatenate([jnp.exp2(s[h * nk:(h + 1) * nk] - m[h])
                                 for h in range(N_KV_HEADS)], axis=0).astype(BF16)
            sink_p = [jnp.exp2(sink2[h] - m[h]) for h in range(N_KV_HEADS)]
        else:
            p = jnp.exp2(s).astype(BF16)
            sink_p = [jnp.exp2(sink2[h]) for h in range(N_KV_HEADS)]
        outs, sums_p = _weighted_values(vs_s, lo, hi, p, sums[nk])
        o_t = jnp.concatenate([outs[h] * (1.0 / (sums_p[h] + sink_p[h]))
                               for h in range(N_KV_HEADS)], axis=0)
        _store_gated(o_ref, g_ref, rows, o_t, w)


def _swa(q, g, kv, sink, layer, shifted):
    b, l, _ = q.shape
    qspec = pl.BlockSpec((1, l, Q_WIDTH), lambda i: (i, 0, 0))
    return pl.pallas_call(
        functools.partial(_swa_kernel, l // SWA_WINDOW, layer, shifted),
        grid=(b,),
        in_specs=[pl.BlockSpec(memory_space=pltpu.SMEM),
                  qspec, qspec,
                  pl.BlockSpec((1, l, 2 * KV_WIDTH), lambda i: (i, 0, 0))],
        out_specs=qspec,
        out_shape=jax.ShapeDtypeStruct((b, l, Q_WIDTH), BF16),
        scratch_shapes=_ATTN_SCRATCH(l),
        compiler_params=_cparams("parallel"),
        name="swa",
    )(sink, q, g, kv)


def _moba_kernel(nblk, shifted, q_ref, qs_ref, g_ref, kv_ref, o_ref, ks_s, vs_s):
    blk = MOBA_BLOCK
    assert blk == PREP_ROWS
    nq = N_Q_TILES * blk
    kmean = jnp.concatenate(_prep_keys_values(kv_ref, ks_s, vs_s, nblk * blk), axis=0)
    sum_row = _sum_row(blk)
    if not shifted:
        r = lax.broadcasted_iota(jnp.int32, (2 * blk, LANES), 0)
        c = lax.broadcasted_iota(jnp.int32, (2 * blk, LANES), 1)
        head_cols = jnp.where(((c == 0) & (r < blk)) | ((c == 1) & (r >= blk)), 1.0, 0.0).astype(BF16)
        bias_row = lax.broadcasted_iota(jnp.int32, (SUM_ROWS, nq), 0)
        bias_pad = jnp.zeros((LANES - SUM_ROWS, nq), BF16)

    blk_id = lax.broadcasted_iota(jnp.int32, (nblk, blk), 0)
    ki = lax.broadcasted_iota(jnp.int32, (2 * blk, nq), 0) & (blk - 1)
    qi = lax.broadcasted_iota(jnp.int32, (2 * blk, nq), 1) & (blk - 1)
    causal = ki <= qi

    def prepare(i):
        rows = slice(i * blk, (i + 1) * blk)
        q_t = _queries_t(q_ref, rows)
        qsum = qs_ref[0, rows, :]

        past = blk_id < i
        sel = []
        for h in range(N_KV_HEADS):
            gate = lax.dot_general(_half_select(kmean, h), _half_select(qsum, h),
                                   (((1,), (1,)), ((), ())),
                                   precision=lax.Precision.HIGHEST,
                                   preferred_element_type=F32)
            gate = jnp.where(past, gate, -jnp.inf)
            rank = jnp.zeros((nblk, blk), jnp.int32)
            for n2 in range(nblk):
                other = gate[n2:n2 + 1, :]
                ahead = (other > gate) | ((other == gate) & (n2 < blk_id))
                rank = rank + ahead.astype(jnp.int32)
            chosen = jnp.where((rank < MOBA_TOPK) & past, 1.0, 0.0)
            sel.append(jnp.concatenate([chosen] * N_Q_TILES, axis=1))
        kst = _stacked_keys(ks_s, i * blk, (i + 1) * blk)
        s_own = jnp.where(causal, jnp.dot(kst, q_t, preferred_element_type=F32), -jnp.inf)
        return q_t, sel, s_own

    prepared = prepare(0)
    for i in range(nblk):
        rows = slice(i * blk, (i + 1) * blk)
        q_t, sel, s = prepared
        if i + 1 < nblk and i == 0:
            prepared = prepare(i + 1)

        def past_scores(n, q_t=q_t, sel=sel):
            kst = _stacked_keys(ks_s, n * blk, (n + 1) * blk)
            if shifted:
                return jnp.dot(kst, q_t, preferred_element_type=F32)
            ok = [sel[h][n:n + 1, :] > 0.5 for h in range(N_KV_HEADS)]
            bias = jnp.where(bias_row == 0, jnp.where(ok[0], 0.0, MASK_BIAS),
                             jnp.where(bias_row == 1, jnp.where(ok[1], 0.0, MASK_BIAS), 0.0))
            q_ext = jnp.concatenate([q_t, bias.astype(BF16), bias_pad], axis=0)
            k_ext = jnp.concatenate([kst, head_cols], axis=1)
            return jnp.dot(k_ext, q_ext, preferred_element_type=F32)

        s_next = past_scores(0) if i > 0 else None
        if shifted:
            m = [jnp.max(s[h * blk:(h + 1) * blk], axis=0, keepdims=True)
                 for h in range(N_KV_HEADS)]
            p = jnp.concatenate([jnp.exp2(s[h * blk:(h + 1) * blk] - m[h])
                                 for h in range(N_KV_HEADS)], axis=0).astype(BF16)
        else:
            p = jnp.exp2(s).astype(BF16)
        acc, den = _weighted_values(vs_s, i * blk, (i + 1) * blk, p, sum_row)

        for n in range(i):
            s = s_next
            if n + 1 < i:
                s_next = past_scores(n + 1)
            elif i + 1 < nblk:
                prepared = prepare(i + 1)
            if shifted:
                ok = [sel[h][n:n + 1, :] > 0.5 for h in range(N_KV_HEADS)]
                alpha = []
                parts = []
                for h in range(N_KV_HEADS):
                    s_h = s[h * blk:(h + 1) * blk]
                    m_new = jnp.where(ok[h], jnp.maximum(m[h], jnp.max(s_h, axis=0, keepdims=True)),
                                      m[h])
                    alpha.append(jnp.exp2(m[h] - m_new))
                    parts.append(jnp.exp2(s_h - jnp.where(ok[h], m_new, jnp.inf)))
                    m[h] = m_new
                p = jnp.concatenate(parts, axis=0).astype(BF16)
                outs, sums_p = _weighted_values(vs_s, n * blk, (n + 1) * blk, p, sum_row)
                acc = [acc[h] * alpha[h] + outs[h] for h in range(N_KV_HEADS)]
                den = [den[h] * alpha[h] + sums_p[h] for h in range(N_KV_HEADS)]
            else:
                p = jnp.exp2(s).astype(BF16)
                outs, sums_p = _weighted_values(vs_s, n * blk, (n + 1) * blk, p, sum_row)
                acc = [acc[h] + outs[h] for h in range(N_KV_HEADS)]
                den = [den[h] + sums_p[h] for h in range(N_KV_HEADS)]

        o_t = jnp.concatenate([acc[h] * (1.0 / den[h]) for h in range(N_KV_HEADS)], axis=0)
        _store_gated(o_ref, g_ref, rows, o_t, blk)


def _moba(q, qsum, g, kv, shifted):
    b, l, _ = q.shape
    qspec = pl.BlockSpec((1, l, Q_WIDTH), lambda i: (i, 0, 0))
    return pl.pallas_call(
        functools.partial(_moba_kernel, l // MOBA_BLOCK, shifted),
        grid=(b,),
        in_specs=[qspec,
                  pl.BlockSpec((1, l, LANES), lambda i: (i, 0, 0)),
                  qspec,
                  pl.BlockSpec((1, l, 2 * KV_WIDTH), lambda i: (i, 0, 0))],
        out_specs=qspec,
        out_shape=jax.ShapeDtypeStruct((b, l, Q_WIDTH), BF16),
        scratch_shapes=_ATTN_SCRATCH(l),
        compiler_params=_cparams("parallel"),
        name="moba",
    )(q, qsum, g, kv)


def _head_tiles(w):
    lead = w.shape[:-1]
    w = w.reshape(*lead, N_KV_HEADS, N_Q_TILES, HEAD_DIM)
    return jnp.swapaxes(w, -3, -2).reshape(*lead, Q_WIDTH)


def _permute_w_in(w):
    s, q, k = SSM_WIDTH, Q_WIDTH, KV_WIDTH
    bounds = [0]
    for n in (s, s, q, k, k, q, q, k, k, q):
        bounds.append(bounds[-1] + n)
    s_u, s_g, a_q, a_k, a_v, a_g, m_q, m_k, m_v, m_g = [
        w[..., bounds[i]:bounds[i + 1]] for i in range(10)]
    return jnp.concatenate(
        [s_u, s_g, _head_tiles(a_q), _head_tiles(a_g), _head_tiles(m_q), _head_tiles(m_g),
         a_k, a_v, m_k, m_v], axis=-1).astype(BF16)


def _permute_w_out(w):
    s, q = SSM_WIDTH, Q_WIDTH
    rows_t = lambda part: jnp.swapaxes(_head_tiles(jnp.swapaxes(part, -1, -2)), -1, -2)
    return jnp.concatenate([w[:, 0:s], rows_t(w[:, s:s + q]), rows_t(w[:, s + q:])],
                           axis=1).astype(BF16)


def _block_diag_in(bb):
    d, g, h, p = bb.shape
    eye = jnp.eye(g, dtype=bb.dtype)[None, :, None, :, None]
    return (bb[:, :, :, None, :] * eye).reshape(d, g * h, g * p)


def _block_diag_out(c):
    d, g, h, p = c.shape
    eye = jnp.eye(g, dtype=c.dtype)[None, :, None, :, None]
    return (jnp.swapaxes(c, -1, -2)[:, :, :, None, :] * eye).reshape(d, g * p, g * h)


def _logit_bound(q_gain, k_gain):
    return ((HEAD_DIM * ATTN_SCALE * LOG2E) * jnp.max(jnp.abs(q_gain), axis=-1)
            * jnp.max(jnp.abs(k_gain), axis=-1))


def kernel(x, norm_g, w_in, ssm_lam_re, ssm_lam_im, ssm_log_dt, ssm_b_re, ssm_b_im,
           ssm_c_re, ssm_c_im, ssm_d, ssm_glu_w, ssm_glu_b, swa_q_norm, swa_k_norm,
           swa_sink, moba_q_norm, moba_k_norm, w_out):
    b, l, d = x.shape
    depth = norm_g.shape[0]
    tiles = _tiles(l)
    a_re, a_im, bb_re, bb_im = _ssm_prep(ssm_lam_re, ssm_lam_im, ssm_log_dt, ssm_b_re, ssm_b_im)
    two = lambda v: jnp.concatenate([v, v], axis=-1).reshape(depth, 1, LANES).astype(F32)
    edge_params = {
        "norm_g": norm_g.reshape(depth, 1, d).astype(F32),
        "w_in": _permute_w_in(w_in),
        "w_out": _permute_w_out(w_out),
        "head_norms": [two(v) for v in (swa_q_norm, swa_k_norm, moba_q_norm, moba_k_norm)],
    }
    by_slab_cols = lambda m: m.reshape(depth, SSM_WIDTH, RE_SLABS, LANES)
    bmat = jnp.concatenate([by_slab_cols(_block_diag_in(bb_re)), by_slab_cols(_block_diag_in(bb_im))],
                           axis=-1).transpose(0, 2, 1, 3).astype(BF16)
    by_slab_rows = lambda m: m.reshape(depth, RE_SLABS, LANES, SSM_WIDTH)
    cmat = jnp.concatenate([by_slab_rows(_block_diag_out(ssm_c_re)),
                            -by_slab_rows(_block_diag_out(ssm_c_im))], axis=2).astype(BF16)
    ssm_params = (
        bmat, a_re.reshape(depth, 1, N_STATE), a_im.reshape(depth, 1, N_STATE), cmat,
        ssm_d.reshape(depth, 1, SSM_WIDTH).astype(F32), ssm_glu_w.astype(BF16),
        ssm_glu_b.reshape(depth, 1, SSM_WIDTH).astype(F32))
    sink = swa_sink.astype(F32)
    small = ((_logit_bound(swa_q_norm, swa_k_norm) <= LOGIT_BOUND)
             & (jnp.max(jnp.abs(sink), axis=-1) * LOG2E <= LOGIT_BOUND)
             & (_logit_bound(moba_q_norm, moba_k_norm) <= LOGIT_BOUND))

    def attention(layer, shifted, a_q, a_g, a_kv, m_q, m_qsum, m_g, m_kv, sink):
        return (_swa(a_q, a_g, a_kv, sink, layer, shifted), _moba(m_q, m_qsum, m_g, m_kv, shifted))

    proj = _layer_edge(x, None, edge_params, None, 0, tiles)
    for layer in range(depth):
        s_u, s_g, a_q, a_g, m_q, m_qsum, m_g, a_kv, m_kv = proj
        y_swa, y_moba = lax.cond(small[layer], functools.partial(attention, layer, False),
                                 functools.partial(attention, layer, True),
                                 a_q, a_g, a_kv, m_q, m_qsum, m_g, m_kv, sink)
        branch_outs = (_ssm(s_u, s_g, ssm_params, layer, tiles["scan_steps"]), y_swa, y_moba)
        nxt = layer + 1 if layer + 1 < depth else None
        x, *proj = _layer_edge(x, branch_outs, edge_params, layer, nxt, tiles)
    return x
```

```python
import functools
import math

import jax
import jax.numpy as jnp
from jax import lax
from jax.experimental import pallas as pl
from jax.experimental.pallas import tpu as pltpu

F32 = jnp.float32
BF16 = jnp.bfloat16

HEAD_DIM = 64
SSM_WIDTH = 256
SSM_GROUPS = 16
SSM_STATE = 64
N_STATE = SSM_GROUPS * SSM_STATE
N_Q_HEADS = 6
N_KV_HEADS = 2
Q_WIDTH = N_Q_HEADS * HEAD_DIM
KV_WIDTH = N_KV_HEADS * HEAD_DIM
SWA_WINDOW = 128
MOBA_BLOCK = 256
MOBA_TOPK = 3
RMS_EPS = 1e-6
ATTN_SCALE = HEAD_DIM ** -0.5
LOG2E = math.log2(math.e)
LANES = 128
N_Q_TILES = Q_WIDTH // LANES
VMEM_LIMIT = 56 * 1024 * 1024
SUM_ROWS = 16
PREP_ROWS = 256
OFF_U, OFF_SG = 0, SSM_WIDTH
OFF_AQ = 2 * SSM_WIDTH
OFF_AG = OFF_AQ + Q_WIDTH
OFF_MQ = OFF_AG + Q_WIDTH
OFF_MG = OFF_MQ + Q_WIDTH
OFF_AKV = OFF_MG + Q_WIDTH
OFF_MKV = OFF_AKV + 2 * KV_WIDTH
PROJ_CHUNK = 512
LOGIT_BOUND = 60.0
MASK_BIAS = -300.0


def _tiles(seq):
    return {"proj_rows": min(1024, seq), "fused_proj_rows": min(1024, seq), "scan_steps": min(64, seq)}


def _cparams(*sem):
    return pltpu.CompilerParams(dimension_semantics=sem, vmem_limit_bytes=VMEM_LIMIT)


def _silu(t):
    return t * jax.nn.sigmoid(t)


def _ssm_prep_kernel(lr_ref, li_ref, ldt_ref, br_ref, bi_ref,
                     are_ref, aim_ref, bbr_ref, bbi_ref):
    lr = lr_ref[...]
    li = li_ref[...]
    dt = jnp.exp(ldt_ref[...])
    mag = jnp.exp(lr * dt)
    a_re = mag * jnp.cos(li * dt)
    a_im = mag * jnp.sin(li * dt)
    den = lr * lr + li * li
    nr = a_re - 1.0
    ni = a_im
    cr = (nr * lr + ni * li) / den
    ci = (ni * lr - nr * li) / den
    br = br_ref[...]
    bi = bi_ref[...]
    are_ref[...] = a_re
    aim_ref[...] = a_im
    bbr_ref[...] = cr * br - ci * bi
    bbi_ref[...] = cr * bi + ci * br


def _ssm_prep(lam_re, lam_im, log_dt, b_re, b_im):
    d, g, p = lam_re.shape
    h = b_re.shape[-1]
    shp = (d * g, h, p)
    bc = lambda a: jnp.broadcast_to(a.reshape(d * g, 1, -1), shp).astype(F32)
    tr = lambda a: jnp.swapaxes(a, -1, -2).reshape(shp).astype(F32)
    outs = pl.pallas_call(
        _ssm_prep_kernel,
        out_shape=[jax.ShapeDtypeStruct(shp, F32)] * 4,
        name="ssm_prep",
    )(bc(lam_re), bc(lam_im), bc(log_dt[..., None]), tr(b_re), tr(b_im))
    a_re, a_im, bb_re, bb_im = outs
    return (a_re[:, 0, :].reshape(d, g, p), a_im[:, 0, :].reshape(d, g, p),
            bb_re.reshape(d, g, h, p), bb_im.reshape(d, g, h, p))


def _pair_rms_norm(t, gain2):
    lane = lax.broadcasted_iota(jnp.int32, t.shape, 1)
    lo = lane < HEAD_DIM
    sq = t * t
    s_lo = jnp.sum(jnp.where(lo, sq, 0.0), axis=-1, keepdims=True)
    s_hi = jnp.sum(jnp.where(lo, 0.0, sq), axis=-1, keepdims=True)
    r_lo = lax.rsqrt(s_lo * (1.0 / HEAD_DIM) + RMS_EPS)
    r_hi = lax.rsqrt(s_hi * (1.0 / HEAD_DIM) + RMS_EPS)
    return t * jnp.where(lo, r_lo, r_hi) * gain2


def _half_select(t, kv_head):
    lane = lax.broadcasted_iota(jnp.int32, t.shape, 1)
    keep = (lane < HEAD_DIM) if kv_head == 0 else (lane >= HEAD_DIM)
    return jnp.where(keep, t, 0.0)


def _project(x, g_ref, w_ref, aqn_ref, akn_ref, mqn_ref, mkn_ref,
             u_ref, sg_ref, aq_ref, ag_ref, mq_ref, mqs_ref, mg_ref, akv_ref, mkv_ref):
    def lanes(j):
        return slice(j * LANES, (j + 1) * LANES)

    half = x.shape[0] // 2
    for r in range(2):
        rs = slice(r * half, (r + 1) * half)
        xr = x[rs]
        ms = jnp.mean(xr * xr, axis=-1, keepdims=True)
        h = (xr * lax.rsqrt(ms + RMS_EPS) * g_ref[...]).astype(BF16)
        chunks = {}

        def tile(col):
            c, k = divmod(col, PROJ_CHUNK)
            if c not in chunks:
                chunks[c] = jnp.dot(h, w_ref[:, c * PROJ_CHUNK:(c + 1) * PROJ_CHUNK],
                                    preferred_element_type=F32)
            return chunks[c][:, k:k + LANES]

        for off, kn_ref, kv_ref in ((OFF_MKV, mkn_ref, mkv_ref), (OFF_AKV, akn_ref, akv_ref)):
            kv_ref[0, rs, lanes(0)] = _pair_rms_norm(tile(off), kn_ref[...]).astype(BF16)
            kv_ref[0, rs, lanes(1)] = tile(off + KV_WIDTH).astype(BF16)

        qsum = jnp.zeros((half, LANES), F32)
        for j in reversed(range(N_Q_TILES)):
            mg_ref[0, rs, lanes(j)] = _silu(tile(OFF_MG + j * LANES)).astype(BF16)
            mqn = _pair_rms_norm(tile(OFF_MQ + j * LANES), mqn_ref[...])
            qsum = qsum + mqn
            mq_ref[0, rs, lanes(j)] = (mqn * (ATTN_SCALE * LOG2E)).astype(BF16)
        mqs_ref[0, rs, :] = qsum
        for j in reversed(range(N_Q_TILES)):
            ag_ref[0, rs, lanes(j)] = _silu(tile(OFF_AG + j * LANES)).astype(BF16)
            aq_ref[0, rs, lanes(j)] = (_pair_rms_norm(tile(OFF_AQ + j * LANES), aqn_ref[...])
                                       * (ATTN_SCALE * LOG2E)).astype(BF16)

        for j in range(SSM_WIDTH // LANES):
            sg_ref[0, rs, lanes(j)] = _silu(tile(OFF_SG + j * LANES)).astype(BF16)
            u_ref[0, rs, lanes(j)] = tile(OFF_U + j * LANES).astype(BF16)


def _residual_out(x_ref, ys_ref, ya_ref, ym_ref, wo_ref):
    y = jnp.concatenate([ys_ref[0], ya_ref[0], ym_ref[0]], axis=1)
    return x_ref[0] + jnp.dot(y, wo_ref[...], preferred_element_type=F32)


def _inproj_kernel(x_ref, *refs):
    _project(x_ref[0], *refs)


def _outproj_kernel(x_ref, ys_ref, ya_ref, ym_ref, wo_ref, o_ref):
    o_ref[0] = _residual_out(x_ref, ys_ref, ya_ref, ym_ref, wo_ref)


def _outin_kernel(x_ref, ys_ref, ya_ref, ym_ref, wo_ref, *refs):
    proj_in, proj_out = refs[:6], refs[7:]
    x = _residual_out(x_ref, ys_ref, ya_ref, ym_ref, wo_ref)
    refs[6][0] = x
    _project(x, *proj_in, *proj_out)


STREAM_BUFFERS = 3


def _outproj_pipelined(x, branch_outs, w_out, layer, tl):
    b, l, d = x.shape

    def stream(width, buffers):
        return pl.BlockSpec((1, tl, width), lambda i, t: (i, t, 0),
                            pipeline_mode=pl.Buffered(buffers))

    def outer(x_hbm, ys_hbm, ya_hbm, ym_hbm, w_hbm, o_hbm):
        pltpu.emit_pipeline(
            _outproj_kernel,
            grid=(b, l // tl),
            in_specs=[stream(d, STREAM_BUFFERS), stream(SSM_WIDTH, STREAM_BUFFERS),
                      stream(Q_WIDTH, STREAM_BUFFERS), stream(Q_WIDTH, STREAM_BUFFERS),
                      pl.BlockSpec((None,) + w_out.shape[1:], lambda i, t: (layer, 0, 0))],
            out_specs=[pl.BlockSpec((1, tl, d), lambda i, t: (i, t, 0))],
        )(x_hbm, ys_hbm, ya_hbm, ym_hbm, w_hbm, o_hbm)

    any_space = pl.BlockSpec(memory_space=pl.ANY)
    return pl.pallas_call(
        outer,
        in_specs=[any_space] * 5,
        out_specs=any_space,
        out_shape=jax.ShapeDtypeStruct((b, l, d), F32),
        compiler_params=pltpu.CompilerParams(vmem_limit_bytes=VMEM_LIMIT),
        name="outproj",
    )(x, *branch_outs, w_out)


def _layer_edge(x, branch_outs, params, out_layer, in_layer, tiles):
    b, l, d = x.shape
    fused = out_layer is not None and in_layer is not None
    tl = tiles["fused_proj_rows" if fused else "proj_rows"]
    row = lambda width: pl.BlockSpec((1, tl, width), lambda i, t: (i, t, 0))
    shape = lambda width, dtype: jax.ShapeDtypeStruct((b, l, width), dtype)

    def layer_block(a, layer):
        return pl.BlockSpec((None,) + a.shape[1:], lambda i, t: (layer,) + (0,) * (a.ndim - 1),
                            pipeline_mode=pl.Buffered(1))

    if out_layer is not None and in_layer is None:
        return [_outproj_pipelined(x, branch_outs, params["w_out"], out_layer, tl)]

    args, in_specs, out_specs, out_shape = [x], [row(d)], [], []
    if out_layer is not None:
        args += list(branch_outs) + [params["w_out"]]
        in_specs += [row(SSM_WIDTH), row(Q_WIDTH), row(Q_WIDTH),
                     layer_block(params["w_out"], out_layer)]
        out_specs.append(row(d))
        out_shape.append(shape(d, F32))
    if in_layer is not None:
        proj_params = [params["norm_g"], params["w_in"]] + params["head_norms"]
        args += proj_params
        in_specs += [layer_block(a, in_layer) for a in proj_params]
        out_specs += [row(SSM_WIDTH), row(SSM_WIDTH), row(Q_WIDTH), row(Q_WIDTH), row(Q_WIDTH),
                      row(LANES), row(Q_WIDTH), row(2 * KV_WIDTH), row(2 * KV_WIDTH)]
        out_shape += [shape(SSM_WIDTH, BF16), shape(SSM_WIDTH, BF16), shape(Q_WIDTH, BF16),
                      shape(Q_WIDTH, BF16), shape(Q_WIDTH, BF16), shape(LANES, F32),
                      shape(Q_WIDTH, BF16), shape(2 * KV_WIDTH, BF16), shape(2 * KV_WIDTH, BF16)]
    body, name = {(False, True): (_inproj_kernel, "inproj"),
                  (True, False): (_outproj_kernel, "outproj"),
                  (True, True): (_outin_kernel, "outin")}[(out_layer is not None, in_layer is not None)]
    outs = pl.pallas_call(
        body,
        grid=(b, l // tl),
        in_specs=in_specs,
        out_specs=out_specs,
        out_shape=out_shape,
        compiler_params=_cparams("parallel", "parallel"),
        name=name,
    )(*args)
    return outs


ROW_PAD = 4
N_SLABS = 2 * N_STATE // LANES
RE_SLABS = N_SLABS // 2
N_STAGES = 3


def _ssm_kernel(nb, tc, u_ref, u2_ref, sg2_ref, bmat_ref, are_ref, aim_ref, cmat_ref,
                d_ref, gw_ref, gb_ref, y_ref, upad_a, upad_c, ypad_s, st_s, *slots):
    pitch = tc + ROW_PAD
    i = pl.program_id(0)

    @pl.when(i == 0)
    def _():
        st_s[...] = jnp.zeros_like(st_s)
        upad_a[...] = jnp.zeros_like(upad_a)
        upad_c[...] = jnp.zeros_like(upad_c)
        for s in slots:
            s[...] = jnp.zeros_like(s)

    pieces = RE_SLABS
    steps = tc // pieces

    def stages(in_s, scan_s, out_s):
        for b in range(nb):
            upad_a[pl.ds(b * pitch, tc), :] = u_ref[b].astype(F32)
            upad_c[pl.ds(b * pitch, tc), :] = u2_ref[b].astype(F32)
        u_in = upad_a[...].astype(BF16)
        a_re = [jnp.broadcast_to(are_ref[:, j * LANES:(j + 1) * LANES], (nb, LANES))
                for j in range(RE_SLABS)]
        a_im = [jnp.broadcast_to(aim_ref[:, j * LANES:(j + 1) * LANES], (nb, LANES))
                for j in range(RE_SLABS)]
        state = [st_s[j] for j in range(N_SLABS)]
        y = d_ref[...] * upad_c[...]
        for k in range(pieces):
            bu = jnp.dot(u_in, bmat_ref[k], preferred_element_type=F32)
            in_s[k] = bu[:, 0:LANES]
            in_s[RE_SLABS + k] = bu[:, LANES:2 * LANES]
            for t in range(k * steps, (k + 1) * steps):
                rows = pl.ds(t, nb, stride=pitch)
                for j in range(RE_SLABS):
                    xr, xi = state[j], state[RE_SLABS + j]
                    nr = a_re[j] * xr - a_im[j] * xi + scan_s[j, rows, :]
                    ni = a_re[j] * xi + a_im[j] * xr + scan_s[RE_SLABS + j, rows, :]
                    scan_s[j, rows, :] = nr
                    scan_s[RE_SLABS + j, rows, :] = ni
                    state[j], state[RE_SLABS + j] = nr, ni
            xs = jnp.concatenate([out_s[k], out_s[RE_SLABS + k]], axis=1).astype(BF16)
            y = y + jnp.dot(xs, cmat_ref[k], preferred_element_type=F32)
        for j in range(N_SLABS):
            st_s[j] = state[j]
        y = jax.nn.gelu(y)
        z = jnp.dot(y.astype(BF16), gw_ref[...], preferred_element_type=F32) + gb_ref[...]
        ypad_s[...] = y * jax.nn.sigmoid(z)
        for b in range(nb):
            y_ref[b] = (ypad_s[pl.ds(b * pitch, tc), :] * sg2_ref[b].astype(F32)).astype(BF16)

    for r in range(N_STAGES):
        @pl.when(lax.rem(i, N_STAGES) == r)
        def _(r=r):
            stages(slots[r], slots[(r + 2) % N_STAGES], slots[(r + 1) % N_STAGES])


def _ssm(u, sg, ssm_params, layer, tc):
    nb, l, _ = u.shape
    rows = nb * (tc + ROW_PAD)
    n_chunks = l // tc
    chunk = lambda lag: pl.BlockSpec(
        (nb, tc, SSM_WIDTH), lambda i: (0, jnp.clip(i - lag, 0, n_chunks - 1), 0))
    full = lambda a: pl.BlockSpec((None,) + a.shape[1:], lambda i: (layer,) + (0,) * (a.ndim - 1))
    slab = pltpu.VMEM((N_SLABS, rows, LANES), F32)
    return pl.pallas_call(
        functools.partial(_ssm_kernel, nb, tc),
        grid=(n_chunks + N_STAGES - 1,),
        in_specs=[chunk(0), chunk(2), chunk(2)] + [full(a) for a in ssm_params],
        out_specs=chunk(2),
        out_shape=jax.ShapeDtypeStruct((nb, l, SSM_WIDTH), BF16),
        scratch_shapes=[pltpu.VMEM((rows, SSM_WIDTH), F32),
                        pltpu.VMEM((rows, SSM_WIDTH), F32),
                        pltpu.VMEM((rows, SSM_WIDTH), F32),
                        pltpu.VMEM((N_SLABS, nb, LANES), F32),
                        slab, slab, slab],
        compiler_params=_cparams("arbitrary"),
        name="ssm",
    )(u, u, sg, *ssm_params)


def _sum_row(n_keys):
    r = lax.broadcasted_iota(jnp.int32, (SUM_ROWS, n_keys), 0)
    return jnp.where(r == 0, 1.0, 0.0).astype(BF16)


def _prep_keys_values(kv_ref, ks_s, vs_s, seq):
    means = []
    for c in range(seq // PREP_ROWS):
        rows = slice(c * PREP_ROWS, (c + 1) * PREP_ROWS)
        kvb = kv_ref[0, rows, :].astype(F32)
        kf = kvb[:, 0:KV_WIDTH]
        means.append(jnp.mean(kf, axis=0, keepdims=True))
        ks_s[0, rows, :] = _half_select(kf, 0).astype(BF16)
        ks_s[1, rows, :] = _half_select(kf, 1).astype(BF16)
        vs_s[:, rows] = jnp.transpose(kvb[:, KV_WIDTH:2 * KV_WIDTH]).astype(BF16)
    return means


def _stacked_keys(ks_s, lo, hi):
    return jnp.concatenate([ks_s[0, lo:hi, :], ks_s[1, lo:hi, :]], axis=0)


def _weighted_values(vs_s, lo, hi, p, sum_row):
    nk = hi - lo
    outs, sums = [], []
    for h in range(N_KV_HEADS):
        v_t = jnp.concatenate([vs_s[h * HEAD_DIM:(h + 1) * HEAD_DIM, lo:hi], sum_row], axis=0)
        pv = jnp.dot(v_t, p[h * nk:(h + 1) * nk], preferred_element_type=F32)
        outs.append(pv[0:HEAD_DIM])
        sums.append(pv[HEAD_DIM:HEAD_DIM + 1])
    return outs, sums


def _queries_t(q_ref, rows):
    return jnp.concatenate(
        [jnp.transpose(q_ref[0, rows, j * LANES:(j + 1) * LANES]) for j in range(N_Q_TILES)], axis=1)


def _store_gated(o_ref, g_ref, rows, o_t, width):
    for j in range(N_Q_TILES):
        cols = slice(j * LANES, (j + 1) * LANES)
        o = jnp.transpose(o_t[:, j * width:(j + 1) * width]) * g_ref[0, rows, cols].astype(F32)
        o_ref[0, rows, cols] = o.astype(BF16)


_ATTN_SCRATCH = lambda seq: [pltpu.VMEM((N_KV_HEADS, seq, KV_WIDTH), BF16),
                             pltpu.VMEM((KV_WIDTH, seq), BF16)]


def _swa_kernel(nblk, layer, shifted, sink_ref, q_ref, g_ref, kv_ref, o_ref, ks_s, vs_s):
    w = SWA_WINDOW
    nq = N_Q_TILES * w
    _prep_keys_values(kv_ref, ks_s, vs_s, nblk * w)
    lane = lax.broadcasted_iota(jnp.int32, (1, nq), 1)
    sink2 = []
    for h in range(N_KV_HEADS):
        a = [sink_ref[layer, j + N_Q_TILES * h] * LOG2E for j in range(N_Q_TILES)]
        sink2.append(jnp.where(lane < w, a[0], jnp.where(lane < 2 * w, a[1], a[2])))

    def window_mask(n_keys):
        kpos = (lax.broadcasted_iota(jnp.int32, (2 * n_keys, nq), 0) & (n_keys - 1)) - (n_keys - w)
        qpos = lax.broadcasted_iota(jnp.int32, (2 * n_keys, nq), 1) & (w - 1)
        rel = qpos - kpos
        return (rel >= 0) & (rel < w)

    masks = {w: window_mask(w), 2 * w: window_mask(2 * w)}
    sums = {w: _sum_row(w), 2 * w: _sum_row(2 * w)}

    def scores(n):
        lo = max(n - 1, 0) * w
        hi = (n + 1) * w
        q_t = _queries_t(q_ref, slice(n * w, (n + 1) * w))
        return jnp.where(masks[hi - lo], jnp.dot(_stacked_keys(ks_s, lo, hi), q_t,
                                                 preferred_element_type=F32), -jnp.inf)

    s_next = scores(0)
    for n in range(nblk):
        rows = slice(n * w, (n + 1) * w)
        lo = max(n - 1, 0) * w
        hi = (n + 1) * w
        nk = hi - lo
        s = s_next
        if n + 1 < nblk:
            s_next = scores(n + 1)
        if shifted:
            m = [jnp.maximum(jnp.max(s[h * nk:(h + 1) * nk], axis=0, keepdims=True), sink2[h])
                 for h in range(N_KV_HEADS)]
            p = jnp.concatenate([jnp.exp2(s[h * nk:(h + 1) * nk] - m[h])
                                 for h in range(N_KV_HEADS)], axis=0).astype(BF16)
            sink_p = [jnp.exp2(sink2[h] - m[h]) for h in range(N_KV_HEADS)]
        else:
            p = jnp.exp2(s).astype(BF16)
            sink_p = [jnp.exp2(sink2[h]) for h in range(N_KV_HEADS)]
        outs, sums_p = _weighted_values(vs_s, lo, hi, p, sums[nk])
        o_t = jnp.concatenate([outs[h] * (1.0 / (sums_p[h] + sink_p[h]))
                               for h in range(N_KV_HEADS)], axis=0)
        _store_gated(o_ref, g_ref, rows, o_t, w)


def _swa(q, g, kv, sink, layer, shifted):
    b, l, _ = q.shape
    qspec = pl.BlockSpec((1, l, Q_WIDTH), lambda i: (i, 0, 0))
    return pl.pallas_call(
        functools.partial(_swa_kernel, l // SWA_WINDOW, layer, shifted),
        grid=(b,),
        in_specs=[pl.BlockSpec(memory_space=pltpu.SMEM),
                  qspec, qspec,
                  pl.BlockSpec((1, l, 2 * KV_WIDTH), lambda i: (i, 0, 0))],
        out_specs=qspec,
        out_shape=jax.ShapeDtypeStruct((b, l, Q_WIDTH), BF16),
        scratch_shapes=_ATTN_SCRATCH(l),
        compiler_params=_cparams("parallel"),
        name="swa",
    )(sink, q, g, kv)


def _moba_kernel(nblk, shifted, q_ref, qs_ref, g_ref, kv_ref, o_ref, ks_s, vs_s):
    blk = MOBA_BLOCK
    assert blk == PREP_ROWS
    nq = N_Q_TILES * blk
    kmean = jnp.concatenate(_prep_keys_values(kv_ref, ks_s, vs_s, nblk * blk), axis=0)
    sum_row = _sum_row(blk)
    if not shifted:
        r = lax.broadcasted_iota(jnp.int32, (2 * blk, LANES), 0)
        c = lax.broadcasted_iota(jnp.int32, (2 * blk, LANES), 1)
        head_cols = jnp.where(((c == 0) & (r < blk)) | ((c == 1) & (r >= blk)), 1.0, 0.0).astype(BF16)
        bias_row = lax.broadcasted_iota(jnp.int32, (SUM_ROWS, nq), 0)
        bias_pad = jnp.zeros((LANES - SUM_ROWS, nq), BF16)

    blk_id = lax.broadcasted_iota(jnp.int32, (nblk, blk), 0)
    ki = lax.broadcasted_iota(jnp.int32, (2 * blk, nq), 0) & (blk - 1)
    qi = lax.broadcasted_iota(jnp.int32, (2 * blk, nq), 1) & (blk - 1)
    causal = ki <= qi

    def prepare(i):
        rows = slice(i * blk, (i + 1) * blk)
        q_t = _queries_t(q_ref, rows)
        qsum = qs_ref[0, rows, :]

        past = blk_id < i
        sel = []
        for h in range(N_KV_HEADS):
            gate = lax.dot_general(_half_select(kmean, h), _half_select(qsum, h),
                                   (((1,), (1,)), ((), ())),
                                   precision=lax.Precision.HIGHEST,
                                   preferred_element_type=F32)
            gate = jnp.where(past, gate, -jnp.inf)
            rank = jnp.zeros((nblk, blk), jnp.int32)
            for n2 in range(nblk):
                other = gate[n2:n2 + 1, :]
                ahead = (other > gate) | ((other == gate) & (n2 < blk_id))
                rank = rank + ahead.astype(jnp.int32)
            chosen = jnp.where((rank < MOBA_TOPK) & past, 1.0, 0.0)
            sel.append(jnp.concatenate([chosen] * N_Q_TILES, axis=1))
        kst = _stacked_keys(ks_s, i * blk, (i + 1) * blk)
        s_own = jnp.where(causal, jnp.dot(kst, q_t, preferred_element_type=F32), -jnp.inf)
        return q_t, sel, s_own

    prepared = prepare(0)
    for i in range(nblk):
        rows = slice(i * blk, (i + 1) * blk)
        q_t, sel, s = prepared
        if i + 1 < nblk and i == 0:
            prepared = prepare(i + 1)

        def past_scores(n, q_t=q_t, sel=sel):
            kst = _stacked_keys(ks_s, n * blk, (n + 1) * blk)
            if shifted:
                return jnp.dot(kst, q_t, preferred_element_type=F32)
            ok = [sel[h][n:n + 1, :] > 0.5 for h in range(N_KV_HEADS)]
            bias = jnp.where(bias_row == 0, jnp.where(ok[0], 0.0, MASK_BIAS),
                             jnp.where(bias_row == 1, jnp.where(ok[1], 0.0, MASK_BIAS), 0.0))
            q_ext = jnp.concatenate([q_t, bias.astype(BF16), bias_pad], axis=0)
            k_ext = jnp.concatenate([kst, head_cols], axis=1)
            return jnp.dot(k_ext, q_ext, preferred_element_type=F32)

        s_next = past_scores(0) if i > 0 else None
        if shifted:
            m = [jnp.max(s[h * blk:(h + 1) * blk], axis=0, keepdims=True)
                 for h in range(N_KV_HEADS)]
            p = jnp.concatenate([jnp.exp2(s[h * blk:(h + 1) * blk] - m[h])
                                 for h in range(N_KV_HEADS)], axis=0).astype(BF16)
        else:
            p = jnp.exp2(s).astype(BF16)
        acc, den = _weighted_values(vs_s, i * blk, (i + 1) * blk, p, sum_row)

        for n in range(i):
            s = s_next
            if n + 1 < i:
                s_next = past_scores(n + 1)
            elif i + 1 < nblk:
                prepared = prepare(i + 1)
            if shifted:
                ok = [sel[h][n:n + 1, :] > 0.5 for h in range(N_KV_HEADS)]
                alpha = []
                parts = []
                for h in range(N_KV_HEADS):
                    s_h = s[h * blk:(h + 1) * blk]
                    m_new = jnp.where(ok[h], jnp.maximum(m[h], jnp.max(s_h, axis=0, keepdims=True)),
                                      m[h])
                    alpha.append(jnp.exp2(m[h] - m_new))
                    parts.append(jnp.exp2(s_h - jnp.where(ok[h], m_new, jnp.inf)))
                    m[h] = m_new
                p = jnp.concatenate(parts, axis=0).astype(BF16)
                outs, sums_p = _weighted_values(vs_s, n * blk, (n + 1) * blk, p, sum_row)
                acc = [acc[h] * alpha[h] + outs[h] for h in range(N_KV_HEADS)]
                den = [den[h] * alpha[h] + sums_p[h] for h in range(N_KV_HEADS)]
            else:
                p = jnp.exp2(s).astype(BF16)
                outs, sums_p = _weighted_values(vs_s, n * blk, (n + 1) * blk, p, sum_row)
                acc = [acc[h] + outs[h] for h in range(N_KV_HEADS)]
                den = [den[h] + sums_p[h] for h in range(N_KV_HEADS)]

        o_t = jnp.concatenate([acc[h] * (1.0 / den[h]) for h in range(N_KV_HEADS)], axis=0)
        _store_gated(o_ref, g_ref, rows, o_t, blk)


def _moba(q, qsum, g, kv, shifted):
    b, l, _ = q.shape
    qspec = pl.BlockSpec((1, l, Q_WIDTH), lambda i: (i, 0, 0))
    return pl.pallas_call(
        functools.partial(_moba_kernel, l // MOBA_BLOCK, shifted),
        grid=(b,),
        in_specs=[qspec,
                  pl.BlockSpec((1, l, LANES), lambda i: (i, 0, 0)),
                  qspec,
                  pl.BlockSpec((1, l, 2 * KV_WIDTH), lambda i: (i, 0, 0))],
        out_specs=qspec,
        out_shape=jax.ShapeDtypeStruct((b, l, Q_WIDTH), BF16),
        scratch_shapes=_ATTN_SCRATCH(l),
        compiler_params=_cparams("parallel"),
        name="moba",
    )(q, qsum, g, kv)


def _head_tiles(w):
    lead = w.shape[:-1]
    w = w.reshape(*lead, N_KV_HEADS, N_Q_TILES, HEAD_DIM)
    return jnp.swapaxes(w, -3, -2).reshape(*lead, Q_WIDTH)


def _permute_w_in(w):
    s, q, k = SSM_WIDTH, Q_WIDTH, KV_WIDTH
    bounds = [0]
    for n in (s, s, q, k, k, q, q, k, k, q):
        bounds.append(bounds[-1] + n)
    s_u, s_g, a_q, a_k, a_v, a_g, m_q, m_k, m_v, m_g = [
        w[..., bounds[i]:bounds[i + 1]] for i in range(10)]
    return jnp.concatenate(
        [s_u, s_g, _head_tiles(a_q), _head_tiles(a_g), _head_tiles(m_q), _head_tiles(m_g),
         a_k, a_v, m_k, m_v], axis=-1).astype(BF16)


def _permute_w_out(w):
    s, q = SSM_WIDTH, Q_WIDTH
    rows_t = lambda part: jnp.swapaxes(_head_tiles(jnp.swapaxes(part, -1, -2)), -1, -2)
    return jnp.concatenate([w[:, 0:s], rows_t(w[:, s:s + q]), rows_t(w[:, s + q:])],
                           axis=1).astype(BF16)


def _block_diag_in(bb):
    d, g, h, p = bb.shape
    eye = jnp.eye(g, dtype=bb.dtype)[None, :, None, :, None]
    return (bb[:, :, :, None, :] * eye).reshape(d, g * h, g * p)


def _block_diag_out(c):
    d, g, h, p = c.shape
    eye = jnp.eye(g, dtype=c.dtype)[None, :, None, :, None]
    return (jnp.swapaxes(c, -1, -2)[:, :, :, None, :] * eye).reshape(d, g * p, g * h)


def _logit_bound(q_gain, k_gain):
    return ((HEAD_DIM * ATTN_SCALE * LOG2E) * jnp.max(jnp.abs(q_gain), axis=-1)
            * jnp.max(jnp.abs(k_gain), axis=-1))


def kernel(x, norm_g, w_in, ssm_lam_re, ssm_lam_im, ssm_log_dt, ssm_b_re, ssm_b_im,
           ssm_c_re, ssm_c_im, ssm_d, ssm_glu_w, ssm_glu_b, swa_q_norm, swa_k_norm,
           swa_sink, moba_q_norm, moba_k_norm, w_out):
    b, l, d = x.shape
    depth = norm_g.shape[0]
    tiles = _tiles(l)
    a_re, a_im, bb_re, bb_im = _ssm_prep(ssm_lam_re, ssm_lam_im, ssm_log_dt, ssm_b_re, ssm_b_im)
    two = lambda v: jnp.concatenate([v, v], axis=-1).reshape(depth, 1, LANES).astype(F32)
    edge_params = {
        "norm_g": norm_g.reshape(depth, 1, d).astype(F32),
        "w_in": _permute_w_in(w_in),
        "w_out": _permute_w_out(w_out),
        "head_norms": [two(v) for v in (swa_q_norm, swa_k_norm, moba_q_norm, moba_k_norm)],
    }
    by_slab_cols = lambda m: m.reshape(depth, SSM_WIDTH, RE_SLABS, LANES)
    bmat = jnp.concatenate([by_slab_cols(_block_diag_in(bb_re)), by_slab_cols(_block_diag_in(bb_im))],
                           axis=-1).transpose(0, 2, 1, 3).astype(BF16)
    by_slab_rows = lambda m: m.reshape(depth, RE_SLABS, LANES, SSM_WIDTH)
    cmat = jnp.concatenate([by_slab_rows(_block_diag_out(ssm_c_re)),
                            -by_slab_rows(_block_diag_out(ssm_c_im))], axis=2).astype(BF16)
    ssm_params = (
        bmat, a_re.reshape(depth, 1, N_STATE), a_im.reshape(depth, 1, N_STATE), cmat,
        ssm_d.reshape(depth, 1, SSM_WIDTH).astype(F32), ssm_glu_w.astype(BF16),
        ssm_glu_b.reshape(depth, 1, SSM_WIDTH).astype(F32))
    sink = swa_sink.astype(F32)
    small = ((_logit_bound(swa_q_norm, swa_k_norm) <= LOGIT_BOUND)
             & (jnp.max(jnp.abs(sink), axis=-1) * LOG2E <= LOGIT_BOUND)
             & (_logit_bound(moba_q_norm, moba_k_norm) <= LOGIT_BOUND))

    def attention(layer, shifted, a_q, a_g, a_kv, m_q, m_qsum, m_g, m_kv, sink):
        return (_swa(a_q, a_g, a_kv, sink, layer, shifted), _moba(m_q, m_qsum, m_g, m_kv, shifted))

    proj = _layer_edge(x, None, edge_params, None, 0, tiles)
    for layer in range(depth):
        s_u, s_g, a_q, a_g, m_q, m_qsum, m_g, a_kv, m_kv = proj
        y_swa, y_moba = lax.cond(small[layer], functools.partial(attention, layer, False),
                                 functools.partial(attention, layer, True),
                                 a_q, a_g, a_kv, m_q, m_qsum, m_g, m_kv, sink)
        branch_outs = (_ssm(s_u, s_g, ssm_params, layer, tiles["scan_steps"]), y_swa, y_moba)
        nxt = layer + 1 if layer + 1 < depth else None
        x, *proj = _layer_edge(x, branch_outs, edge_params, layer, nxt, tiles)
    return x
```
